```python
import math
import jax
import jax.numpy as jnp
from jax import lax
import numpy as np

D_MODEL = 2048
BATCH = 4
SEQ = 2048
DEPTH = 2
DEC_BATCH = 128
DEC_SEQ = 8
PAST_LEN = 2048
PAGE_SIZE = 128

N_BRANCH = 4
W_BRANCH = D_MODEL // 4
W_POOL = W_BRANCH
POOL_WINDOWS = (2, 4, 8, 16)
POOL_GROUP = W_POOL // len(POOL_WINDOWS)
POOL_BUF = max(POOL_WINDOWS) - 1
W_CONV = W_BRANCH
CONV_WIDTH = 3
N_HEADS_C = 8
HEAD_DIM_C = W_BRANCH // N_HEADS_C
W_ATTN = N_HEADS_C * HEAD_DIM_C
ATTN_SCALE = HEAD_DIM_C ** -0.5
IDX_HEADS = 8
IDX_DIM = 64
IDX_SCALE = IDX_DIM ** -0.5
IDX_W_SCALE = IDX_HEADS ** -0.5
TOPK_MAX = 256
Q_BLOCK = 128
W_SSM = W_BRANCH
SSM_GROUP_CH = 16
SSM_GROUPS = W_SSM // SSM_GROUP_CH
SSM_STATE = 64
STEP_MIN = 1e-3
STEP_MAX = 1e-1
X_HEADS = 4
X_HEAD_DIM = 128
X_SCALE = X_HEAD_DIM ** -0.5
N_MEM = 256
D_FF = 2 * D_MODEL
IN_SIZES = (W_POOL, W_CONV, W_CONV, W_CONV, W_ATTN, HEAD_DIM_C, HEAD_DIM_C,
            IDX_HEADS * IDX_DIM, IDX_DIM, IDX_HEADS, W_SSM)
N_IN = sum(IN_SIZES)
EPS = 1e-6

kernel_name = 'hybrid_pool_conv_dsa_s5_decoder_step'


def rmsnorm(x, g):
    xf = x.astype(jnp.float32)
    y = xf * lax.rsqrt(jnp.mean(xf * xf, axis=-1, keepdims=True) + EPS)
    return (y * g.astype(jnp.float32)).astype(x.dtype)


def swiglu(x, w_in, w_out):
    a, b = jnp.split(x @ w_in, 2, axis=-1)
    return (jax.nn.silu(a) * b) @ w_out


def split_columns(z):
    parts, start = [], 0
    for n in IN_SIZES:
        parts.append(z[..., start:start + n])
        start += n
    return parts


def pool_mixer(xp, buf, pos0, w_mix, scale):
    Bt, T, W = xp.shape
    full = jnp.concatenate([buf.astype(xp.dtype), xp], axis=1).astype(jnp.float32)
    cs = jnp.concatenate([jnp.zeros((Bt, 1, W), jnp.float32), jnp.cumsum(full, axis=1)], axis=1)
    pos = pos0 + jnp.arange(T)
    outs = []
    for gi, w in enumerate(POOL_WINDOWS):
        sl = slice(gi * POOL_GROUP, (gi + 1) * POOL_GROUP)
        hi = cs[:, POOL_BUF + 1:POOL_BUF + 1 + T, sl]
        lo = cs[:, POOL_BUF + 1 - w:POOL_BUF + 1 - w + T, sl]
        cnt = jnp.minimum(pos + 1, w).astype(jnp.float32)[None, :, None]
        outs.append((hi - lo) / cnt - full[:, POOL_BUF:, sl])
    d = jnp.stack(outs, axis=2)
    y = jnp.einsum('btgc,gcd->btgd', d, w_mix.astype(jnp.float32)).reshape(Bt, T, W)
    y = y * scale.astype(jnp.float32)
    return y.astype(xp.dtype), full[:, -POOL_BUF:].astype(xp.dtype)


def conv_mixer(xc, bg, cg, buf, conv_w):
    T = xc.shape[1]
    v = cg * xc
    full = jnp.concatenate([buf.astype(v.dtype), v], axis=1)
    y = conv_w[0] * full[:, 0:T]
    for j in range(1, CONV_WIDTH):
        y = y + conv_w[j] * full[:, j:j + T]
    return bg * y, full[:, -(CONV_WIDTH - 1):]


def sparse_attention(q, k_all, v_all, iq, ik_all, iw, q_pos):
    L = k_all.shape[1]
    n_sel = min(TOPK_MAX, L // 4)
    causal = jnp.arange(L)[None, :] <= q_pos[:, None]
    idx_logits = jnp.einsum('bthd,bsd->bths', iq, ik_all, preferred_element_type=jnp.float32) * IDX_SCALE
    score = jnp.einsum('bths,bth->bts', jax.nn.relu(idx_logits), iw.astype(jnp.float32))
    score = jnp.where(causal[None], score, -jnp.inf)
    _, sel = lax.top_k(score, n_sel)
    valid = sel <= q_pos[None, :, None]
    gather = jax.vmap(lambda rows, ids: rows[ids])
    k_sel = gather(k_all, sel)
    v_sel = gather(v_all, sel)
    logits = jnp.einsum('bthd,btkd->bthk', q, k_sel, preferred_element_type=jnp.float32) * ATTN_SCALE
    logits = jnp.where(valid[:, :, None, :], logits, -jnp.inf)
    p = jax.nn.softmax(logits, axis=-1).astype(v_sel.dtype)
    return jnp.einsum('bthk,btkd->bthd', p, v_sel)


def sparse_attention_blocked(q, k_all, v_all, iq, ik_all, iw, q_pos):
    T = q.shape[1]
    if T <= Q_BLOCK:
        return sparse_attention(q, k_all, v_all, iq, ik_all, iw, q_pos)
    nb = T // Q_BLOCK

    def blk(a):
        return jnp.moveaxis(a.reshape(a.shape[0], nb, Q_BLOCK, *a.shape[2:]), 1, 0)

    outs = lax.map(
        lambda xs: sparse_attention(xs[0], k_all, v_all, xs[1], ik_all, xs[2], xs[3]),
        (blk(q), blk(iq), blk(iw), q_pos.reshape(nb, Q_BLOCK)))
    return jnp.moveaxis(outs, 0, 1).reshape(q.shape)


def complex_affine_combine(e1, e2):
    ar1, ai1, br1, bi1 = e1
    ar2, ai2, br2, bi2 = e2
    return (ar1 * ar2 - ai1 * ai2,
            ar1 * ai2 + ai1 * ar2,
            ar2 * br1 - ai2 * bi1 + br2,
            ar2 * bi1 + ai2 * br1 + bi2)


def ssm_mixer(xs, s_re0, s_im0, a_re, a_im, log_step, b_re, b_im, c_re, c_im, d, glu_w, glu_b):
    f32 = jnp.float32
    Bt, T, W = xs.shape
    uf = xs.astype(f32).reshape(Bt, T, SSM_GROUPS, SSM_GROUP_CH)
    a_re = a_re.astype(f32)
    a_im = a_im.astype(f32)
    step = jnp.exp(log_step.astype(f32))[:, None]
    decay = jnp.exp(step * a_re)
    ab_re = decay * jnp.cos(step * a_im)
    ab_im = decay * jnp.sin(step * a_im)
    den = a_re * a_re + a_im * a_im
    nr = ab_re - 1.0
    co_re = (nr * a_re + ab_im * a_im) / den
    co_im = (ab_im * a_re - nr * a_im) / den
    b_re = b_re.astype(f32)
    b_im = b_im.astype(f32)
    bb_re = co_re[..., None] * b_re - co_im[..., None] * b_im
    bb_im = co_re[..., None] * b_im + co_im[..., None] * b_re
    bu_re = jnp.einsum('btgh,gnh->btgn', uf, bb_re)
    bu_im = jnp.einsum('btgh,gnh->btgn', uf, bb_im)
    s_re0 = s_re0.astype(f32)
    s_im0 = s_im0.astype(f32)
    bu_re = bu_re.at[:, 0].add(ab_re * s_re0 - ab_im * s_im0)
    bu_im = bu_im.at[:, 0].add(ab_re * s_im0 + ab_im * s_re0)
    ar = jnp.broadcast_to(ab_re, bu_re.shape)
    ai = jnp.broadcast_to(ab_im, bu_im.shape)
    _, _, h_re, h_im = lax.associative_scan(complex_affine_combine, (ar, ai, bu_re, bu_im), axis=1)
    y = (jnp.einsum('btgn,ghn->btgh', h_re, c_re.astype(f32))
         - jnp.einsum('btgn,ghn->btgh', h_im, c_im.astype(f32))).reshape(Bt, T, W)
    y = y + d.astype(f32) * xs.astype(f32)
    z = jax.nn.gelu(y)
    out = z * jax.nn.sigmoid(z @ glu_w.astype(f32) + glu_b.astype(f32))
    return out.astype(xs.dtype), h_re[:, -1], h_im[:, -1]


def memory_kv(mem, g_mem, w_xk, w_xv, g_xk):
    Bt, M, _ = mem.shape
    m = rmsnorm(mem, g_mem)
    k = rmsnorm((m @ w_xk).reshape(Bt, M, X_HEADS, X_HEAD_DIM), g_xk)
    v = (m @ w_xv).reshape(Bt, M, X_HEADS, X_HEAD_DIM)
    return k, v


def cross_attention(h, mem_k, mem_v, w_xq, g_xq, w_xo):
    Bt, T, _ = h.shape
    q = rmsnorm((h @ w_xq).reshape(Bt, T, X_HEADS, X_HEAD_DIM), g_xq)
    s = jnp.einsum('bthd,bmhd->bhtm', q, mem_k.astype(q.dtype), preferred_element_type=jnp.float32) * X_SCALE
    p = jax.nn.softmax(s, axis=-1).astype(mem_v.dtype)
    o = jnp.einsum('bhtm,bmhd->bthd', p, mem_v)
    return o.reshape(Bt, T, X_HEADS * X_HEAD_DIM).astype(h.dtype) @ w_xo


def decoder_layer(x, pos0, past_k, past_v, past_ik, pool_buf, conv_buf, ssm_re, ssm_im, mem_k, mem_v, p):
    Bt, T, _ = x.shape
    h = x + 0.5 * swiglu(rmsnorm(x, p['norm_g'][0]), p['ffn_in'][0], p['ffn_out'][0])
    u = rmsnorm(h, p['norm_g'][1])
    xp, xc, bg, cg, q, k, v, iq, ik, iw, xs = split_columns(u @ p['w_in'])
    ya, new_pool = pool_mixer(xp, pool_buf, pos0, p['pool_mix'], p['pool_scale'])
    yb, new_conv = conv_mixer(xc, bg, cg, conv_buf, p['conv_w'])
    q = rmsnorm(q.reshape(Bt, T, N_HEADS_C, HEAD_DIM_C), p['q_norm_g'])
    k = rmsnorm(k, p['k_norm_g'])
    k_all = jnp.concatenate([past_k.astype(k.dtype), k], axis=1)
    v_all = jnp.concatenate([past_v.astype(v.dtype), v], axis=1)
    ik_all = jnp.concatenate([past_ik.astype(ik.dtype), ik], axis=1)
    q_pos = pos0 + jnp.arange(T)
    yc = sparse_attention_blocked(q, k_all, v_all, iq.reshape(Bt, T, IDX_HEADS, IDX_DIM),
                                  ik_all, iw * IDX_W_SCALE, q_pos).reshape(Bt, T, W_ATTN)
    yd, s_re, s_im = ssm_mixer(xs, ssm_re, ssm_im, p['ssm_a_re'], p['ssm_a_im'], p['ssm_log_step'],
                               p['ssm_b_re'], p['ssm_b_im'], p['ssm_c_re'], p['ssm_c_im'],
                               p['ssm_d'], p['ssm_glu_w'], p['ssm_glu_b'])
    br = jnp.stack([ya, yb, yc.astype(ya.dtype), yd], axis=2)
    outs = jnp.einsum('btiw,iwd->btid', br, p['w_branch'])
    gates = jax.nn.sigmoid((u @ p['w_gate'] + p['b_gate']).reshape(Bt, T, N_BRANCH, D_MODEL))
    h = h + jnp.sum(gates * outs, axis=2) @ p['w_o']
    h = h + cross_attention(rmsnorm(h, p['norm_g'][2]), mem_k, mem_v, p['w_xq'], p['xq_norm_g'], p['w_xo'])
    h = h + 0.5 * swiglu(rmsnorm(h, p['norm_g'][3]), p['ffn_in'][1], p['ffn_out'][1])
    return h, (k, v, ik, new_pool, new_conv, s_re, s_im)


def setup_inputs(seed: int = 0) -> dict:
    key = jax.random.key(seed)
    ks = iter(jax.random.split(key, 64))
    f32 = jnp.float32

    def nrm(shape, scale):
        return jax.random.normal(next(ks), shape, f32) * scale

    def gain(shape):
        return 1.0 + nrm(shape, 0.01)

    n_pages = PAST_LEN // PAGE_SIZE
    n_pool = (DEC_BATCH * n_pages * 5) // 4
    x_prompt = nrm((BATCH, SEQ, D_MODEL), 1.0)
    x_sample = nrm((DEC_BATCH, DEC_SEQ, D_MODEL), 1.0)
    cache_attn_k = nrm((DEPTH, n_pool, PAGE_SIZE, HEAD_DIM_C), 1.0)
    cache_attn_v = nrm((DEPTH, n_pool, PAGE_SIZE, HEAD_DIM_C), 1.0)
    cache_idx_k = nrm((DEPTH, n_pool, PAGE_SIZE, IDX_DIM), 1.0)
    cache_mem_k = nrm((DEPTH, DEC_BATCH, N_MEM, X_HEADS, X_HEAD_DIM), 1.0)
    cache_mem_v = nrm((DEPTH, DEC_BATCH, N_MEM, X_HEADS, X_HEAD_DIM), 1.0)
    state_pool = nrm((DEPTH, DEC_BATCH, POOL_BUF, W_POOL), 1.0)
    state_conv = nrm((DEPTH, DEC_BATCH, CONV_WIDTH - 1, W_CONV), 1.0)
    state_ssm_re = nrm((DEPTH, DEC_BATCH, SSM_GROUPS, SSM_STATE), 0.1)
    state_ssm_im = nrm((DEPTH, DEC_BATCH, SSM_GROUPS, SSM_STATE), 0.1)
    perm = jax.random.permutation(next(ks), n_pool)
    page_table = perm[:DEC_BATCH * n_pages].reshape(DEC_BATCH, n_pages).astype(jnp.int32)
    mem_prompt = nrm((BATCH, N_MEM, D_MODEL), 1.0)
    return {
        'x_prompt': x_prompt,
        'x_sample': x_sample,
        'cache_attn_k': cache_attn_k,
        'cache_attn_v': cache_attn_v,
        'cache_idx_k': cache_idx_k,
        'cache_mem_k': cache_mem_k,
        'cache_mem_v': cache_mem_v,
        'state_pool': state_pool,
        'state_conv': state_conv,
        'state_ssm_re': state_ssm_re,
        'state_ssm_im': state_ssm_im,
        'page_table': page_table,
        'mem_prompt': mem_prompt,
        'norm_g': gain((DEPTH, 4, D_MODEL)),
        'ffn_in': nrm((DEPTH, 2, D_MODEL, 2 * D_FF), D_MODEL ** -0.5),
        'ffn_out': nrm((DEPTH, 2, D_FF, D_MODEL), D_FF ** -0.5),
        'w_in': nrm((DEPTH, D_MODEL, N_IN), D_MODEL ** -0.5),
        'q_norm_g': gain((DEPTH, HEAD_DIM_C)),
        'k_norm_g': gain((DEPTH, HEAD_DIM_C)),
        'pool_mix': nrm((DEPTH, len(POOL_WINDOWS), POOL_GROUP, POOL_GROUP), POOL_GROUP ** -0.5),
        'pool_scale': gain((DEPTH, W_POOL)),
        'conv_w': nrm((DEPTH, CONV_WIDTH, W_CONV), CONV_WIDTH ** -0.5),
        'ssm_a_re': -0.5 + nrm((DEPTH, SSM_GROUPS, SSM_STATE), 0.01),
        'ssm_a_im': math.pi * jnp.arange(SSM_STATE, dtype=f32) + nrm((DEPTH, SSM_GROUPS, SSM_STATE), 0.01),
        'ssm_log_step': jax.random.uniform(next(ks), (DEPTH, SSM_GROUPS), f32,
                                           math.log(STEP_MIN), math.log(STEP_MAX)),
        'ssm_b_re': nrm((DEPTH, SSM_GROUPS, SSM_STATE, SSM_GROUP_CH), (2 * SSM_GROUP_CH) ** -0.5),
        'ssm_b_im': nrm((DEPTH, SSM_GROUPS, SSM_STATE, SSM_GROUP_CH), (2 * SSM_GROUP_CH) ** -0.5),
        'ssm_c_re': nrm((DEPTH, SSM_GROUPS, SSM_GROUP_CH, SSM_STATE), (2 * SSM_STATE) ** -0.5),
        'ssm_c_im': nrm((DEPTH, SSM_GROUPS, SSM_GROUP_CH, SSM_STATE), (2 * SSM_STATE) ** -0.5),
        'ssm_d': nrm((DEPTH, W_SSM), 1.0),
        'ssm_glu_w': nrm((DEPTH, W_SSM, W_SSM), W_SSM ** -0.5),
        'ssm_glu_b': nrm((DEPTH, W_SSM), 0.01),
        'w_branch': nrm((DEPTH, N_BRANCH, W_BRANCH, D_MODEL), W_BRANCH ** -0.5),
        'w_gate': nrm((DEPTH, D_MODEL, N_BRANCH * D_MODEL), D_MODEL ** -0.5),
        'b_gate': nrm((DEPTH, N_BRANCH * D_MODEL), 0.01),
        'w_o': nrm((DEPTH, D_MODEL, D_MODEL), 0.5 * D_MODEL ** -0.5),
        'mem_norm_g': gain((DEPTH, D_MODEL)),
        'w_xq': nrm((DEPTH, D_MODEL, X_HEADS * X_HEAD_DIM), D_MODEL ** -0.5),
        'w_xk': nrm((DEPTH, D_MODEL, X_HEADS * X_HEAD_DIM), D_MODEL ** -0.5),
        'w_xv': nrm((DEPTH, D_MODEL, X_HEADS * X_HEAD_DIM), D_MODEL ** -0.5),
        'xq_norm_g': gain((DEPTH, X_HEAD_DIM)),
        'xk_norm_g': gain((DEPTH, X_HEAD_DIM)),
        'w_xo': nrm((DEPTH, X_HEADS * X_HEAD_DIM, D_MODEL), (X_HEADS * X_HEAD_DIM) ** -0.5),
    }


def reference(x_prompt, x_sample, cache_attn_k, cache_attn_v, cache_idx_k, cache_mem_k, cache_mem_v,
              state_pool, state_conv, state_ssm_re, state_ssm_im, page_table, mem_prompt,
              norm_g, ffn_in, ffn_out, w_in, q_norm_g, k_norm_g, pool_mix, pool_scale, conv_w,
              ssm_a_re, ssm_a_im, ssm_log_step, ssm_b_re, ssm_b_im, ssm_c_re, ssm_c_im, ssm_d,
              ssm_glu_w, ssm_glu_b, w_branch, w_gate, b_gate, w_o, mem_norm_g, w_xq, w_xk, w_xv,
              xq_norm_g, xk_norm_g, w_xo):
    Bp = x_prompt.shape[0]
    Bs = x_sample.shape[0]
    dt = x_prompt.dtype
    hp, hs = x_prompt, x_sample
    pk, pv, pik, pmk, pmv, ppool, pconv, pre, pim = [], [], [], [], [], [], [], [], []
    sk, sv, sik, spool, sconv, sre, sim = [], [], [], [], [], [], []
    for l in range(DEPTH):
        p = {
            'norm_g': norm_g[l], 'ffn_in': ffn_in[l], 'ffn_out': ffn_out[l], 'w_in': w_in[l],
            'q_norm_g': q_norm_g[l], 'k_norm_g': k_norm_g[l], 'pool_mix': pool_mix[l],
            'pool_scale': pool_scale[l], 'conv_w': conv_w[l], 'ssm_a_re': ssm_a_re[l],
            'ssm_a_im': ssm_a_im[l], 'ssm_log_step': ssm_log_step[l], 'ssm_b_re': ssm_b_re[l],
            'ssm_b_im': ssm_b_im[l], 'ssm_c_re': ssm_c_re[l], 'ssm_c_im': ssm_c_im[l],
            'ssm_d': ssm_d[l], 'ssm_glu_w': ssm_glu_w[l], 'ssm_glu_b': ssm_glu_b[l],
            'w_branch': w_branch[l], 'w_gate': w_gate[l], 'b_gate': b_gate[l], 'w_o': w_o[l],
            'w_xq': w_xq[l], 'xq_norm_g': xq_norm_g[l], 'w_xo': w_xo[l],
        }
        mk, mv = memory_kv(mem_prompt, mem_norm_g[l], w_xk[l], w_xv[l], xk_norm_g[l])
        hp, st = decoder_layer(
            hp, 0,
            jnp.zeros((Bp, 0, HEAD_DIM_C), dt), jnp.zeros((Bp, 0, HEAD_DIM_C), dt),
            jnp.zeros((Bp, 0, IDX_DIM), dt),
            jnp.zeros((Bp, POOL_BUF, W_POOL), dt), jnp.zeros((Bp, CONV_WIDTH - 1, W_CONV), dt),
            jnp.zeros((Bp, SSM_GROUPS, SSM_STATE), jnp.float32),
            jnp.zeros((Bp, SSM_GROUPS, SSM_STATE), jnp.float32),
            mk, mv, p)
        pk.append(st[0]); pv.append(st[1]); pik.append(st[2]); ppool.append(st[3])
        pconv.append(st[4]); pre.append(st[5]); pim.append(st[6]); pmk.append(mk); pmv.append(mv)
        past_k = cache_attn_k[l][page_table].reshape(Bs, -1, HEAD_DIM_C)
        past_v = cache_attn_v[l][page_table].reshape(Bs, -1, HEAD_DIM_C)
        past_ik = cache_idx_k[l][page_table].reshape(Bs, -1, IDX_DIM)
        hs, st = decoder_layer(
            hs, past_k.shape[1], past_k, past_v, past_ik,
            state_pool[l], state_conv[l], state_ssm_re[l], state_ssm_im[l],
            cache_mem_k[l], cache_mem_v[l], p)
        sk.append(st[0]); sv.append(st[1]); sik.append(st[2]); spool.append(st[3])
        sconv.append(st[4]); sre.append(st[5]); sim.append(st[6])
    return (hp, hs,
            jnp.stack(pk), jnp.stack(pv), jnp.stack(pik), jnp.stack(pmk), jnp.stack(pmv),
            jnp.stack(ppool), jnp.stack(pconv), jnp.stack(pre), jnp.stack(pim),
            jnp.stack(sk), jnp.stack(sv), jnp.stack(sik), jnp.stack(spool), jnp.stack(sconv),
            jnp.stack(sre), jnp.stack(sim))
```

```python
import functools
import math

import jax
import jax.numpy as jnp
from jax import lax
from jax.experimental import pallas as pl
from jax.experimental.pallas import tpu as pltpu

F32, BF16, I32 = jnp.float32, jnp.bfloat16, jnp.int32
EPS = 1e-6
NEG_INF = float("-inf")

V7X_VMEM_BYTES = 64 * 1024 * 1024
VMEM_LIMIT = V7X_VMEM_BYTES - 8 * 1024 * 1024
LANES = 128
SUBLANES = 8

POOL_WINDOWS = (2, 4, 8, 16)
POOL_HIST = 16
CONV_WIDTH = 3
CONV_HIST = 8
N_HEADS_C = 8
IDX_HEADS = 8
TOPK_MAX = 256
X_HEADS = 4
SSM_CHUNK = 128


def _cparams(sem, vmem=VMEM_LIMIT):
    return pltpu.CompilerParams(dimension_semantics=sem, vmem_limit_bytes=vmem)


def _dot(a, b):
    return jnp.dot(a, b, preferred_element_type=F32)


def _dot_nt(a, b):
    return lax.dot_general(a, b, (((1,), (1,)), ((), ())), preferred_element_type=F32)


def _rmsnorm_kernel(x_ref, g_ref, o_ref):
    x = x_ref[...]
    ms = jnp.mean(x * x, axis=-1, keepdims=True)
    o_ref[...] = (x * lax.rsqrt(ms + EPS) * g_ref[...]).astype(o_ref.dtype)


def rmsnorm_rows(x, g, tm=512):
    M, D = x.shape
    return pl.pallas_call(
        _rmsnorm_kernel,
        grid=(M // tm,),
        in_specs=[pl.BlockSpec((tm, D), lambda i: (i, 0)), pl.BlockSpec((1, D), lambda i: (0, 0))],
        out_specs=pl.BlockSpec((tm, D), lambda i: (i, 0)),
        out_shape=jax.ShapeDtypeStruct((M, D), BF16),
        compiler_params=_cparams(("arbitrary",)),
        name="rmsnorm",
    )(x, g.reshape(1, D))


def _mm_kernel(*refs, nw, mode, scale, sub, group):
    x_ref = refs[0]
    w_refs = refs[1:1 + nw]
    p = 1 + nw
    aux_ref = None
    if mode in ("bias_sigmoid", "residual", "headnorm"):
        aux_ref = refs[p]
        p += 1
    o_ref = refs[p]
    scr = refs[p + 1:p + 1 + nw]

    @pl.when(pl.program_id(1) == 0)
    def _():
        for w_ref, s in zip(w_refs, scr):
            s[...] = w_ref[...].astype(BF16)

    tm = x_ref.shape[0]

    def body(r, carry):
        rows = pl.ds(pl.multiple_of(r * sub, sub), sub)
        x = x_ref[rows, :]
        acc = [_dot(x, s[...]) for s in scr]
        if mode == "swiglu":
            a, b = acc
            y = (a * jax.nn.sigmoid(a)) * b
        elif mode == "bias_sigmoid":
            y = jax.nn.sigmoid(acc[0] + aux_ref[...])
        elif mode == "residual":
            y = aux_ref[rows, :] + scale * acc[0]
        elif mode == "headnorm":
            a = acc[0]
            parts = []
            for h in range(a.shape[1] // group):
                ah = a[:, h * group:(h + 1) * group]
                ms = jnp.mean(ah * ah, axis=-1, keepdims=True)
                parts.append(ah * lax.rsqrt(ms + EPS))
            y = jnp.concatenate(parts, axis=1) * aux_ref[...]
        else:
            y = acc[0]
        o_ref[rows, :] = y.astype(o_ref.dtype)
        return carry

    lax.fori_loop(0, tm // sub, body, 0)


def matmul(x, weights, *, n_out, tn, tm, mode="plain", aux=None, scale=1.0, out_dtype=F32, group=LANES,
           name="matmul"):
    M, K = x.shape
    nw = len(weights)
    sub = min(tm, 256)
    in_specs = [pl.BlockSpec((tm, K), lambda j, i: (i, 0))]
    args = [x]
    for arr, lead, coff in weights:
        nl = len(lead)
        in_specs.append(pl.BlockSpec((None,) * nl + (K, tn),
                                     functools.partial(lambda j, i, lead, coff: (*lead, 0, coff + j), lead=lead, coff=coff)))
        args.append(arr)
    if mode in ("bias_sigmoid", "headnorm"):
        in_specs.append(pl.BlockSpec((1, tn), lambda j, i: (0, j)))
        args.append(aux)
    elif mode == "residual":
        in_specs.append(pl.BlockSpec((tm, tn), lambda j, i: (i, j)))
        args.append(aux)
    return pl.pallas_call(
        functools.partial(_mm_kernel, nw=nw, mode=mode, scale=scale, sub=sub, group=group),
        grid=(n_out // tn, M // tm),
        in_specs=in_specs,
        out_specs=pl.BlockSpec((tm, tn), lambda j, i: (i, j)),
        out_shape=jax.ShapeDtypeStruct((M, n_out), out_dtype),
        scratch_shapes=[pltpu.VMEM((K, tn), BF16) for _ in range(nw)],
        compiler_params=_cparams(("arbitrary", "arbitrary")),
        name=name,
    )(*args)


def _merge_kernel(*refs, nb):
    br = refs[0:nb]
    gt = refs[nb:2 * nb]
    ws = refs[2 * nb:3 * nb]
    o_ref = refs[3 * nb]
    scr = refs[3 * nb + 1:3 * nb + 1 + nb]

    @pl.when(pl.program_id(1) == 0)
    def _():
        for w_ref, s in zip(ws, scr):
            s[...] = w_ref[...].astype(BF16)

    tm = o_ref.shape[0]
    sub = min(tm, 256)

    def body(r, carry):
        rows = pl.ds(pl.multiple_of(r * sub, sub), sub)
        acc = None
        for k in range(nb):
            t = gt[k][rows, :].astype(F32) * _dot(br[k][rows, :], scr[k][...])
            acc = t if acc is None else acc + t
        o_ref[rows, :] = acc.astype(o_ref.dtype)
        return carry

    lax.fori_loop(0, tm // sub, body, 0)


def gated_merge(branches, gates, w_branch, layer, *, tn=512, tm=1024):
    nb = len(branches)
    M, W = branches[0].shape
    D = w_branch.shape[-1]
    nj = D // tn
    in_specs = [pl.BlockSpec((tm, W), lambda j, i: (i, 0)) for _ in range(nb)]
    in_specs += [pl.BlockSpec((tm, tn), functools.partial(lambda j, i, k: (i, k * nj + j), k=k)) for k in range(nb)]
    in_specs += [pl.BlockSpec((None, None, W, tn), functools.partial(lambda j, i, k: (layer, k, 0, j), k=k))
                 for k in range(nb)]
    return pl.pallas_call(
        functools.partial(_merge_kernel, nb=nb),
        grid=(nj, M // tm),
        in_specs=in_specs,
        out_specs=pl.BlockSpec((tm, tn), lambda j, i: (i, j)),
        out_shape=jax.ShapeDtypeStruct((M, D), BF16),
        scratch_shapes=[pltpu.VMEM((W, tn), BF16) for _ in range(nb)],
        compiler_params=_cparams(("arbitrary", "arbitrary")),
        name="gated_merge",
    )(*branches, *([gates] * nb), *([w_branch] * nb))


def _kvprep_kernel(t_ref, g_ref, k_ref, v_ref, ik_ref, *, dh):
    t = t_ref[...]
    k = t[:, 0:dh]
    ms = jnp.mean(k * k, axis=-1, keepdims=True)
    k_ref[...] = k * lax.rsqrt(ms + EPS) * g_ref[...]
    v_ref[...] = t[:, dh:2 * dh]
    ik_ref[...] = t[:, 2 * dh:3 * dh]


def kv_prep(z, tail_block, tail_w, gk, dh, tm=1024):
    M = z.shape[0]
    out = jax.ShapeDtypeStruct((M, dh), F32)
    return pl.pallas_call(
        functools.partial(_kvprep_kernel, dh=dh),
        grid=(M // tm,),
        in_specs=[pl.BlockSpec((tm, tail_w), lambda i: (i, tail_block)), pl.BlockSpec((1, dh), lambda i: (0, 0))],
        out_specs=[pl.BlockSpec((tm, dh), lambda i: (i, 0))] * 3,
        out_shape=[out, out, out],
        compiler_params=_cparams(("arbitrary",)),
        name="kv_prep",
    )(z, gk.reshape(1, dh))


def _poolconv_kernel(*refs, G, T, W, pos0, has_state):
    xp_ref, xc_ref, bg_ref, cg_ref = refs[0:4]
    p = 4
    if has_state:
        pbuf_ref, cbuf_ref = refs[4:6]
        p = 6
    wmix_ref, pscale_ref, convw_ref = refs[p:p + 3]
    ya_ref, yb_ref, npool_ref, nconv_ref = refs[p + 3:p + 7]
    fullp, fullc = refs[p + 7:p + 9]
    PH, CH = POOL_HIST, CONV_HIST
    nh = CONV_WIDTH - 1
    gw = W // len(POOL_WINDOWS)
    c = pl.program_id(1)

    @pl.when(c == 0)
    def _():
        if has_state:
            fullp[:, 1:PH, :] = pbuf_ref[...]
            fullc[:, CH - nh:CH, :] = cbuf_ref[...]
        else:
            fullp[:, 0:PH, :] = jnp.zeros((G, PH, W), F32)
            fullc[:, 0:CH, :] = jnp.zeros((G, CH, W), F32)

    @pl.when(c > 0)
    def _():
        fullp[:, 0:PH, :] = fullp[:, T:T + PH, :]
        fullc[:, 0:CH, :] = fullc[:, T:T + CH, :]

    fullp[:, PH:PH + T, :] = xp_ref[...].reshape(G, T, W)
    t_idx = lax.broadcasted_iota(I32, (1, T, 1), 1) + (c * T + (pos0 + 1))
    for gi, w in enumerate(POOL_WINDOWS):
        cols = slice(gi * gw, (gi + 1) * gw)
        acc = fullp[:, PH:PH + T, cols]
        for j in range(1, w):
            acc = acc + fullp[:, PH - j:PH - j + T, cols]
        cnt = jnp.minimum(t_idx, w).astype(F32)
        d = acc / cnt - fullp[:, PH:PH + T, cols]
        y = _dot(d.reshape(G * T, gw).astype(BF16), wmix_ref[gi]) * pscale_ref[:, cols]
        ya_ref[:, cols] = y.astype(ya_ref.dtype)
    npool_ref[...] = fullp[:, T + 1:T + PH, :]

    fullc[:, CH:CH + T, :] = (cg_ref[...] * xc_ref[...]).reshape(G, T, W)
    y = None
    for j in range(CONV_WIDTH):
        wj = convw_ref[j:j + 1, :].reshape(1, 1, W)
        term = wj * fullc[:, CH - nh + j:CH - nh + j + T, :]
        y = term if y is None else y + term
    yb_ref[...] = (bg_ref[...] * y.reshape(G * T, W)).astype(yb_ref.dtype)
    nconv_ref[...] = fullc[:, CH + T - nh:CH + T, :]


def pool_conv(z, row0, Bt, T, G, TC, pos0, pool_buf, conv_buf, wmix_bf16, pool_scale, conv_w, W):
    has_state = pool_buf is not None
    assert G == 1 or TC == T
    R = G * TC
    nc = T // TC
    rb0 = row0 // R
    nh = CONV_WIDTH - 1

    def zspec(cb):
        return pl.BlockSpec((R, W), functools.partial(lambda i, c, cb: (rb0 + i * nc + c, cb), cb=cb))

    in_specs = [zspec(0), zspec(1), zspec(2), zspec(3)]
    args = [z, z, z, z]
    if has_state:
        in_specs += [pl.BlockSpec((G, POOL_HIST - 1, W), lambda i, c: (i, 0, 0)),
                     pl.BlockSpec((G, nh, W), lambda i, c: (i, 0, 0))]
        args += [pool_buf, conv_buf]
    nwin = len(POOL_WINDOWS)
    in_specs += [pl.BlockSpec((nwin, W // nwin, W // nwin), lambda i, c: (0, 0, 0)),
                 pl.BlockSpec((1, W), lambda i, c: (0, 0)),
                 pl.BlockSpec((CONV_WIDTH, W), lambda i, c: (0, 0))]
    args += [wmix_bf16, pool_scale.reshape(1, W), conv_w]
    return pl.pallas_call(
        functools.partial(_poolconv_kernel, G=G, T=TC, W=W, pos0=pos0, has_state=has_state),
        grid=(Bt // G, nc),
        in_specs=in_specs,
        out_specs=[pl.BlockSpec((R, W), lambda i, c: (i * nc + c, 0)), pl.BlockSpec((R, W), lambda i, c: (i * nc + c, 0)),
                   pl.BlockSpec((G, POOL_HIST - 1, W), lambda i, c: (i, 0, 0)),
                   pl.BlockSpec((G, nh, W), lambda i, c: (i, 0, 0))],
        out_shape=[jax.ShapeDtypeStruct((Bt * T, W), BF16), jax.ShapeDtypeStruct((Bt * T, W), BF16),
                   jax.ShapeDtypeStruct((Bt, POOL_HIST - 1, W), F32), jax.ShapeDtypeStruct((Bt, nh, W), F32)],
        scratch_shapes=[pltpu.VMEM((G, POOL_HIST + TC, W), F32), pltpu.VMEM((G, CONV_HIST + TC, W), F32)],
        compiler_params=_cparams(("arbitrary", "arbitrary")),
        name="pool_conv",
    )(*args)


def _ssm_params_kernel(are_ref, aim_ref, ls_ref, abre_ref, abim_ref, core_ref, coim_ref):
    a_re = are_ref[...]
    a_im = aim_ref[...]
    step = jnp.exp(ls_ref[...])
    decay = jnp.exp(step * a_re)
    ab_re = decay * jnp.cos(step * a_im)
    ab_im = decay * jnp.sin(step * a_im)
    den = a_re * a_re + a_im * a_im
    nr = ab_re - 1.0
    abre_ref[...] = ab_re
    abim_ref[...] = ab_im
    core_ref[...] = (nr * a_re + ab_im * a_im) / den
    coim_ref[...] = (ab_im * a_re - nr * a_im) / den


def ssm_params(a_re, a_im, log_step):
    G, N = a_re.shape
    out = jax.ShapeDtypeStruct((G, N), F32)
    return pl.pallas_call(_ssm_params_kernel, out_shape=[out, out, out, out], name="ssm_params")(
        a_re, a_im, log_step.reshape(G, 1))


def _scan_levels(C):
    return [1 << k for k in range(int(math.log2(C)))]


def _ssm_tables_kernel(abre_ref, abim_ref, lre_ref, lim_ref, pre_ref, pim_ref, *, C):
    ar = abre_ref[...]
    ai = abim_ref[...]
    N = ar.shape[1]
    row = lax.broadcasted_iota(I32, (C, N), 0)
    hr = jnp.where(row == 0, ar, 0.0)
    hi = jnp.where(row == 0, ai, 0.0)
    lre_ref[...] = jnp.zeros(lre_ref.shape, F32)
    lim_ref[...] = jnp.zeros(lim_ref.shape, F32)
    for k, s in enumerate(_scan_levels(C)):
        lre_ref[k:k + 1, :] = ar
        lim_ref[k:k + 1, :] = ai
        sr = jnp.where(row >= s, pltpu.roll(hr, s, 0), 0.0)
        si = jnp.where(row >= s, pltpu.roll(hi, s, 0), 0.0)
        hr, hi = hr + ar * sr - ai * si, hi + ar * si + ai * sr
        ar, ai = ar * ar - ai * ai, 2.0 * ar * ai
    pre_ref[...] = hr
    pim_ref[...] = hi


def ssm_tables(ab_re, ab_im, C):
    N = ab_re.shape[1]
    nlev = len(_scan_levels(C))
    lev = jax.ShapeDtypeStruct((SUBLANES * ((nlev + SUBLANES - 1) // SUBLANES), N), F32)
    pw = jax.ShapeDtypeStruct((C, N), F32)
    return pl.pallas_call(functools.partial(_ssm_tables_kernel, C=C), out_shape=[lev, lev, pw, pw],
                          name="ssm_tables")(ab_re, ab_im)


def _gelu_tanh(x):
    return 0.5 * x * (1.0 + jnp.tanh(math.sqrt(2.0 / math.pi) * (x + 0.044715 * (x * x * x))))


def _ssm_kernel(*refs, R, T, NB, chained):
    (xs_ref, bre_ref, bim_ref, cre_ref, cim_ref, core_ref, coim_ref, lre_ref, lim_ref) = refs[0:9]
    p = 9
    if chained:
        pre_ref, pim_ref = refs[p:p + 2]
        p += 2
    else:
        h0r_ref, h0i_ref = refs[p:p + 2]
        p += 2
    d_ref, gw_ref, gb_ref = refs[p:p + 3]
    yd_ref, sre_ref, sim_ref = refs[p + 3:p + 6]
    p += 6
    if chained:
        car_ref, cai_ref = refs[p:p + 2]
    G = R // T
    xs = xs_ref[...]
    xb = xs.astype(BF16)
    row = lax.broadcasted_iota(I32, (R, LANES), 0)
    tpos = row % T
    levels = _scan_levels(T)

    if chained:
        c = pl.program_id(1)

        @pl.when(c == 0)
        def _():
            car_ref[...] = jnp.zeros(car_ref.shape, F32)
            cai_ref[...] = jnp.zeros(cai_ref.shape, F32)

    y = d_ref[...] * xs
    for cb in range(NB):
        pr = _dot(xb, bre_ref[cb])
        pi = _dot(xb, bim_ref[cb])
        cor = core_ref[cb][0:1, :]
        coi = coim_ref[cb][0:1, :]
        hr = cor * pr - coi * pi
        hi = cor * pi + coi * pr
        if not chained:
            ar = lre_ref[cb][0:1, :]
            ai = lim_ref[cb][0:1, :]
            h0r = jnp.broadcast_to(h0r_ref[:, cb:cb + 1, :], (G, T, LANES)).reshape(R, LANES)
            h0i = jnp.broadcast_to(h0i_ref[:, cb:cb + 1, :], (G, T, LANES)).reshape(R, LANES)
            first = tpos == 0
            hr = hr + jnp.where(first, ar * h0r - ai * h0i, 0.0)
            hi = hi + jnp.where(first, ar * h0i + ai * h0r, 0.0)
        for k, s in enumerate(levels):
            ar = lre_ref[cb][k:k + 1, :]
            ai = lim_ref[cb][k:k + 1, :]
            sr = jnp.where(tpos >= s, pltpu.roll(hr, s, 0), 0.0)
            si = jnp.where(tpos >= s, pltpu.roll(hi, s, 0), 0.0)
            hr, hi = hr + ar * sr - ai * si, hi + ar * si + ai * sr
        if chained:
            cr = car_ref[cb:cb + 1, :]
            ci = cai_ref[cb:cb + 1, :]
            pwr = pre_ref[cb]
            pwi = pim_ref[cb]
            hr, hi = hr + pwr * cr - pwi * ci, hi + pwr * ci + pwi * cr
            car_ref[cb:cb + 1, :] = hr[R - 1:R, :]
            cai_ref[cb:cb + 1, :] = hi[R - 1:R, :]
        else:
            sre_ref[:, cb:cb + 1, :] = hr.reshape(G, T, LANES)[:, T - 1:T, :]
            sim_ref[:, cb:cb + 1, :] = hi.reshape(G, T, LANES)[:, T - 1:T, :]
        y = y + _dot(hr.astype(BF16), cre_ref[cb]) - _dot(hi.astype(BF16), cim_ref[cb])

    z = _gelu_tanh(y)
    out = z * jax.nn.sigmoid(_dot(z.astype(BF16), gw_ref[...]) + gb_ref[...])
    yd_ref[...] = out.astype(yd_ref.dtype)

    if chained:
        @pl.when(c == pl.num_programs(1) - 1)
        def _():
            sre_ref[0] = car_ref[...]
            sim_ref[0] = cai_ref[...]


def ssm_mixer(z, xs_block, row0, Bt, T, consts, h0):
    (bre3, bim3, cre3, cim3, core3, coim3, lre3, lim3, pre3, pim3, dvec, gw, gb) = consts
    NB = bre3.shape[0]
    W = dvec.shape[1]
    chained = h0 is None
    R = SSM_CHUNK
    rb0 = row0 // R

    def full(a):
        nd = a.ndim
        return pl.BlockSpec(a.shape, lambda *_: (0,) * nd)

    if chained:
        nchunk = T // R
        grid = (Bt, nchunk)
        xs_spec = pl.BlockSpec((R, W), lambda b, c: (rb0 + b * nchunk + c, xs_block))
        st_args, st_specs = [pre3, pim3], [full(pre3), full(pim3)]
        yd_spec = pl.BlockSpec((R, W), lambda b, c: (b * nchunk + c, 0))
        s_spec = pl.BlockSpec((1, NB, LANES), lambda b, c: (b, 0, 0))
        scratch = [pltpu.VMEM((NB, LANES), F32), pltpu.VMEM((NB, LANES), F32)]
        sem = ("arbitrary", "arbitrary")
        Tk = R
    else:
        G = R // T
        grid = (Bt // G,)
        xs_spec = pl.BlockSpec((R, W), lambda i: (rb0 + i, xs_block))
        st_args = list(h0)
        st_specs = [pl.BlockSpec((G, NB, LANES), lambda i: (i, 0, 0))] * 2
        yd_spec = pl.BlockSpec((R, W), lambda i: (i, 0))
        s_spec = pl.BlockSpec((G, NB, LANES), lambda i: (i, 0, 0))
        scratch = []
        sem = ("arbitrary",)
        Tk = T
    shared = [bre3, bim3, cre3, cim3, core3, coim3, lre3, lim3]
    tailc = [dvec, gw, gb]
    s_shape = jax.ShapeDtypeStruct((Bt, NB, LANES), F32)
    return pl.pallas_call(
        functools.partial(_ssm_kernel, R=R, T=Tk, NB=NB, chained=chained),
        grid=grid,
        in_specs=[xs_spec] + [full(a) for a in shared] + st_specs + [full(a) for a in tailc],
        out_specs=[yd_spec, s_spec, s_spec],
        out_shape=[jax.ShapeDtypeStruct((Bt * T, W), BF16), s_shape, s_shape],
        scratch_shapes=scratch,
        compiler_params=_cparams(sem),
        name="ssm_chained" if chained else "ssm_stateful",
    )(z, *shared, *st_args, *tailc)


def _sort_keys(score):
    b = pltpu.bitcast(score, I32)
    key = jnp.where(b < 0, b ^ jnp.int32(0x7FFFFFFF), b)
    return jnp.where(key == -1, 0, key)


def _kth_largest_key(key, k):
    rows = key.shape[0]

    def body(i, t):
        cand = t + (jnp.int32(1) << (31 - i))
        cnt = jnp.sum(jnp.where(key >= cand, 1.0, 0.0), axis=-1, keepdims=True)
        return jnp.where(cnt >= float(k), cand, t)

    return lax.fori_loop(0, 32, body, jnp.full((rows, 1), -2 ** 31, I32))


def _blocked_prefix(eq, tri, offset):
    outs = []
    run = offset
    for j in range(eq.shape[1] // LANES):
        blk = eq[:, j * LANES:(j + 1) * LANES].astype(BF16)
        pj = _dot(blk, tri) + run
        outs.append(pj)
        run = pj[:, LANES - 1:LANES]
    return outs, run


def _attn_prompt_kernel(q_ref, iq_ref, tail_ref, k_ref, v_ref, ik_ref, gq_ref, e_ref, tri_ref, o_ref,
                        kb, vb, ikb, bias, *, TQ, L, dh, iw_off, n_sel):
    qb = pl.program_id(1)

    @pl.when(qb == 0)
    def _():
        kb[...] = k_ref[0].astype(BF16)
        vb[...] = v_ref[0].astype(BF16)
        ikb[...] = ik_ref[0].astype(BF16)

    scale = dh ** -0.5
    q = q_ref[...]
    q2 = q * q
    hi = q2.astype(BF16)
    lo = (q2 - hi.astype(F32)).astype(BF16)
    ss = _dot(hi, e_ref[...]) + _dot(lo, e_ref[...])
    qn = (q * lax.rsqrt(ss * (1.0 / dh) + EPS) * gq_ref[...] * scale).astype(BF16)
    iqs = (iq_ref[...] * scale).astype(BF16)
    iw = tail_ref[:, iw_off:iw_off + IDX_HEADS] * (IDX_HEADS ** -0.5)

    score = None
    for h in range(IDX_HEADS):
        lg = _dot_nt(iqs[:, h * dh:(h + 1) * dh], ikb[...])
        t = jnp.maximum(lg, 0.0) * iw[:, h:h + 1]
        score = t if score is None else score + t
    col = lax.broadcasted_iota(I32, (TQ, L), 1)
    qpos = qb * TQ + lax.broadcasted_iota(I32, (TQ, L), 0)
    causal = col <= qpos
    score = jnp.where(causal, score, NEG_INF)

    key = _sort_keys(score)
    thr = _kth_largest_key(key, n_sel)
    gt = key > thr
    eq = key == thr
    need = float(n_sel) - jnp.sum(jnp.where(gt, 1.0, 0.0), axis=-1, keepdims=True)
    pref, _ = _blocked_prefix(jnp.where(eq, 1.0, 0.0), tri_ref[...], jnp.zeros((TQ, 1), F32))
    pref = jnp.concatenate(pref, axis=1)
    sel = (gt | (eq & (pref <= need))) & causal
    bias[...] = jnp.where(sel, 0.0, NEG_INF)

    for h in range(N_HEADS_C):
        lg = _dot_nt(qn[:, h * dh:(h + 1) * dh], kb[...]) + bias[...]
        m = jnp.max(lg, axis=-1, keepdims=True)
        e = jnp.exp(lg - m)
        den = jnp.sum(e, axis=-1, keepdims=True)
        o = _dot(e.astype(BF16), vb[...]) / den
        o_ref[:, h * dh:(h + 1) * dh] = o.astype(o_ref.dtype)


def attn_prompt(z, q_block, iq_block, tail_block, tail_w, iw_off, kn, v, ik, gq_tiled, B, T, W, dh, TQ=128):
    nq = T // TQ
    n_sel = min(TOPK_MAX, T // 4)
    eye = (jnp.arange(W)[:, None] // dh == jnp.arange(W)[None, :] // dh).astype(BF16)
    tri = (jnp.arange(LANES)[:, None] <= jnp.arange(LANES)[None, :]).astype(BF16)
    kspec = pl.BlockSpec((1, T, dh), lambda b, i: (b, 0, 0))
    return pl.pallas_call(
        functools.partial(_attn_prompt_kernel, TQ=TQ, L=T, dh=dh, iw_off=iw_off, n_sel=n_sel),
        grid=(B, nq),
        in_specs=[pl.BlockSpec((TQ, W), lambda b, i: (b * nq + i, q_block)),
                  pl.BlockSpec((TQ, W), lambda b, i: (b * nq + i, iq_block)),
                  pl.BlockSpec((TQ, tail_w), lambda b, i: (b * nq + i, tail_block)),
                  kspec, kspec, kspec,
                  pl.BlockSpec((1, W), lambda b, i: (0, 0)),
                  pl.BlockSpec((W, W), lambda b, i: (0, 0)),
                  pl.BlockSpec((LANES, LANES), lambda b, i: (0, 0))],
        out_specs=pl.BlockSpec((TQ, W), lambda b, i: (b * nq + i, 0)),
        out_shape=jax.ShapeDtypeStruct((B * T, W), BF16),
        scratch_shapes=[pltpu.VMEM((T, dh), BF16), pltpu.VMEM((T, dh), BF16), pltpu.VMEM((T, dh), BF16),
                        pltpu.VMEM((TQ, T), F32)],
        compiler_params=_cparams(("arbitrary", "arbitrary")),
        name="attn_prompt",
    )(z, z, z, kn, v, ik, gq_tiled, eye, tri)


def _xattn_kernel(q_ref, mk_ref, mv_ref, g_ref, o_ref, *, G, T, dh, scale):
    g = g_ref[...]
    for s in range(G):
        rows = slice(s * T, (s + 1) * T)
        for h in range(X_HEADS):
            cols = slice(h * dh, (h + 1) * dh)
            qh = q_ref[rows, cols]
            ms = jnp.mean(qh * qh, axis=-1, keepdims=True)
            qn = (qh * lax.rsqrt(ms + EPS) * g).astype(BF16)
            lg = _dot_nt(qn, mk_ref[s, :, cols].astype(BF16)) * scale
            m = jnp.max(lg, axis=-1, keepdims=True)
            e = jnp.exp(lg - m)
            den = jnp.sum(e, axis=-1, keepdims=True)
            o = _dot(e.astype(BF16), mv_ref[s, :, cols].astype(BF16)) / den
            o_ref[rows, cols] = o.astype(o_ref.dtype)


def cross_attention(qx, row0, Bt, T, G, TQ, mem_k, mem_v, layer, gq):
    Wx = qx.shape[1]
    dh = Wx // X_HEADS
    n_mem = mem_k.shape[2]
    assert G == 1 or TQ == T
    R = G * TQ
    nt = T // TQ
    rb0 = row0 // R
    mspec = pl.BlockSpec((None, G, n_mem, Wx), lambda i, t: (layer, i, 0, 0))
    return pl.pallas_call(
        functools.partial(_xattn_kernel, G=G, T=TQ, dh=dh, scale=dh ** -0.5),
        grid=(Bt // G, nt),
        in_specs=[pl.BlockSpec((R, Wx), lambda i, t: (rb0 + i * nt + t, 0)), mspec, mspec,
                  pl.BlockSpec((1, dh), lambda i, t: (0, 0))],
        out_specs=pl.BlockSpec((R, Wx), lambda i, t: (i * nt + t, 0)),
        out_shape=jax.ShapeDtypeStruct((Bt * T, Wx), BF16),
        compiler_params=_cparams(("arbitrary", "arbitrary")),
        name="cross_attention",
    )(qx, mem_k, mem_v, gq.reshape(1, dh))


def _attn_sample_kernel(pt_ref, q_ref, iq_ref, tail_ref, kn_ref, vn_ref, ikn_ref, ck_hbm, cv_hbm, cik_hbm,
                        gq_ref, e_ref, tri_ref, o_ref,
                        kp, vp, ikp, knp, vnp, iknp, qs_scr, iqs_scr, iw_scr, score, bias, sems,
                        *, G, T, dh, layer, n_pages, hp, iw_off, n_sel):
    step = pl.program_id(0)
    R = G * T
    H = N_HEADS_C
    LP = n_pages * hp
    LC = 2 * LP + LANES

    def page_copies(i):
        g = i // n_pages
        p = i - g * n_pages
        page = pt_ref[step * G + g, p]
        dst_rows = pl.ds(pl.multiple_of(p * hp, hp), hp)
        return [pltpu.make_async_copy(src.at[layer, page], dst.at[g, dst_rows, :], sems.at[s])
                for s, (src, dst) in enumerate(((ck_hbm, kp), (cv_hbm, vp), (cik_hbm, ikp)))]

    def start_all(i, c):
        for cp in page_copies(i):
            cp.start()
        return c

    def wait_all(i, c):
        for cp in page_copies(i):
            cp.wait()
        return c

    lax.fori_loop(0, G * n_pages, start_all, 0)

    @pl.when(step == 0)
    def _():
        knp[...] = jnp.zeros(knp.shape, F32)
        vnp[...] = jnp.zeros(vnp.shape, F32)
        iknp[...] = jnp.zeros(iknp.shape, F32)

    scale = dh ** -0.5
    q = q_ref[...]
    q2 = q * q
    hi = q2.astype(BF16)
    lo = (q2 - hi.astype(F32)).astype(BF16)
    ss = _dot(hi, e_ref[...]) + _dot(lo, e_ref[...])
    qs_scr[...] = q * lax.rsqrt(ss * (1.0 / dh) + EPS) * gq_ref[...] * scale
    iqs_scr[...] = iq_ref[...] * scale
    iw_blk = (iw_off // LANES) * LANES
    iw_lane = iw_off - iw_blk
    iw_scr[...] = tail_ref[:, iw_blk:iw_blk + LANES]

    lax.fori_loop(0, G * n_pages, wait_all, 0)

    zeros_q = jnp.zeros((H * T, dh), F32)
    new_ok = lax.broadcasted_iota(I32, (T, LANES), 1) <= lax.broadcasted_iota(I32, (T, LANES), 0)

    def stack_heads(ref, rows):
        qs = jnp.concatenate([ref[rows, h * dh:(h + 1) * dh] for h in range(H)], axis=0)
        return (jnp.concatenate([qs, zeros_q], axis=1).astype(BF16),
                jnp.concatenate([zeros_q, qs], axis=1).astype(BF16), qs.astype(BF16))

    def score_body(g, c):
        rows = pl.ds(pl.multiple_of(g * T, T), T)
        iknp[0:T, :] = ikn_ref[rows, :]
        qe, qo, qs = stack_heads(iqs_scr, rows)
        ik2 = ikp[g].astype(BF16)
        lg_e = _dot_nt(qe, ik2)
        lg_o = _dot_nt(qo, ik2)
        lg_n = _dot_nt(qs, iknp[...].astype(BF16))
        iw = iw_scr[rows, :] * (IDX_HEADS ** -0.5)
        se = so = sn = None
        for h in range(IDX_HEADS):
            w = iw[:, iw_lane + h:iw_lane + h + 1]
            hs = slice(h * T, (h + 1) * T)
            te = jnp.maximum(lg_e[hs], 0.0) * w
            to = jnp.maximum(lg_o[hs], 0.0) * w
            tn = jnp.maximum(lg_n[hs], 0.0) * w
            se, so, sn = (te, to, tn) if se is None else (se + te, so + to, sn + tn)
        score[rows, 0:LP] = se
        score[rows, LP:2 * LP] = so
        score[rows, 2 * LP:LC] = jnp.where(new_ok, sn, NEG_INF)
        return c

    lax.fori_loop(0, G, score_body, 0)

    key = _sort_keys(score[...])
    thr = _kth_largest_key(key, n_sel)
    need = float(n_sel) - jnp.sum(jnp.where(key > thr, 1.0, 0.0), axis=-1, keepdims=True)
    key_e, key_o, key_n = key[:, 0:LP], key[:, LP:2 * LP], key[:, 2 * LP:LC]
    eq_e = jnp.where(key_e == thr, 1.0, 0.0)
    eq_o = jnp.where(key_o == thr, 1.0, 0.0)
    eq_n = jnp.where(key_n == thr, 1.0, 0.0)
    zero = jnp.zeros((R, 1), F32)
    tri = tri_ref[...]
    pe, te_tot = _blocked_prefix(eq_e, tri, zero)
    po, to_tot = _blocked_prefix(eq_o, tri, zero)
    pn, _ = _blocked_prefix(eq_n, tri, te_tot + to_tot)
    pe = jnp.concatenate(pe, axis=1)
    po = jnp.concatenate(po, axis=1)
    sel_e = (key_e > thr) | ((key_e == thr) & (pe + po - eq_o <= need))
    sel_o = (key_o > thr) | ((key_o == thr) & (pe + po <= need))
    new_ok_all = (lax.broadcasted_iota(I32, (R, LANES), 1) <= lax.broadcasted_iota(I32, (R, LANES), 0) % T)
    sel_n = ((key_n > thr) | ((key_n == thr) & (pn[0] <= need))) & new_ok_all
    bias[:, 0:LP] = jnp.where(sel_e, 0.0, NEG_INF)
    bias[:, LP:2 * LP] = jnp.where(sel_o, 0.0, NEG_INF)
    bias[:, 2 * LP:LC] = jnp.where(sel_n, 0.0, NEG_INF)

    def attn_body(g, c):
        rows = pl.ds(pl.multiple_of(g * T, T), T)
        knp[0:T, :] = kn_ref[rows, :]
        vnp[0:T, :] = vn_ref[rows, :]
        qe, qo, qs = stack_heads(qs_scr, rows)
        k2 = kp[g].astype(BF16)
        v2 = vp[g].astype(BF16)
        b = bias[rows, :]
        bh = jnp.concatenate([b] * H, axis=0)
        lg_e = _dot_nt(qe, k2) + bh[:, 0:LP]
        lg_o = _dot_nt(qo, k2) + bh[:, LP:2 * LP]
        lg_n = _dot_nt(qs, knp[...].astype(BF16)) + bh[:, 2 * LP:LC]
        m = jnp.maximum(jnp.maximum(jnp.max(lg_e, axis=-1, keepdims=True), jnp.max(lg_o, axis=-1, keepdims=True)),
                        jnp.max(lg_n, axis=-1, keepdims=True))
        ee = jnp.exp(lg_e - m)
        eo = jnp.exp(lg_o - m)
        en = jnp.exp(lg_n - m)
        den = (jnp.sum(ee, axis=-1, keepdims=True) + jnp.sum(eo, axis=-1, keepdims=True)
               + jnp.sum(en, axis=-1, keepdims=True))
        oe = _dot(ee.astype(BF16), v2)
        oo = _dot(eo.astype(BF16), v2)
        on = _dot(en.astype(BF16), vnp[...].astype(BF16))
        o = (oe[:, 0:dh] + oo[:, dh:2 * dh] + on) / den
        for h in range(H):
            o_ref[rows, h * dh:(h + 1) * dh] = o[h * T:(h + 1) * T, :].astype(o_ref.dtype)
        return c

    lax.fori_loop(0, G, attn_body, 0)


def attn_sample(z, row0, q_block, iq_block, tail_block, tail_w, iw_off, kn, v, ik, caches, page_table, layer,
                gq_tiled, Bs, T, W, dh, G=16):
    n_pages = page_table.shape[1]
    page = caches[0].shape[2]
    hp = page // 2
    R = G * T
    rb0 = row0 // R
    LP = n_pages * hp
    LC = 2 * LP + LANES
    n_sel = min(TOPK_MAX, (n_pages * page + T) // 4)
    packed = [c.reshape(c.shape[0], c.shape[1], hp, 2 * dh) for c in caches]
    eye = (jnp.arange(W)[:, None] // dh == jnp.arange(W)[None, :] // dh).astype(BF16)
    tri = (jnp.arange(LANES)[:, None] <= jnp.arange(LANES)[None, :]).astype(BF16)
    any_spec = pl.BlockSpec(memory_space=pl.ANY)
    nspec = pl.BlockSpec((R, dh), lambda i, pt: (rb0 + i, 0))
    grid_spec = pltpu.PrefetchScalarGridSpec(
        num_scalar_prefetch=1,
        grid=(Bs // G,),
        in_specs=[pl.BlockSpec((R, W), lambda i, pt: (rb0 + i, q_block)),
                  pl.BlockSpec((R, W), lambda i, pt: (rb0 + i, iq_block)),
                  pl.BlockSpec((R, tail_w), lambda i, pt: (rb0 + i, tail_block)),
                  nspec, nspec, nspec, any_spec, any_spec, any_spec,
                  pl.BlockSpec((1, W), lambda i, pt: (0, 0)),
                  pl.BlockSpec((W, W), lambda i, pt: (0, 0)),
                  pl.BlockSpec((LANES, LANES), lambda i, pt: (0, 0))],
        out_specs=pl.BlockSpec((R, W), lambda i, pt: (i, 0)),
        scratch_shapes=[pltpu.VMEM((G, LP, 2 * dh), F32), pltpu.VMEM((G, LP, 2 * dh), F32),
                        pltpu.VMEM((G, LP, 2 * dh), F32),
                        pltpu.VMEM((LANES, dh), F32), pltpu.VMEM((LANES, dh), F32), pltpu.VMEM((LANES, dh), F32),
                        pltpu.VMEM((R, W), F32), pltpu.VMEM((R, W), F32), pltpu.VMEM((R, LANES), F32),
                        pltpu.VMEM((R, LC), F32), pltpu.VMEM((R, LC), F32),
                        pltpu.SemaphoreType.DMA((3,))],
    )
    return pl.pallas_call(
        functools.partial(_attn_sample_kernel, G=G, T=T, dh=dh, layer=layer, n_pages=n_pages, hp=hp,
                          iw_off=iw_off, n_sel=n_sel),
        grid_spec=grid_spec,
        out_shape=jax.ShapeDtypeStruct((Bs * T, W), BF16),
        compiler_params=_cparams(("arbitrary",)),
        name="attn_sample",
    )(page_table, z, z, z, kn, v, ik, *packed, gq_tiled, eye, tri)


def _block_diag(blocks):
    G, a, b = blocks.shape
    eye = jnp.eye(G, dtype=blocks.dtype)
    return (eye[:, None, :, None] * blocks[:, :, None, :]).reshape(G * a, G * b)


def _lane_blocks(a):
    rows, n = a.shape
    return a.reshape(rows, n // LANES, LANES).transpose(1, 0, 2)


def kernel(x_prompt, x_sample, cache_attn_k, cache_attn_v, cache_idx_k, cache_mem_k, cache_mem_v, state_pool,
           state_conv, state_ssm_re, state_ssm_im, page_table, mem_prompt, norm_g, ffn_in, ffn_out, w_in,
           q_norm_g, k_norm_g, pool_mix, pool_scale, conv_w, ssm_a_re, ssm_a_im, ssm_log_step, ssm_b_re,
           ssm_b_im, ssm_c_re, ssm_c_im, ssm_d, ssm_glu_w, ssm_glu_b, w_branch, w_gate, b_gate, w_o,
           mem_norm_g, w_xq, w_xk, w_xv, xq_norm_g, xk_norm_g, w_xo):
    B, T, D = x_prompt.shape
    Bs, Ts, _ = x_sample.shape
    depth = norm_g.shape[0]
    Mp, Ms = B * T, Bs * Ts
    W = pool_scale.shape[1]
    dh = k_norm_g.shape[1]
    d_ff = ffn_out.shape[2]
    n_mem = mem_prompt.shape[1]
    Wx = w_xq.shape[2]
    SG, SN = ssm_a_re.shape[1:]
    NB = SG * SN // LANES
    past_len = page_table.shape[1] * cache_attn_k.shape[2]
    assert cache_idx_k.shape[-1] == dh and W == N_HEADS_C * dh == IDX_HEADS * dh
    TM = 1024

    o_k = 5 * W
    o_iq = o_k + 2 * dh
    o_ik = o_iq + W
    o_xs = o_ik + dh + IDX_HEADS
    tail_w = 2 * LANES
    Q_BLK, IQ_BLK, XS_BLK = 4, 5, 6
    TAIL_BLK = 7 * W // tail_w
    IW_OFF = 3 * dh
    n_z = 7 * W + tail_w

    h = jnp.concatenate([x_prompt.reshape(Mp, D), x_sample.reshape(Ms, D)], axis=0)
    caches = (cache_attn_k, cache_attn_v, cache_idx_k)
    mem_rows = mem_prompt.reshape(B * n_mem, D)
    outs = [[] for _ in range(16)]

    def ffn(h, l, i, g):
        n = rmsnorm_rows(h, g)
        act = matmul(n, [(ffn_in, (l, i), 0), (ffn_in, (l, i), d_ff // 512)], n_out=d_ff, tn=512, tm=TM,
                     mode="swiglu", out_dtype=BF16, name="ffn_in")
        return matmul(act, [(ffn_out, (l, i), 0)], n_out=D, tn=256, tm=TM, mode="residual", aux=h, scale=0.5,
                      name="ffn_out")

    for l in range(depth):
        h = ffn(h, l, 0, norm_g[l, 0])

        u = rmsnorm_rows(h, norm_g[l, 1])
        wl = w_in[l]
        w_in2 = jnp.concatenate([wl[:, 0:o_k], wl[:, o_iq:o_iq + W], wl[:, o_xs:o_xs + W], wl[:, o_k:o_k + 2 * dh],
                                 wl[:, o_ik:o_ik + dh + IDX_HEADS],
                                 jnp.zeros((D, tail_w - 3 * dh - IDX_HEADS), F32)], axis=1)
        z = matmul(u, [(w_in2, (), 0)], n_out=n_z, tn=768, tm=TM, name="in_proj")
        gates = matmul(u, [(w_gate, (l,), 0)], n_out=4 * D, tn=512, tm=TM, mode="bias_sigmoid",
                       aux=b_gate[l].reshape(1, 4 * D), out_dtype=BF16, name="gates")

        wmix = pool_mix[l].astype(BF16)
        ya_p, yb_p, npool_p, nconv_p = pool_conv(z, 0, B, T, 1, 512, 0, None, None, wmix, pool_scale[l], conv_w[l], W)
        ya_s, yb_s, npool_s, nconv_s = pool_conv(z, Mp, Bs, Ts, 16, Ts, past_len, state_pool[l], state_conv[l],
                                                 wmix, pool_scale[l], conv_w[l], W)

        kn, vv, ik = kv_prep(z, TAIL_BLK, tail_w, k_norm_g[l], dh)
        gq_tiled = jnp.tile(q_norm_g[l], N_HEADS_C).reshape(1, W)
        yc_p = attn_prompt(z, Q_BLK, IQ_BLK, TAIL_BLK, tail_w, IW_OFF, kn[:Mp].reshape(B, T, dh),
                           vv[:Mp].reshape(B, T, dh), ik[:Mp].reshape(B, T, dh), gq_tiled, B, T, W, dh)
        yc_s = attn_sample(z, Mp, Q_BLK, IQ_BLK, TAIL_BLK, tail_w, IW_OFF, kn, vv, ik, caches, page_table, l,
                           gq_tiled, Bs, Ts, W, dh)

        ab_re, ab_im, co_re, co_im = ssm_params(ssm_a_re[l], ssm_a_im[l], ssm_log_step[l])
        flat = lambda a: a.reshape(1, SG * SN)
        lev_re, lev_im, pw_re, pw_im = ssm_tables(flat(ab_re), flat(ab_im), SSM_CHUNK)
        rep = lambda a: _lane_blocks(jnp.broadcast_to(flat(a), (SUBLANES, SG * SN)))
        consts = (
            _lane_blocks(_block_diag(ssm_b_re[l].transpose(0, 2, 1))).astype(BF16),
            _lane_blocks(_block_diag(ssm_b_im[l].transpose(0, 2, 1))).astype(BF16),
            _block_diag(ssm_c_re[l].transpose(0, 2, 1)).reshape(NB, LANES, W).astype(BF16),
            _block_diag(ssm_c_im[l].transpose(0, 2, 1)).reshape(NB, LANES, W).astype(BF16),
            rep(co_re), rep(co_im), _lane_blocks(lev_re), _lane_blocks(lev_im),
            _lane_blocks(pw_re), _lane_blocks(pw_im),
            ssm_d[l].reshape(1, W), ssm_glu_w[l].astype(BF16), ssm_glu_b[l].reshape(1, W))
        yd_p, sre_p, sim_p = ssm_mixer(z, XS_BLK, 0, B, T, consts, None)
        h0 = (state_ssm_re[l].reshape(Bs, NB, LANES), state_ssm_im[l].reshape(Bs, NB, LANES))
        yd_s, sre_s, sim_s = ssm_mixer(z, XS_BLK, Mp, Bs, Ts, consts, h0)

        cat = lambda a, b: jnp.concatenate([a, b], axis=0)
        merged = gated_merge([cat(ya_p, ya_s), cat(yb_p, yb_s), cat(yc_p, yc_s), cat(yd_p, yd_s)], gates, w_branch, l)
        h = matmul(merged, [(w_o, (l,), 0)], n_out=D, tn=512, tm=TM, mode="residual", aux=h, scale=1.0, name="w_o")

        mn = rmsnorm_rows(mem_rows, mem_norm_g[l])
        mk = matmul(mn, [(w_xk, (l,), 0)], n_out=Wx, tn=Wx, tm=B * n_mem, mode="headnorm",
                    aux=jnp.tile(xk_norm_g[l], X_HEADS).reshape(1, Wx), group=Wx // X_HEADS, name="mem_k")
        mv = matmul(mn, [(w_xv, (l,), 0)], n_out=Wx, tn=Wx, tm=B * n_mem, name="mem_v")
        n2 = rmsnorm_rows(h, norm_g[l, 2])
        qx = matmul(n2, [(w_xq, (l,), 0)], n_out=Wx, tn=Wx, tm=TM, name="w_xq")
        xa_p = cross_attention(qx, 0, B, T, 1, 512, mk.reshape(1, B, n_mem, Wx), mv.reshape(1, B, n_mem, Wx), 0,
                               xq_norm_g[l])
        xa_s = cross_attention(qx, Mp, Bs, Ts, 8, Ts, cache_mem_k.reshape(depth, Bs, n_mem, Wx),
                               cache_mem_v.reshape(depth, Bs, n_mem, Wx), l, xq_norm_g[l])
        h = matmul(cat(xa_p, xa_s), [(w_xo, (l,), 0)], n_out=D, tn=512, tm=TM, mode="residual", aux=h, scale=1.0,
                   name="w_xo")

        h = ffn(h, l, 1, norm_g[l, 3])

        xh = Wx // X_HEADS
        layer_out = (kn[:Mp].reshape(B, T, dh), vv[:Mp].reshape(B, T, dh), ik[:Mp].reshape(B, T, dh),
                     mk.reshape(B, n_mem, X_HEADS, xh), mv.reshape(B, n_mem, X_HEADS, xh), npool_p, nconv_p,
                     sre_p.reshape(B, SG, SN), sim_p.reshape(B, SG, SN),
                     kn[Mp:].reshape(Bs, Ts, dh), vv[Mp:].reshape(Bs, Ts, dh), ik[Mp:].reshape(Bs, Ts, dh),
                     npool_s, nconv_s, sre_s.reshape(Bs, SG, SN), sim_s.reshape(Bs, SG, SN))
        for acc, val in zip(outs, layer_out):
            acc.append(val)

    return (h[:Mp].reshape(B, T, D), h[Mp:].reshape(Bs, Ts, D)) + tuple(jnp.stack(o) for o in outs)
```

```python
import functools
import math

import jax
import jax.numpy as jnp
from jax import lax
from jax.experimental import pallas as pl
from jax.experimental.pallas import tpu as pltpu

F32, BF16, I32 = jnp.float32, jnp.bfloat16, jnp.int32
EPS = 1e-6
NEG_INF = float("-inf")

V7X_VMEM_BYTES = 64 * 1024 * 1024
VMEM_LIMIT = V7X_VMEM_BYTES - 8 * 1024 * 1024
LANES = 128
SUBLANES = 8

POOL_WINDOWS = (2, 4, 8, 16)
POOL_HIST = 16
CONV_WIDTH = 3
CONV_HIST = 8
N_HEADS_C = 8
IDX_HEADS = 8
TOPK_MAX = 256
X_HEADS = 4
SSM_CHUNK = 128


def _cparams(sem, vmem=VMEM_LIMIT):
    return pltpu.CompilerParams(dimension_semantics=sem, vmem_limit_bytes=vmem)


def _dot(a, b):
    return jnp.dot(a, b, preferred_element_type=F32)


def _dot_nt(a, b):
    return lax.dot_general(a, b, (((1,), (1,)), ((), ())), preferred_element_type=F32)


def _rmsnorm_kernel(x_ref, g_ref, o_ref):
    x = x_ref[...]
    ms = jnp.mean(x * x, axis=-1, keepdims=True)
    o_ref[...] = (x * lax.rsqrt(ms + EPS) * g_ref[...]).astype(o_ref.dtype)


def rmsnorm_rows(x, g, tm=512):
    M, D = x.shape
    return pl.pallas_call(
        _rmsnorm_kernel,
        grid=(M // tm,),
        in_specs=[pl.BlockSpec((tm, D), lambda i: (i, 0)), pl.BlockSpec((1, D), lambda i: (0, 0))],
        out_specs=pl.BlockSpec((tm, D), lambda i: (i, 0)),
        out_shape=jax.ShapeDtypeStruct((M, D), BF16),
        compiler_params=_cparams(("arbitrary",)),
        name="rmsnorm",
    )(x, g.reshape(1, D))


def _mm_kernel(*refs, nw, mode, scale, sub, group):
    x_ref = refs[0]
    w_refs = refs[1:1 + nw]
    p = 1 + nw
    aux_ref = None
    if mode in ("bias_sigmoid", "residual", "headnorm"):
        aux_ref = refs[p]
        p += 1
    o_ref = refs[p]
    scr = refs[p + 1:p + 1 + nw]

    @pl.when(pl.program_id(1) == 0)
    def _():
        for w_ref, s in zip(w_refs, scr):
            s[...] = w_ref[...].astype(BF16)

    tm = x_ref.shape[0]

    def body(r, carry):
        rows = pl.ds(pl.multiple_of(r * sub, sub), sub)
        x = x_ref[rows, :]
        acc = [_dot(x, s[...]) for s in scr]
        if mode == "swiglu":
            a, b = acc
            y = (a * jax.nn.sigmoid(a)) * b
        elif mode == "bias_sigmoid":
            y = jax.nn.sigmoid(acc[0] + aux_ref[...])
        elif mode == "residual":
            y = aux_ref[rows, :] + scale * acc[0]
        elif mode == "headnorm":
            a = acc[0]
            parts = []
            for h in range(a.shape[1] // group):
                ah = a[:, h * group:(h + 1) * group]
                ms = jnp.mean(ah * ah, axis=-1, keepdims=True)
                parts.append(ah * lax.rsqrt(ms + EPS))
            y = jnp.concatenate(parts, axis=1) * aux_ref[...]
        else:
            y = acc[0]
        o_ref[rows, :] = y.astype(o_ref.dtype)
        return carry

    lax.fori_loop(0, tm // sub, body, 0, unroll=True)


def matmul(x, weights, *, n_out, tn, tm, mode="plain", aux=None, scale=1.0, out_dtype=F32, group=LANES,
           name="matmul"):
    M, K = x.shape
    nw = len(weights)
    sub = min(tm, 256)
    in_specs = [pl.BlockSpec((tm, K), lambda j, i: (i, 0))]
    args = [x]
    for arr, lead, coff in weights:
        nl = len(lead)
        in_specs.append(pl.BlockSpec((None,) * nl + (K, tn),
                                     functools.partial(lambda j, i, lead, coff: (*lead, 0, coff + j), lead=lead, coff=coff)))
        args.append(arr)
    if mode in ("bias_sigmoid", "headnorm"):
        in_specs.append(pl.BlockSpec((1, tn), lambda j, i: (0, j)))
        args.append(aux)
    elif mode == "residual":
        in_specs.append(pl.BlockSpec((tm, tn), lambda j, i: (i, j)))
        args.append(aux)
    return pl.pallas_call(
        functools.partial(_mm_kernel, nw=nw, mode=mode, scale=scale, sub=sub, group=group),
        grid=(n_out // tn, M // tm),
        in_specs=in_specs,
        out_specs=pl.BlockSpec((tm, tn), lambda j, i: (i, j)),
        out_shape=jax.ShapeDtypeStruct((M, n_out), out_dtype),
        scratch_shapes=[pltpu.VMEM((K, tn), BF16) for _ in range(nw)],
        compiler_params=_cparams(("arbitrary", "arbitrary")),
        name=name,
    )(*args)


def _merge_kernel(*refs, nb):
    br = refs[0:nb]
    gt = refs[nb:2 * nb]
    ws = refs[2 * nb:3 * nb]
    o_ref = refs[3 * nb]
    scr = refs[3 * nb + 1:3 * nb + 1 + nb]

    @pl.when(pl.program_id(1) == 0)
    def _():
        for w_ref, s in zip(ws, scr):
            s[...] = w_ref[...].astype(BF16)

    tm = o_ref.shape[0]
    sub = min(tm, 256)

    def body(r, carry):
        rows = pl.ds(pl.multiple_of(r * sub, sub), sub)
        acc = None
        for k in range(nb):
            t = gt[k][rows, :].astype(F32) * _dot(br[k][rows, :], scr[k][...])
            acc = t if acc is None else acc + t
        o_ref[rows, :] = acc.astype(o_ref.dtype)
        return carry

    lax.fori_loop(0, tm // sub, body, 0, unroll=True)


def gated_merge(branches, gates, w_branch, layer, *, tn=512, tm=1024):
    nb = len(branches)
    M, W = branches[0].shape
    D = w_branch.shape[-1]
    nj = D // tn
    in_specs = [pl.BlockSpec((tm, W), lambda j, i: (i, 0)) for _ in range(nb)]
    in_specs += [pl.BlockSpec((tm, tn), functools.partial(lambda j, i, k: (i, k * nj + j), k=k)) for k in range(nb)]
    in_specs += [pl.BlockSpec((None, None, W, tn), functools.partial(lambda j, i, k: (layer, k, 0, j), k=k))
                 for k in range(nb)]
    return pl.pallas_call(
        functools.partial(_merge_kernel, nb=nb),
        grid=(nj, M // tm),
        in_specs=in_specs,
        out_specs=pl.BlockSpec((tm, tn), lambda j, i: (i, j)),
        out_shape=jax.ShapeDtypeStruct((M, D), BF16),
        scratch_shapes=[pltpu.VMEM((W, tn), BF16) for _ in range(nb)],
        compiler_params=_cparams(("arbitrary", "arbitrary")),
        name="gated_merge",
    )(*branches, *([gates] * nb), *([w_branch] * nb))


def _kvprep_kernel(t_ref, g_ref, k_ref, v_ref, ik_ref, *, dh):
    t = t_ref[...]
    k = t[:, 0:dh]
    ms = jnp.mean(k * k, axis=-1, keepdims=True)
    k_ref[...] = k * lax.rsqrt(ms + EPS) * g_ref[...]
    v_ref[...] = t[:, dh:2 * dh]
    ik_ref[...] = t[:, 2 * dh:3 * dh]


def kv_prep(z, tail_block, tail_w, gk, dh, tm=1024):
    M = z.shape[0]
    out = jax.ShapeDtypeStruct((M, dh), F32)
    return pl.pallas_call(
        functools.partial(_kvprep_kernel, dh=dh),
        grid=(M // tm,),
        in_specs=[pl.BlockSpec((tm, tail_w), lambda i: (i, tail_block)), pl.BlockSpec((1, dh), lambda i: (0, 0))],
        out_specs=[pl.BlockSpec((tm, dh), lambda i: (i, 0))] * 3,
        out_shape=[out, out, out],
        compiler_params=_cparams(("arbitrary",)),
        name="kv_prep",
    )(z, gk.reshape(1, dh))


def _poolconv_kernel(*refs, G, T, W, pos0, has_state):
    xp_ref, xc_ref, bg_ref, cg_ref = refs[0:4]
    p = 4
    if has_state:
        pbuf_ref, cbuf_ref = refs[4:6]
        p = 6
    wmix_ref, pscale_ref, convw_ref = refs[p:p + 3]
    ya_ref, yb_ref, npool_ref, nconv_ref = refs[p + 3:p + 7]
    fullp, fullc = refs[p + 7:p + 9]
    PH, CH = POOL_HIST, CONV_HIST
    nh = CONV_WIDTH - 1
    gw = W // len(POOL_WINDOWS)
    c = pl.program_id(1)

    @pl.when(c == 0)
    def _():
        if has_state:
            fullp[:, 1:PH, :] = pbuf_ref[...]
            fullc[:, CH - nh:CH, :] = cbuf_ref[...]
        else:
            fullp[:, 0:PH, :] = jnp.zeros((G, PH, W), F32)
            fullc[:, 0:CH, :] = jnp.zeros((G, CH, W), F32)

    @pl.when(c > 0)
    def _():
        fullp[:, 0:PH, :] = fullp[:, T:T + PH, :]
        fullc[:, 0:CH, :] = fullc[:, T:T + CH, :]

    fullp[:, PH:PH + T, :] = xp_ref[...].reshape(G, T, W)
    t_idx = lax.broadcasted_iota(I32, (1, T, 1), 1) + (c * T + (pos0 + 1))
    for gi, w in enumerate(POOL_WINDOWS):
        cols = slice(gi * gw, (gi + 1) * gw)
        acc = fullp[:, PH:PH + T, cols]
        for j in range(1, w):
            acc = acc + fullp[:, PH - j:PH - j + T, cols]
        cnt = jnp.minimum(t_idx, w).astype(F32)
        d = acc / cnt - fullp[:, PH:PH + T, cols]
        y = _dot(d.reshape(G * T, gw).astype(BF16), wmix_ref[gi]) * pscale_ref[:, cols]
        ya_ref[:, cols] = y.astype(ya_ref.dtype)
    npool_ref[...] = fullp[:, T + 1:T + PH, :]

    fullc[:, CH:CH + T, :] = (cg_ref[...] * xc_ref[...]).reshape(G, T, W)
    y = None
    for j in range(CONV_WIDTH):
        wj = convw_ref[j:j + 1, :].reshape(1, 1, W)
        term = wj * fullc[:, CH - nh + j:CH - nh + j + T, :]
        y = term if y is None else y + term
    yb_ref[...] = (bg_ref[...] * y.reshape(G * T, W)).astype(yb_ref.dtype)
    nconv_ref[...] = fullc[:, CH + T - nh:CH + T, :]


def pool_conv(z, row0, Bt, T, G, TC, pos0, pool_buf, conv_buf, wmix_bf16, pool_scale, conv_w, W):
    has_state = pool_buf is not None
    assert G == 1 or TC == T
    R = G * TC
    nc = T // TC
    rb0 = row0 // R
    nh = CONV_WIDTH - 1

    def zspec(cb):
        return pl.BlockSpec((R, W), functools.partial(lambda i, c, cb: (rb0 + i * nc + c, cb), cb=cb))

    in_specs = [zspec(0), zspec(1), zspec(2), zspec(3)]
    args = [z, z, z, z]
    if has_state:
        in_specs += [pl.BlockSpec((G, POOL_HIST - 1, W), lambda i, c: (i, 0, 0)),
                     pl.BlockSpec((G, nh, W), lambda i, c: (i, 0, 0))]
        args += [pool_buf, conv_buf]
    nwin = len(POOL_WINDOWS)
    in_specs += [pl.BlockSpec((nwin, W // nwin, W // nwin), lambda i, c: (0, 0, 0)),
                 pl.BlockSpec((1, W), lambda i, c: (0, 0)),
                 pl.BlockSpec((CONV_WIDTH, W), lambda i, c: (0, 0))]
    args += [wmix_bf16, pool_scale.reshape(1, W), conv_w]
    return pl.pallas_call(
        functools.partial(_poolconv_kernel, G=G, T=TC, W=W, pos0=pos0, has_state=has_state),
        grid=(Bt // G, nc),
        in_specs=in_specs,
        out_specs=[pl.BlockSpec((R, W), lambda i, c: (i * nc + c, 0)), pl.BlockSpec((R, W), lambda i, c: (i * nc + c, 0)),
                   pl.BlockSpec((G, POOL_HIST - 1, W), lambda i, c: (i, 0, 0)),
                   pl.BlockSpec((G, nh, W), lambda i, c: (i, 0, 0))],
        out_shape=[jax.ShapeDtypeStruct((Bt * T, W), BF16), jax.ShapeDtypeStruct((Bt * T, W), BF16),
                   jax.ShapeDtypeStruct((Bt, POOL_HIST - 1, W), F32), jax.ShapeDtypeStruct((Bt, nh, W), F32)],
        scratch_shapes=[pltpu.VMEM((G, POOL_HIST + TC, W), F32), pltpu.VMEM((G, CONV_HIST + TC, W), F32)],
        compiler_params=_cparams(("arbitrary", "arbitrary")),
        name="pool_conv",
    )(*args)


def _ssm_params_kernel(are_ref, aim_ref, ls_ref, abre_ref, abim_ref, core_ref, coim_ref):
    a_re = are_ref[...]
    a_im = aim_ref[...]
    step = jnp.exp(ls_ref[...])
    decay = jnp.exp(step * a_re)
    ab_re = decay * jnp.cos(step * a_im)
    ab_im = decay * jnp.sin(step * a_im)
    den = a_re * a_re + a_im * a_im
    nr = ab_re - 1.0
    abre_ref[...] = ab_re
    abim_ref[...] = ab_im
    core_ref[...] = (nr * a_re + ab_im * a_im) / den
    coim_ref[...] = (ab_im * a_re - nr * a_im) / den


def ssm_params(a_re, a_im, log_step):
    G, N = a_re.shape
    out = jax.ShapeDtypeStruct((G, N), F32)
    return pl.pallas_call(_ssm_params_kernel, out_shape=[out, out, out, out], name="ssm_params")(
        a_re, a_im, log_step.reshape(G, 1))


def _scan_levels(C):
    return [1 << k for k in range(int(math.log2(C)))]


def _ssm_tables_kernel(abre_ref, abim_ref, lre_ref, lim_ref, pre_ref, pim_ref, *, C):
    ar = abre_ref[...]
    ai = abim_ref[...]
    N = ar.shape[1]
    row = lax.broadcasted_iota(I32, (C, N), 0)
    hr = jnp.where(row == 0, ar, 0.0)
    hi = jnp.where(row == 0, ai, 0.0)
    lre_ref[...] = jnp.zeros(lre_ref.shape, F32)
    lim_ref[...] = jnp.zeros(lim_ref.shape, F32)
    for k, s in enumerate(_scan_levels(C)):
        lre_ref[k:k + 1, :] = ar
        lim_ref[k:k + 1, :] = ai
        sr = jnp.where(row >= s, pltpu.roll(hr, s, 0), 0.0)
        si = jnp.where(row >= s, pltpu.roll(hi, s, 0), 0.0)
        hr, hi = hr + ar * sr - ai * si, hi + ar * si + ai * sr
        ar, ai = ar * ar - ai * ai, 2.0 * ar * ai
    pre_ref[...] = hr
    pim_ref[...] = hi


def ssm_tables(ab_re, ab_im, C):
    N = ab_re.shape[1]
    nlev = len(_scan_levels(C))
    lev = jax.ShapeDtypeStruct((SUBLANES * ((nlev + SUBLANES - 1) // SUBLANES), N), F32)
    pw = jax.ShapeDtypeStruct((C, N), F32)
    return pl.pallas_call(functools.partial(_ssm_tables_kernel, C=C), out_shape=[lev, lev, pw, pw],
                          name="ssm_tables")(ab_re, ab_im)


def _gelu_tanh(x):
    return 0.5 * x * (1.0 + jnp.tanh(math.sqrt(2.0 / math.pi) * (x + 0.044715 * (x * x * x))))


def _ssm_kernel(*refs, R, T, NB, chained):
    (xs_ref, bre_ref, bim_ref, cre_ref, cim_ref, core_ref, coim_ref, lre_ref, lim_ref) = refs[0:9]
    p = 9
    if chained:
        pre_ref, pim_ref = refs[p:p + 2]
        p += 2
    else:
        h0r_ref, h0i_ref = refs[p:p + 2]
        p += 2
    d_ref, gw_ref, gb_ref = refs[p:p + 3]
    yd_ref, sre_ref, sim_ref = refs[p + 3:p + 6]
    p += 6
    if chained:
        car_ref, cai_ref = refs[p:p + 2]
    G = R // T
    xs = xs_ref[...]
    xb = xs.astype(BF16)
    row = lax.broadcasted_iota(I32, (R, LANES), 0)
    tpos = row % T
    levels = _scan_levels(T)

    if chained:
        c = pl.program_id(1)

        @pl.when(c == 0)
        def _():
            car_ref[...] = jnp.zeros(car_ref.shape, F32)
            cai_ref[...] = jnp.zeros(cai_ref.shape, F32)

    y = d_ref[...] * xs
    for cb in range(NB):
        pr = _dot(xb, bre_ref[cb])
        pi = _dot(xb, bim_ref[cb])
        cor = core_ref[cb][0:1, :]
        coi = coim_ref[cb][0:1, :]
        hr = cor * pr - coi * pi
        hi = cor * pi + coi * pr
        if not chained:
            ar = lre_ref[cb][0:1, :]
            ai = lim_ref[cb][0:1, :]
            h0r = jnp.broadcast_to(h0r_ref[:, cb:cb + 1, :], (G, T, LANES)).reshape(R, LANES)
            h0i = jnp.broadcast_to(h0i_ref[:, cb:cb + 1, :], (G, T, LANES)).reshape(R, LANES)
            first = tpos == 0
            hr = hr + jnp.where(first, ar * h0r - ai * h0i, 0.0)
            hi = hi + jnp.where(first, ar * h0i + ai * h0r, 0.0)
        for k, s in enumerate(levels):
            ar = lre_ref[cb][k:k + 1, :]
            ai = lim_ref[cb][k:k + 1, :]
            sr = jnp.where(tpos >= s, pltpu.roll(hr, s, 0), 0.0)
            si = jnp.where(tpos >= s, pltpu.roll(hi, s, 0), 0.0)
            hr, hi = hr + ar * sr - ai * si, hi + ar * si + ai * sr
        if chained:
            cr = car_ref[cb:cb + 1, :]
            ci = cai_ref[cb:cb + 1, :]
            pwr = pre_ref[cb]
            pwi = pim_ref[cb]
            hr, hi = hr + pwr * cr - pwi * ci, hi + pwr * ci + pwi * cr
            car_ref[cb:cb + 1, :] = hr[R - 1:R, :]
            cai_ref[cb:cb + 1, :] = hi[R - 1:R, :]
        else:
            sre_ref[:, cb:cb + 1, :] = hr.reshape(G, T, LANES)[:, T - 1:T, :]
            sim_ref[:, cb:cb + 1, :] = hi.reshape(G, T, LANES)[:, T - 1:T, :]
        y = y + _dot(hr.astype(BF16), cre_ref[cb]) - _dot(hi.astype(BF16), cim_ref[cb])

    z = _gelu_tanh(y)
    out = z * jax.nn.sigmoid(_dot(z.astype(BF16), gw_ref[...]) + gb_ref[...])
    yd_ref[...] = out.astype(yd_ref.dtype)

    if chained:
        @pl.when(c == pl.num_programs(1) - 1)
        def _():
            sre_ref[0] = car_ref[...]
            sim_ref[0] = cai_ref[...]


def ssm_mixer(z, xs_block, row0, Bt, T, consts, h0):
    (bre3, bim3, cre3, cim3, core3, coim3, lre3, lim3, pre3, pim3, dvec, gw, gb) = consts
    NB = bre3.shape[0]
    W = dvec.shape[1]
    chained = h0 is None
    R = SSM_CHUNK
    rb0 = row0 // R

    def full(a):
        nd = a.ndim
        return pl.BlockSpec(a.shape, lambda *_: (0,) * nd)

    if chained:
        nchunk = T // R
        grid = (Bt, nchunk)
        xs_spec = pl.BlockSpec((R, W), lambda b, c: (rb0 + b * nchunk + c, xs_block))
        st_args, st_specs = [pre3, pim3], [full(pre3), full(pim3)]
        yd_spec = pl.BlockSpec((R, W), lambda b, c: (b * nchunk + c, 0))
        s_spec = pl.BlockSpec((1, NB, LANES), lambda b, c: (b, 0, 0))
        scratch = [pltpu.VMEM((NB, LANES), F32), pltpu.VMEM((NB, LANES), F32)]
        sem = ("arbitrary", "arbitrary")
        Tk = R
    else:
        G = R // T
        grid = (Bt // G,)
        xs_spec = pl.BlockSpec((R, W), lambda i: (rb0 + i, xs_block))
        st_args = list(h0)
        st_specs = [pl.BlockSpec((G, NB, LANES), lambda i: (i, 0, 0))] * 2
        yd_spec = pl.BlockSpec((R, W), lambda i: (i, 0))
        s_spec = pl.BlockSpec((G, NB, LANES), lambda i: (i, 0, 0))
        scratch = []
        sem = ("arbitrary",)
        Tk = T
    shared = [bre3, bim3, cre3, cim3, core3, coim3, lre3, lim3]
    tailc = [dvec, gw, gb]
    s_shape = jax.ShapeDtypeStruct((Bt, NB, LANES), F32)
    return pl.pallas_call(
        functools.partial(_ssm_kernel, R=R, T=Tk, NB=NB, chained=chained),
        grid=grid,
        in_specs=[xs_spec] + [full(a) for a in shared] + st_specs + [full(a) for a in tailc],
        out_specs=[yd_spec, s_spec, s_spec],
        out_shape=[jax.ShapeDtypeStruct((Bt * T, W), BF16), s_shape, s_shape],
        scratch_shapes=scratch,
        compiler_params=_cparams(sem),
        name="ssm_chained" if chained else "ssm_stateful",
    )(z, *shared, *st_args, *tailc)


def _sort_keys(score):
    b = pltpu.bitcast(score, I32)
    key = jnp.where(b < 0, b ^ jnp.int32(0x7FFFFFFF), b)
    return jnp.where(key == -1, 0, key)


def _kth_largest_key(key, k):
    rows = key.shape[0]

    def body(i, t):
        cand = t + (jnp.int32(1) << (31 - i))
        cnt = jnp.sum(jnp.where(key >= cand, 1.0, 0.0), axis=-1, keepdims=True)
        return jnp.where(cnt >= float(k), cand, t)

    return lax.fori_loop(0, 32, body, jnp.full((rows, 1), -2 ** 31, I32))


def _blocked_prefix(eq, tri, offset):
    outs = []
    run = offset
    for j in range(eq.shape[1] // LANES):
        blk = eq[:, j * LANES:(j + 1) * LANES].astype(BF16)
        pj = _dot(blk, tri) + run
        outs.append(pj)
        run = pj[:, LANES - 1:LANES]
    return outs, run


def _attn_prompt_kernel(q_ref, iq_ref, tail_ref, k_ref, v_ref, ik_ref, gq_ref, e_ref, tri_ref, o_ref,
                        kb, vb, ikb, bias, *, TQ, L, dh, iw_off, n_sel, n_buckets):
    qb = pl.program_id(1)

    @pl.when(qb == 0)
    def _():
        kb[...] = k_ref[0].astype(BF16)
        vb[...] = v_ref[0].astype(BF16)
        ikb[...] = ik_ref[0].astype(BF16)

    scale = dh ** -0.5
    q = q_ref[...]
    q2 = q * q
    hi = q2.astype(BF16)
    lo = (q2 - hi.astype(F32)).astype(BF16)
    ss = _dot(hi, e_ref[...]) + _dot(lo, e_ref[...])
    qn = (q * lax.rsqrt(ss * (1.0 / dh) + EPS) * gq_ref[...] * scale).astype(BF16)
    iqs = (iq_ref[...] * scale).astype(BF16)
    iw = tail_ref[:, iw_off:iw_off + IDX_HEADS] * (IDX_HEADS ** -0.5)

    def attend(Lk):
        score = None
        for h in range(IDX_HEADS):
            lg = _dot_nt(iqs[:, h * dh:(h + 1) * dh], ikb[0:Lk, :])
            t = jnp.maximum(lg, 0.0) * iw[:, h:h + 1]
            score = t if score is None else score + t
        col = lax.broadcasted_iota(I32, (TQ, Lk), 1)
        qpos = qb * TQ + lax.broadcasted_iota(I32, (TQ, Lk), 0)
        causal = col <= qpos
        score = jnp.where(causal, score, NEG_INF)

        key = _sort_keys(score)
        thr = _kth_largest_key(key, n_sel)
        gt = key > thr
        eq = key == thr
        need = float(n_sel) - jnp.sum(jnp.where(gt, 1.0, 0.0), axis=-1, keepdims=True)
        pref, _ = _blocked_prefix(jnp.where(eq, 1.0, 0.0), tri_ref[...], jnp.zeros((TQ, 1), F32))
        pref = jnp.concatenate(pref, axis=1)
        sel = (gt | (eq & (pref <= need))) & causal
        bias[:, 0:Lk] = jnp.where(sel, 0.0, NEG_INF)

        for h in range(N_HEADS_C):
            lg = _dot_nt(qn[:, h * dh:(h + 1) * dh], kb[0:Lk, :]) + bias[:, 0:Lk]
            m = jnp.max(lg, axis=-1, keepdims=True)
            e = jnp.exp(lg - m)
            den = jnp.sum(e, axis=-1, keepdims=True)
            o = _dot(e.astype(BF16), vb[0:Lk, :]) / den
            o_ref[:, h * dh:(h + 1) * dh] = o.astype(o_ref.dtype)

    per = (L // TQ) // n_buckets
    for bk in range(n_buckets):
        pl.when(qb // per == bk)(functools.partial(attend, (bk + 1) * per * TQ))


def attn_prompt(z, q_block, iq_block, tail_block, tail_w, iw_off, kn, v, ik, gq_tiled, B, T, W, dh, TQ=128,
                n_buckets=4):
    nq = T // TQ
    n_sel = min(TOPK_MAX, T // 4)
    eye = (jnp.arange(W)[:, None] // dh == jnp.arange(W)[None, :] // dh).astype(BF16)
    tri = (jnp.arange(LANES)[:, None] <= jnp.arange(LANES)[None, :]).astype(BF16)
    kspec = pl.BlockSpec((1, T, dh), lambda b, i: (b, 0, 0))
    return pl.pallas_call(
        functools.partial(_attn_prompt_kernel, TQ=TQ, L=T, dh=dh, iw_off=iw_off, n_sel=n_sel, n_buckets=n_buckets),
        grid=(B, nq),
        in_specs=[pl.BlockSpec((TQ, W), lambda b, i: (b * nq + i, q_block)),
                  pl.BlockSpec((TQ, W), lambda b, i: (b * nq + i, iq_block)),
                  pl.BlockSpec((TQ, tail_w), lambda b, i: (b * nq + i, tail_block)),
                  kspec, kspec, kspec,
                  pl.BlockSpec((1, W), lambda b, i: (0, 0)),
                  pl.BlockSpec((W, W), lambda b, i: (0, 0)),
                  pl.BlockSpec((LANES, LANES), lambda b, i: (0, 0))],
        out_specs=pl.BlockSpec((TQ, W), lambda b, i: (b * nq + i, 0)),
        out_shape=jax.ShapeDtypeStruct((B * T, W), BF16),
        scratch_shapes=[pltpu.VMEM((T, dh), BF16), pltpu.VMEM((T, dh), BF16), pltpu.VMEM((T, dh), BF16),
                        pltpu.VMEM((TQ, T), F32)],
        compiler_params=_cparams(("arbitrary", "arbitrary")),
        name="attn_prompt",
    )(z, z, z, kn, v, ik, gq_tiled, eye, tri)


def _xattn_kernel(q_ref, mk_ref, mv_ref, g_ref, o_ref, *, G, T, dh, scale):
    g = g_ref[...]
    for s in range(G):
        rows = slice(s * T, (s + 1) * T)
        for h in range(X_HEADS):
            cols = slice(h * dh, (h + 1) * dh)
            qh = q_ref[rows, cols]
            ms = jnp.mean(qh * qh, axis=-1, keepdims=True)
            qn = (qh * lax.rsqrt(ms + EPS) * g).astype(BF16)
            lg = _dot_nt(qn, mk_ref[s, :, cols].astype(BF16)) * scale
            m = jnp.max(lg, axis=-1, keepdims=True)
            e = jnp.exp(lg - m)
            den = jnp.sum(e, axis=-1, keepdims=True)
            o = _dot(e.astype(BF16), mv_ref[s, :, cols].astype(BF16)) / den
            o_ref[rows, cols] = o.astype(o_ref.dtype)


def _xattn_rows_kernel(q_ref, mk_ref, mv_ref, g_ref, o_ref, *, G, T, dh, scale):
    g = g_ref[...]
    H = X_HEADS
    n = mk_ref.shape[1]
    own = (lax.broadcasted_iota(I32, (H * T, n), 1) % H) == (lax.broadcasted_iota(I32, (H * T, n), 0) // T)
    for s in range(G):
        rows = slice(s * T, (s + 1) * T)
        parts = []
        for h in range(H):
            qh = q_ref[rows, h * dh:(h + 1) * dh]
            ms = jnp.mean(qh * qh, axis=-1, keepdims=True)
            parts.append(qh * lax.rsqrt(ms + EPS) * g)
        qs = jnp.concatenate(parts, axis=0).astype(BF16)
        lg = jnp.where(own, _dot_nt(qs, mk_ref[s].astype(BF16)) * scale, NEG_INF)
        m = jnp.max(lg, axis=-1, keepdims=True)
        e = jnp.exp(lg - m)
        den = jnp.sum(e, axis=-1, keepdims=True)
        o = _dot(e.astype(BF16), mv_ref[s].astype(BF16)) / den
        for h in range(H):
            o_ref[rows, h * dh:(h + 1) * dh] = o[h * T:(h + 1) * T, :].astype(o_ref.dtype)


def cross_attention(qx, row0, Bt, T, G, TQ, mem_k, mem_v, layer, gq):
    Wx = qx.shape[1]
    dh = Wx // X_HEADS
    n_rows, wm = mem_k.shape[2:]
    assert G == 1 or TQ == T
    R = G * TQ
    nt = T // TQ
    rb0 = row0 // R
    mspec = pl.BlockSpec((None, G, n_rows, wm), lambda i, t: (layer, i, 0, 0))
    body = _xattn_kernel if wm == Wx else _xattn_rows_kernel
    return pl.pallas_call(
        functools.partial(body, G=G, T=TQ, dh=dh, scale=dh ** -0.5),
        grid=(Bt // G, nt),
        in_specs=[pl.BlockSpec((R, Wx), lambda i, t: (rb0 + i * nt + t, 0)), mspec, mspec,
                  pl.BlockSpec((1, dh), lambda i, t: (0, 0))],
        out_specs=pl.BlockSpec((R, Wx), lambda i, t: (i * nt + t, 0)),
        out_shape=jax.ShapeDtypeStruct((Bt * T, Wx), BF16),
        compiler_params=_cparams(("arbitrary", "arbitrary")),
        name="cross_attention",
    )(qx, mem_k, mem_v, gq.reshape(1, dh))


def _attn_sample_kernel(pt_ref, q_ref, iq_ref, tail_ref, kn_ref, vn_ref, ikn_ref, ck_hbm, cv_hbm, cik_hbm,
                        gq_ref, e_ref, tri_ref, o_ref,
                        kp, vp, ikp, knp, vnp, iknp, qs_scr, iqs_scr, iw_scr, score, bias, sems,
                        *, G, T, dh, layer, n_pages, hp, iw_off, n_sel):
    step = pl.program_id(0)
    R = G * T
    H = N_HEADS_C
    LP = n_pages * hp
    LC = LP + LANES

    def page_copies(g, p):
        page = pt_ref[step * G + g, p]
        dst_rows = pl.ds(pl.multiple_of(p * hp, hp), hp)
        return [pltpu.make_async_copy(src.at[layer, page], dst.at[g, dst_rows, :], sems.at[s, g])
                for s, (src, dst) in enumerate(((ck_hbm, kp), (cv_hbm, vp), (cik_hbm, ikp)))]

    def start_all(i, c):
        g = i // n_pages
        for cp in page_copies(g, i - g * n_pages):
            cp.start()
        return c

    def wait_seq(g):
        def wait_page(p, c):
            for cp in page_copies(g, p):
                cp.wait()
            return c
        lax.fori_loop(0, n_pages, wait_page, 0)

    lax.fori_loop(0, G * n_pages, start_all, 0)

    @pl.when(step == 0)
    def _():
        knp[...] = jnp.zeros(knp.shape, F32)
        vnp[...] = jnp.zeros(vnp.shape, F32)
        iknp[...] = jnp.zeros(iknp.shape, F32)

    scale = dh ** -0.5
    q = q_ref[...]
    q2 = q * q
    hi = q2.astype(BF16)
    lo = (q2 - hi.astype(F32)).astype(BF16)
    ss = _dot(hi, e_ref[...]) + _dot(lo, e_ref[...])
    qs_scr[...] = q * lax.rsqrt(ss * (1.0 / dh) + EPS) * gq_ref[...] * scale
    iqs_scr[...] = iq_ref[...] * scale
    iw_blk = (iw_off // LANES) * LANES
    iw_lane = iw_off - iw_blk
    iw_scr[...] = tail_ref[:, iw_blk:iw_blk + LANES]

    new_ok = lax.broadcasted_iota(I32, (R, LANES), 1) <= lax.broadcasted_iota(I32, (R, LANES), 0) % T

    def stack_heads(ref, rows):
        return jnp.concatenate([ref[rows, h * dh:(h + 1) * dh] for h in range(H)], axis=0).astype(BF16)

    def score_body(g, c):
        wait_seq(g)
        rows = pl.ds(pl.multiple_of(g * T, T), T)
        iknp[0:T, :] = ikn_ref[rows, :]
        qs = stack_heads(iqs_scr, rows)
        lg_p = _dot_nt(qs, ikp[g].astype(BF16))
        lg_n = _dot_nt(qs, iknp[...].astype(BF16))
        iw = iw_scr[rows, :] * (IDX_HEADS ** -0.5)
        sp = sn = None
        for h in range(IDX_HEADS):
            w = iw[:, iw_lane + h:iw_lane + h + 1]
            hs = slice(h * T, (h + 1) * T)
            tp = jnp.maximum(lg_p[hs], 0.0) * w
            tn = jnp.maximum(lg_n[hs], 0.0) * w
            sp, sn = (tp, tn) if sp is None else (sp + tp, sn + tn)
        score[rows, 0:LP] = sp
        score[rows, LP:LC] = sn
        return c

    lax.fori_loop(0, G, score_body, 0)
    score[:, LP:LC] = jnp.where(new_ok, score[:, LP:LC], NEG_INF)

    key = _sort_keys(score[...])
    thr = _kth_largest_key(key, n_sel)
    gt = key > thr
    eq = key == thr
    need = float(n_sel) - jnp.sum(jnp.where(gt, 1.0, 0.0), axis=-1, keepdims=True)
    pref, _ = _blocked_prefix(jnp.where(eq, 1.0, 0.0), tri_ref[...], jnp.zeros((R, 1), F32))
    pref = jnp.concatenate(pref, axis=1)
    bias[...] = jnp.where(gt | (eq & (pref <= need)), 0.0, NEG_INF)
    bias[:, LP:LC] = jnp.where(new_ok, bias[:, LP:LC], NEG_INF)

    def attn_body(g, c):
        rows = pl.ds(pl.multiple_of(g * T, T), T)
        knp[0:T, :] = kn_ref[rows, :]
        vnp[0:T, :] = vn_ref[rows, :]
        qs = stack_heads(qs_scr, rows)
        bh = jnp.concatenate([bias[rows, :]] * H, axis=0)
        lg_p = _dot_nt(qs, kp[g].astype(BF16)) + bh[:, 0:LP]
        lg_n = _dot_nt(qs, knp[...].astype(BF16)) + bh[:, LP:LC]
        m = jnp.maximum(jnp.max(lg_p, axis=-1, keepdims=True), jnp.max(lg_n, axis=-1, keepdims=True))
        ep = jnp.exp(lg_p - m)
        en = jnp.exp(lg_n - m)
        den = jnp.sum(ep, axis=-1, keepdims=True) + jnp.sum(en, axis=-1, keepdims=True)
        o = (_dot(ep.astype(BF16), vp[g].astype(BF16)) + _dot(en.astype(BF16), vnp[...].astype(BF16))) / den
        for h in range(H):
            o_ref[rows, h * dh:(h + 1) * dh] = o[h * T:(h + 1) * T, :].astype(o_ref.dtype)
        return c

    lax.fori_loop(0, G, attn_body, 0)


def attn_sample(z, row0, q_block, iq_block, tail_block, tail_w, iw_off, kn, v, ik, caches, page_table, layer,
                gq_tiled, Bs, T, W, dh, G=8):
    n_pages = page_table.shape[1]
    page = caches[0].shape[2]
    R = G * T
    rb0 = row0 // R
    LP = n_pages * page
    LC = LP + LANES
    n_sel = min(TOPK_MAX, (LP + T) // 4)
    eye = (jnp.arange(W)[:, None] // dh == jnp.arange(W)[None, :] // dh).astype(BF16)
    tri = (jnp.arange(LANES)[:, None] <= jnp.arange(LANES)[None, :]).astype(BF16)
    any_spec = pl.BlockSpec(memory_space=pl.ANY)
    nspec = pl.BlockSpec((R, dh), lambda i, pt: (rb0 + i, 0))
    grid_spec = pltpu.PrefetchScalarGridSpec(
        num_scalar_prefetch=1,
        grid=(Bs // G,),
        in_specs=[pl.BlockSpec((R, W), lambda i, pt: (rb0 + i, q_block)),
                  pl.BlockSpec((R, W), lambda i, pt: (rb0 + i, iq_block)),
                  pl.BlockSpec((R, tail_w), lambda i, pt: (rb0 + i, tail_block)),
                  nspec, nspec, nspec, any_spec, any_spec, any_spec,
                  pl.BlockSpec((1, W), lambda i, pt: (0, 0)),
                  pl.BlockSpec((W, W), lambda i, pt: (0, 0)),
                  pl.BlockSpec((LANES, LANES), lambda i, pt: (0, 0))],
        out_specs=pl.BlockSpec((R, W), lambda i, pt: (i, 0)),
        scratch_shapes=[pltpu.VMEM((G, LP, dh), F32), pltpu.VMEM((G, LP, dh), F32), pltpu.VMEM((G, LP, dh), F32),
                        pltpu.VMEM((LANES, dh), F32), pltpu.VMEM((LANES, dh), F32), pltpu.VMEM((LANES, dh), F32),
                        pltpu.VMEM((R, W), F32), pltpu.VMEM((R, W), F32), pltpu.VMEM((R, LANES), F32),
                        pltpu.VMEM((R, LC), F32), pltpu.VMEM((R, LC), F32),
                        pltpu.SemaphoreType.DMA((3, G))],
    )
    return pl.pallas_call(
        functools.partial(_attn_sample_kernel, G=G, T=T, dh=dh, layer=layer, n_pages=n_pages, hp=page,
                          iw_off=iw_off, n_sel=n_sel),
        grid_spec=grid_spec,
        out_shape=jax.ShapeDtypeStruct((Bs * T, W), BF16),
        compiler_params=_cparams(("arbitrary",)),
        name="attn_sample",
    )(page_table, z, z, z, kn, v, ik, *caches, gq_tiled, eye, tri)


def _block_diag(blocks):
    G, a, b = blocks.shape
    eye = jnp.eye(G, dtype=blocks.dtype)
    return (eye[:, None, :, None] * blocks[:, :, None, :]).reshape(G * a, G * b)


def _lane_blocks(a):
    rows, n = a.shape
    return a.reshape(rows, n // LANES, LANES).transpose(1, 0, 2)


def kernel(x_prompt, x_sample, cache_attn_k, cache_attn_v, cache_idx_k, cache_mem_k, cache_mem_v, state_pool,
           state_conv, state_ssm_re, state_ssm_im, page_table, mem_prompt, norm_g, ffn_in, ffn_out, w_in,
           q_norm_g, k_norm_g, pool_mix, pool_scale, conv_w, ssm_a_re, ssm_a_im, ssm_log_step, ssm_b_re,
           ssm_b_im, ssm_c_re, ssm_c_im, ssm_d, ssm_glu_w, ssm_glu_b, w_branch, w_gate, b_gate, w_o,
           mem_norm_g, w_xq, w_xk, w_xv, xq_norm_g, xk_norm_g, w_xo):
    B, T, D = x_prompt.shape
    Bs, Ts, _ = x_sample.shape
    depth = norm_g.shape[0]
    Mp, Ms = B * T, Bs * Ts
    W = pool_scale.shape[1]
    dh = k_norm_g.shape[1]
    d_ff = ffn_out.shape[2]
    n_mem = mem_prompt.shape[1]
    Wx = w_xq.shape[2]
    SG, SN = ssm_a_re.shape[1:]
    NB = SG * SN // LANES
    past_len = page_table.shape[1] * cache_attn_k.shape[2]
    assert cache_idx_k.shape[-1] == dh and W == N_HEADS_C * dh == IDX_HEADS * dh
    TM = 1024

    o_k = 5 * W
    o_iq = o_k + 2 * dh
    o_ik = o_iq + W
    o_xs = o_ik + dh + IDX_HEADS
    tail_w = 2 * LANES
    Q_BLK, IQ_BLK, XS_BLK = 4, 5, 6
    TAIL_BLK = 7 * W // tail_w
    IW_OFF = 3 * dh
    n_z = 7 * W + tail_w

    h = jnp.concatenate([x_prompt.reshape(Mp, D), x_sample.reshape(Ms, D)], axis=0)
    caches = (cache_attn_k, cache_attn_v, cache_idx_k)
    mem_rows = mem_prompt.reshape(B * n_mem, D)
    outs = [[] for _ in range(16)]

    def ffn(h, l, i, g):
        n = rmsnorm_rows(h, g)
        act = matmul(n, [(ffn_in, (l, i), 0), (ffn_in, (l, i), d_ff // 512)], n_out=d_ff, tn=512, tm=TM,
                     mode="swiglu", out_dtype=BF16, name="ffn_in")
        return matmul(act, [(ffn_out, (l, i), 0)], n_out=D, tn=512, tm=TM, mode="residual", aux=h, scale=0.5,
                      name="ffn_out")

    for l in range(depth):
        h = ffn(h, l, 0, norm_g[l, 0])

        u = rmsnorm_rows(h, norm_g[l, 1])
        wl = w_in[l]
        w_in2 = jnp.concatenate([wl[:, 0:o_k], wl[:, o_iq:o_iq + W], wl[:, o_xs:o_xs + W], wl[:, o_k:o_k + 2 * dh],
                                 wl[:, o_ik:o_ik + dh + IDX_HEADS],
                                 jnp.zeros((D, tail_w - 3 * dh - IDX_HEADS), F32)], axis=1)
        z = matmul(u, [(w_in2, (), 0)], n_out=n_z, tn=768, tm=TM, name="in_proj")
        gates = matmul(u, [(w_gate, (l,), 0)], n_out=4 * D, tn=512, tm=TM, mode="bias_sigmoid",
                       aux=b_gate[l].reshape(1, 4 * D), out_dtype=BF16, name="gates")

        wmix = pool_mix[l].astype(BF16)
        ya_p, yb_p, npool_p, nconv_p = pool_conv(z, 0, B, T, 1, 512, 0, None, None, wmix, pool_scale[l], conv_w[l], W)
        ya_s, yb_s, npool_s, nconv_s = pool_conv(z, Mp, Bs, Ts, 16, Ts, past_len, state_pool[l], state_conv[l],
                                                 wmix, pool_scale[l], conv_w[l], W)

        kn, vv, ik = kv_prep(z, TAIL_BLK, tail_w, k_norm_g[l], dh)
        gq_tiled = jnp.tile(q_norm_g[l], N_HEADS_C).reshape(1, W)
        yc_p = attn_prompt(z, Q_BLK, IQ_BLK, TAIL_BLK, tail_w, IW_OFF, kn[:Mp].reshape(B, T, dh),
                           vv[:Mp].reshape(B, T, dh), ik[:Mp].reshape(B, T, dh), gq_tiled, B, T, W, dh)
        yc_s = attn_sample(z, Mp, Q_BLK, IQ_BLK, TAIL_BLK, tail_w, IW_OFF, kn, vv, ik, caches, page_table, l,
                           gq_tiled, Bs, Ts, W, dh)

        ab_re, ab_im, co_re, co_im = ssm_params(ssm_a_re[l], ssm_a_im[l], ssm_log_step[l])
        flat = lambda a: a.reshape(1, SG * SN)
        lev_re, lev_im, pw_re, pw_im = ssm_tables(flat(ab_re), flat(ab_im), SSM_CHUNK)
        rep = lambda a: _lane_blocks(jnp.broadcast_to(flat(a), (SUBLANES, SG * SN)))
        consts = (
            _lane_blocks(_block_diag(ssm_b_re[l].transpose(0, 2, 1))).astype(BF16),
            _lane_blocks(_block_diag(ssm_b_im[l].transpose(0, 2, 1))).astype(BF16),
            _block_diag(ssm_c_re[l].transpose(0, 2, 1)).reshape(NB, LANES, W).astype(BF16),
            _block_diag(ssm_c_im[l].transpose(0, 2, 1)).reshape(NB, LANES, W).astype(BF16),
            rep(co_re), rep(co_im), _lane_blocks(lev_re), _lane_blocks(lev_im),
            _lane_blocks(pw_re), _lane_blocks(pw_im),
            ssm_d[l].reshape(1, W), ssm_glu_w[l].astype(BF16), ssm_glu_b[l].reshape(1, W))
        yd_p, sre_p, sim_p = ssm_mixer(z, XS_BLK, 0, B, T, consts, None)
        h0 = (state_ssm_re[l].reshape(Bs, NB, LANES), state_ssm_im[l].reshape(Bs, NB, LANES))
        yd_s, sre_s, sim_s = ssm_mixer(z, XS_BLK, Mp, Bs, Ts, consts, h0)

        cat = lambda a, b: jnp.concatenate([a, b], axis=0)
        merged = gated_merge([cat(ya_p, ya_s), cat(yb_p, yb_s), cat(yc_p, yc_s), cat(yd_p, yd_s)], gates, w_branch, l)
        h = matmul(merged, [(w_o, (l,), 0)], n_out=D, tn=512, tm=TM, mode="residual", aux=h, scale=1.0, name="w_o")

        mn = rmsnorm_rows(mem_rows, mem_norm_g[l])
        mk = matmul(mn, [(w_xk, (l,), 0)], n_out=Wx, tn=Wx, tm=B * n_mem, mode="headnorm",
                    aux=jnp.tile(xk_norm_g[l], X_HEADS).reshape(1, Wx), group=Wx // X_HEADS, name="mem_k")
        mv = matmul(mn, [(w_xv, (l,), 0)], n_out=Wx, tn=Wx, tm=B * n_mem, name="mem_v")
        n2 = rmsnorm_rows(h, norm_g[l, 2])
        qx = matmul(n2, [(w_xq, (l,), 0)], n_out=Wx, tn=Wx, tm=TM, name="w_xq")
        xa_p = cross_attention(qx, 0, B, T, 1, 512, mk.reshape(1, B, n_mem, Wx), mv.reshape(1, B, n_mem, Wx), 0,
                               xq_norm_g[l])
        xa_s = cross_attention(qx, Mp, Bs, Ts, 8, Ts, cache_mem_k.reshape(depth, Bs, n_mem * X_HEADS, Wx // X_HEADS),
                               cache_mem_v.reshape(depth, Bs, n_mem * X_HEADS, Wx // X_HEADS), l, xq_norm_g[l])
        h = matmul(cat(xa_p, xa_s), [(w_xo, (l,), 0)], n_out=D, tn=512, tm=TM, mode="residual", aux=h, scale=1.0,
                   name="w_xo")

        h = ffn(h, l, 1, norm_g[l, 3])

        xh = Wx // X_HEADS
        layer_out = (kn[:Mp].reshape(B, T, dh), vv[:Mp].reshape(B, T, dh), ik[:Mp].reshape(B, T, dh),
                     mk.reshape(B, n_mem, X_HEADS, xh), mv.reshape(B, n_mem, X_HEADS, xh), npool_p, nconv_p,
                     sre_p.reshape(B, SG, SN), sim_p.reshape(B, SG, SN),
                     kn[Mp:].reshape(Bs, Ts, dh), vv[Mp:].reshape(Bs, Ts, dh), ik[Mp:].reshape(Bs, Ts, dh),
                     npool_s, nconv_s, sre_s.reshape(Bs, SG, SN), sim_s.reshape(Bs, SG, SN))
        for acc, val in zip(outs, layer_out):
            acc.append(val)

    return (h[:Mp].reshape(B, T, D), h[Mp:].reshape(Bs, Ts, D)) + tuple(jnp.stack(o) for o in outs)
```

```python
import functools
import math

import jax
import jax.numpy as jnp
from jax import lax
from jax.experimental import pallas as pl
from jax.experimental.pallas import tpu as pltpu

F32, BF16, I32 = jnp.float32, jnp.bfloat16, jnp.int32
EPS = 1e-6
NEG_INF = float("-inf")

V7X_VMEM_BYTES = 64 * 1024 * 1024
VMEM_LIMIT = V7X_VMEM_BYTES - 8 * 1024 * 1024
LANES = 128
SUBLANES = 8

POOL_WINDOWS = (2, 4, 8, 16)
POOL_HIST = 16
CONV_WIDTH = 3
CONV_HIST = 8
N_HEADS_C = 8
IDX_HEADS = 8
TOPK_MAX = 256
X_HEADS = 4
SSM_CHUNK = 128


def _cparams(sem, vmem=VMEM_LIMIT):
    return pltpu.CompilerParams(dimension_semantics=sem, vmem_limit_bytes=vmem)


def _dot(a, b):
    return jnp.dot(a, b, preferred_element_type=F32)


def _dot_nt(a, b):
    return lax.dot_general(a, b, (((1,), (1,)), ((), ())), preferred_element_type=F32)


def _rmsnorm_kernel(x_ref, g_ref, o_ref):
    x = x_ref[...]
    ms = jnp.mean(x * x, axis=-1, keepdims=True)
    o_ref[...] = (x * lax.rsqrt(ms + EPS) * g_ref[...]).astype(o_ref.dtype)


def rmsnorm_rows(x, g, tm=512):
    M, D = x.shape
    return pl.pallas_call(
        _rmsnorm_kernel,
        grid=(M // tm,),
        in_specs=[pl.BlockSpec((tm, D), lambda i: (i, 0)), pl.BlockSpec((1, D), lambda i: (0, 0))],
        out_specs=pl.BlockSpec((tm, D), lambda i: (i, 0)),
        out_shape=jax.ShapeDtypeStruct((M, D), BF16),
        compiler_params=_cparams(("arbitrary",)),
        name="rmsnorm",
    )(x, g.reshape(1, D))


def _mm_kernel(*refs, nw, mode, scale, sub, group):
    x_ref = refs[0]
    w_refs = refs[1:1 + nw]
    p = 1 + nw
    aux_ref = None
    if mode in ("bias_sigmoid", "residual", "headnorm"):
        aux_ref = refs[p]
        p += 1
    o_ref = refs[p]
    scr = refs[p + 1:p + 1 + nw]

    @pl.when(pl.program_id(1) == 0)
    def _():
        for w_ref, s in zip(w_refs, scr):
            s[...] = w_ref[...].astype(BF16)

    tm = x_ref.shape[0]

    def body(r, carry):
        rows = pl.ds(pl.multiple_of(r * sub, sub), sub)
        x = x_ref[rows, :]
        acc = [_dot(x, s[...]) for s in scr]
        if mode == "swiglu":
            a, b = acc
            y = (a * jax.nn.sigmoid(a)) * b
        elif mode == "bias_sigmoid":
            y = jax.nn.sigmoid(acc[0] + aux_ref[...])
        elif mode == "residual":
            y = aux_ref[rows, :] + scale * acc[0]
        elif mode == "headnorm":
            a = acc[0]
            parts = []
            for h in range(a.shape[1] // group):
                ah = a[:, h * group:(h + 1) * group]
                ms = jnp.mean(ah * ah, axis=-1, keepdims=True)
                parts.append(ah * lax.rsqrt(ms + EPS))
            y = jnp.concatenate(parts, axis=1) * aux_ref[...]
        else:
            y = acc[0]
        o_ref[rows, :] = y.astype(o_ref.dtype)
        return carry

    lax.fori_loop(0, tm // sub, body, 0, unroll=True)


def matmul(x, weights, *, n_out, tn, tm, mode="plain", aux=None, scale=1.0, out_dtype=F32, group=LANES,
           name="matmul"):
    M, K = x.shape
    nw = len(weights)
    sub = min(tm, 256)
    in_specs = [pl.BlockSpec((tm, K), lambda j, i: (i, 0))]
    args = [x]
    for arr, lead, coff in weights:
        nl = len(lead)
        in_specs.append(pl.BlockSpec((None,) * nl + (K, tn),
                                     functools.partial(lambda j, i, lead, coff: (*lead, 0, coff + j), lead=lead, coff=coff)))
        args.append(arr)
    if mode in ("bias_sigmoid", "headnorm"):
        in_specs.append(pl.BlockSpec((1, tn), lambda j, i: (0, j)))
        args.append(aux)
    elif mode == "residual":
        in_specs.append(pl.BlockSpec((tm, tn), lambda j, i: (i, j)))
        args.append(aux)
    return pl.pallas_call(
        functools.partial(_mm_kernel, nw=nw, mode=mode, scale=scale, sub=sub, group=group),
        grid=(n_out // tn, M // tm),
        in_specs=in_specs,
        out_specs=pl.BlockSpec((tm, tn), lambda j, i: (i, j)),
        out_shape=jax.ShapeDtypeStruct((M, n_out), out_dtype),
        scratch_shapes=[pltpu.VMEM((K, tn), BF16) for _ in range(nw)],
        compiler_params=_cparams(("arbitrary", "arbitrary")),
        name=name,
    )(*args)


def _merge_kernel(*refs, nb):
    br = refs[0:nb]
    gt = refs[nb:2 * nb]
    ws = refs[2 * nb:3 * nb]
    o_ref = refs[3 * nb]
    scr = refs[3 * nb + 1:3 * nb + 1 + nb]

    @pl.when(pl.program_id(1) == 0)
    def _():
        for w_ref, s in zip(ws, scr):
            s[...] = w_ref[...].astype(BF16)

    tm = o_ref.shape[0]
    sub = min(tm, 256)

    def body(r, carry):
        rows = pl.ds(pl.multiple_of(r * sub, sub), sub)
        acc = None
        for k in range(nb):
            t = gt[k][rows, :].astype(F32) * _dot(br[k][rows, :], scr[k][...])
            acc = t if acc is None else acc + t
        o_ref[rows, :] = acc.astype(o_ref.dtype)
        return carry

    lax.fori_loop(0, tm // sub, body, 0, unroll=True)


def gated_merge(branches, gates, w_branch, layer, *, tn=512, tm=1024):
    nb = len(branches)
    M, W = branches[0].shape
    D = w_branch.shape[-1]
    nj = D // tn
    in_specs = [pl.BlockSpec((tm, W), lambda j, i: (i, 0)) for _ in range(nb)]
    in_specs += [pl.BlockSpec((tm, tn), functools.partial(lambda j, i, k: (i, k * nj + j), k=k)) for k in range(nb)]
    in_specs += [pl.BlockSpec((None, None, W, tn), functools.partial(lambda j, i, k: (layer, k, 0, j), k=k))
                 for k in range(nb)]
    return pl.pallas_call(
        functools.partial(_merge_kernel, nb=nb),
        grid=(nj, M // tm),
        in_specs=in_specs,
        out_specs=pl.BlockSpec((tm, tn), lambda j, i: (i, j)),
        out_shape=jax.ShapeDtypeStruct((M, D), BF16),
        scratch_shapes=[pltpu.VMEM((W, tn), BF16) for _ in range(nb)],
        compiler_params=_cparams(("arbitrary", "arbitrary")),
        name="gated_merge",
    )(*branches, *([gates] * nb), *([w_branch] * nb))


def _kvprep_kernel(t_ref, g_ref, k_ref, v_ref, ik_ref, *, dh):
    t = t_ref[...]
    k = t[:, 0:dh]
    ms = jnp.mean(k * k, axis=-1, keepdims=True)
    k_ref[...] = k * lax.rsqrt(ms + EPS) * g_ref[...]
    v_ref[...] = t[:, dh:2 * dh]
    ik_ref[...] = t[:, 2 * dh:3 * dh]


def kv_prep(z, tail_block, tail_w, gk, dh, tm=1024):
    M = z.shape[0]
    out = jax.ShapeDtypeStruct((M, dh), F32)
    return pl.pallas_call(
        functools.partial(_kvprep_kernel, dh=dh),
        grid=(M // tm,),
        in_specs=[pl.BlockSpec((tm, tail_w), lambda i: (i, tail_block)), pl.BlockSpec((1, dh), lambda i: (0, 0))],
        out_specs=[pl.BlockSpec((tm, dh), lambda i: (i, 0))] * 3,
        out_shape=[out, out, out],
        compiler_params=_cparams(("arbitrary",)),
        name="kv_prep",
    )(z, gk.reshape(1, dh))


def _poolconv_kernel(*refs, G, T, W, pos0, has_state):
    xp_ref, xc_ref, bg_ref, cg_ref = refs[0:4]
    p = 4
    if has_state:
        pbuf_ref, cbuf_ref = refs[4:6]
        p = 6
    wmix_ref, pscale_ref, convw_ref = refs[p:p + 3]
    ya_ref, yb_ref, npool_ref, nconv_ref = refs[p + 3:p + 7]
    fullp, fullc = refs[p + 7:p + 9]
    PH, CH = POOL_HIST, CONV_HIST
    nh = CONV_WIDTH - 1
    gw = W // len(POOL_WINDOWS)
    c = pl.program_id(1)

    @pl.when(c == 0)
    def _():
        if has_state:
            fullp[:, 1:PH, :] = pbuf_ref[...]
            fullc[:, CH - nh:CH, :] = cbuf_ref[...]
        else:
            fullp[:, 0:PH, :] = jnp.zeros((G, PH, W), F32)
            fullc[:, 0:CH, :] = jnp.zeros((G, CH, W), F32)

    @pl.when(c > 0)
    def _():
        fullp[:, 0:PH, :] = fullp[:, T:T + PH, :]
        fullc[:, 0:CH, :] = fullc[:, T:T + CH, :]

    fullp[:, PH:PH + T, :] = xp_ref[...].reshape(G, T, W)
    t_idx = lax.broadcasted_iota(I32, (1, T, 1), 1) + (c * T + (pos0 + 1))
    for gi, w in enumerate(POOL_WINDOWS):
        cols = slice(gi * gw, (gi + 1) * gw)
        acc = fullp[:, PH:PH + T, cols]
        for j in range(1, w):
            acc = acc + fullp[:, PH - j:PH - j + T, cols]
        cnt = jnp.minimum(t_idx, w).astype(F32)
        d = acc / cnt - fullp[:, PH:PH + T, cols]
        y = _dot(d.reshape(G * T, gw).astype(BF16), wmix_ref[gi]) * pscale_ref[:, cols]
        ya_ref[:, cols] = y.astype(ya_ref.dtype)
    npool_ref[...] = fullp[:, T + 1:T + PH, :]

    fullc[:, CH:CH + T, :] = (cg_ref[...] * xc_ref[...]).reshape(G, T, W)
    y = None
    for j in range(CONV_WIDTH):
        wj = convw_ref[j:j + 1, :].reshape(1, 1, W)
        term = wj * fullc[:, CH - nh + j:CH - nh + j + T, :]
        y = term if y is None else y + term
    yb_ref[...] = (bg_ref[...] * y.reshape(G * T, W)).astype(yb_ref.dtype)
    nconv_ref[...] = fullc[:, CH + T - nh:CH + T, :]


def pool_conv(z, row0, Bt, T, G, TC, pos0, pool_buf, conv_buf, wmix_bf16, pool_scale, conv_w, W):
    has_state = pool_buf is not None
    assert G == 1 or TC == T
    R = G * TC
    nc = T // TC
    rb0 = row0 // R
    nh = CONV_WIDTH - 1

    def zspec(cb):
        return pl.BlockSpec((R, W), functools.partial(lambda i, c, cb: (rb0 + i * nc + c, cb), cb=cb))

    in_specs = [zspec(0), zspec(1), zspec(2), zspec(3)]
    args = [z, z, z, z]
    if has_state:
        in_specs += [pl.BlockSpec((G, POOL_HIST - 1, W), lambda i, c: (i, 0, 0)),
                     pl.BlockSpec((G, nh, W), lambda i, c: (i, 0, 0))]
        args += [pool_buf, conv_buf]
    nwin = len(POOL_WINDOWS)
    in_specs += [pl.BlockSpec((nwin, W // nwin, W // nwin), lambda i, c: (0, 0, 0)),
                 pl.BlockSpec((1, W), lambda i, c: (0, 0)),
                 pl.BlockSpec((CONV_WIDTH, W), lambda i, c: (0, 0))]
    args += [wmix_bf16, pool_scale.reshape(1, W), conv_w]
    return pl.pallas_call(
        functools.partial(_poolconv_kernel, G=G, T=TC, W=W, pos0=pos0, has_state=has_state),
        grid=(Bt // G, nc),
        in_specs=in_specs,
        out_specs=[pl.BlockSpec((R, W), lambda i, c: (i * nc + c, 0)), pl.BlockSpec((R, W), lambda i, c: (i * nc + c, 0)),
                   pl.BlockSpec((G, POOL_HIST - 1, W), lambda i, c: (i, 0, 0)),
                   pl.BlockSpec((G, nh, W), lambda i, c: (i, 0, 0))],
        out_shape=[jax.ShapeDtypeStruct((Bt * T, W), BF16), jax.ShapeDtypeStruct((Bt * T, W), BF16),
                   jax.ShapeDtypeStruct((Bt, POOL_HIST - 1, W), F32), jax.ShapeDtypeStruct((Bt, nh, W), F32)],
        scratch_shapes=[pltpu.VMEM((G, POOL_HIST + TC, W), F32), pltpu.VMEM((G, CONV_HIST + TC, W), F32)],
        compiler_params=_cparams(("arbitrary", "arbitrary")),
        name="pool_conv",
    )(*args)


def _ssm_params_kernel(are_ref, aim_ref, ls_ref, abre_ref, abim_ref, core_ref, coim_ref):
    a_re = are_ref[...]
    a_im = aim_ref[...]
    step = jnp.exp(ls_ref[...])
    decay = jnp.exp(step * a_re)
    ab_re = decay * jnp.cos(step * a_im)
    ab_im = decay * jnp.sin(step * a_im)
    den = a_re * a_re + a_im * a_im
    nr = ab_re - 1.0
    abre_ref[...] = ab_re
    abim_ref[...] = ab_im
    core_ref[...] = (nr * a_re + ab_im * a_im) / den
    coim_ref[...] = (ab_im * a_re - nr * a_im) / den


def ssm_params(a_re, a_im, log_step):
    G, N = a_re.shape
    out = jax.ShapeDtypeStruct((G, N), F32)
    return pl.pallas_call(_ssm_params_kernel, out_shape=[out, out, out, out], name="ssm_params")(
        a_re, a_im, log_step.reshape(G, 1))


def _scan_levels(C):
    return [1 << k for k in range(int(math.log2(C)))]


def _ssm_tables_kernel(abre_ref, abim_ref, lre_ref, lim_ref, pre_ref, pim_ref, *, C):
    ar = abre_ref[...]
    ai = abim_ref[...]
    N = ar.shape[1]
    row = lax.broadcasted_iota(I32, (C, N), 0)
    hr = jnp.where(row == 0, ar, 0.0)
    hi = jnp.where(row == 0, ai, 0.0)
    lre_ref[...] = jnp.zeros(lre_ref.shape, F32)
    lim_ref[...] = jnp.zeros(lim_ref.shape, F32)
    for k, s in enumerate(_scan_levels(C)):
        lre_ref[k:k + 1, :] = ar
        lim_ref[k:k + 1, :] = ai
        sr = jnp.where(row >= s, pltpu.roll(hr, s, 0), 0.0)
        si = jnp.where(row >= s, pltpu.roll(hi, s, 0), 0.0)
        hr, hi = hr + ar * sr - ai * si, hi + ar * si + ai * sr
        ar, ai = ar * ar - ai * ai, 2.0 * ar * ai
    pre_ref[...] = hr
    pim_ref[...] = hi


def ssm_tables(ab_re, ab_im, C):
    N = ab_re.shape[1]
    nlev = len(_scan_levels(C))
    lev = jax.ShapeDtypeStruct((SUBLANES * ((nlev + SUBLANES - 1) // SUBLANES), N), F32)
    pw = jax.ShapeDtypeStruct((C, N), F32)
    return pl.pallas_call(functools.partial(_ssm_tables_kernel, C=C), out_shape=[lev, lev, pw, pw],
                          name="ssm_tables")(ab_re, ab_im)


def _gelu_tanh(x):
    return 0.5 * x * (1.0 + jnp.tanh(math.sqrt(2.0 / math.pi) * (x + 0.044715 * (x * x * x))))


def _ssm_kernel(*refs, R, T, NB, chained):
    (xs_ref, bre_ref, bim_ref, cre_ref, cim_ref, core_ref, coim_ref, lre_ref, lim_ref) = refs[0:9]
    p = 9
    if chained:
        pre_ref, pim_ref = refs[p:p + 2]
        p += 2
    else:
        h0r_ref, h0i_ref = refs[p:p + 2]
        p += 2
    d_ref, gw_ref, gb_ref = refs[p:p + 3]
    yd_ref, sre_ref, sim_ref = refs[p + 3:p + 6]
    p += 6
    if chained:
        car_ref, cai_ref = refs[p:p + 2]
    G = R // T
    xs = xs_ref[...]
    xb = xs.astype(BF16)
    row = lax.broadcasted_iota(I32, (R, LANES), 0)
    tpos = row % T
    levels = _scan_levels(T)

    if chained:
        c = pl.program_id(1)

        @pl.when(c == 0)
        def _():
            car_ref[...] = jnp.zeros(car_ref.shape, F32)
            cai_ref[...] = jnp.zeros(cai_ref.shape, F32)

    y = d_ref[...] * xs
    for cb in range(NB):
        pr = _dot(xb, bre_ref[cb])
        pi = _dot(xb, bim_ref[cb])
        cor = core_ref[cb][0:1, :]
        coi = coim_ref[cb][0:1, :]
        hr = cor * pr - coi * pi
        hi = cor * pi + coi * pr
        if not chained:
            ar = lre_ref[cb][0:1, :]
            ai = lim_ref[cb][0:1, :]
            h0r = jnp.broadcast_to(h0r_ref[:, cb:cb + 1, :], (G, T, LANES)).reshape(R, LANES)
            h0i = jnp.broadcast_to(h0i_ref[:, cb:cb + 1, :], (G, T, LANES)).reshape(R, LANES)
            first = tpos == 0
            hr = hr + jnp.where(first, ar * h0r - ai * h0i, 0.0)
            hi = hi + jnp.where(first, ar * h0i + ai * h0r, 0.0)
        for k, s in enumerate(levels):
            ar = lre_ref[cb][k:k + 1, :]
            ai = lim_ref[cb][k:k + 1, :]
            sr = jnp.where(tpos >= s, pltpu.roll(hr, s, 0), 0.0)
            si = jnp.where(tpos >= s, pltpu.roll(hi, s, 0), 0.0)
            hr, hi = hr + ar * sr - ai * si, hi + ar * si + ai * sr
        if chained:
            cr = car_ref[cb:cb + 1, :]
            ci = cai_ref[cb:cb + 1, :]
            pwr = pre_ref[cb]
            pwi = pim_ref[cb]
            hr, hi = hr + pwr * cr - pwi * ci, hi + pwr * ci + pwi * cr
            car_ref[cb:cb + 1, :] = hr[R - 1:R, :]
            cai_ref[cb:cb + 1, :] = hi[R - 1:R, :]
        else:
            sre_ref[:, cb:cb + 1, :] = hr.reshape(G, T, LANES)[:, T - 1:T, :]
            sim_ref[:, cb:cb + 1, :] = hi.reshape(G, T, LANES)[:, T - 1:T, :]
        y = y + _dot(hr.astype(BF16), cre_ref[cb]) - _dot(hi.astype(BF16), cim_ref[cb])

    z = _gelu_tanh(y)
    out = z * jax.nn.sigmoid(_dot(z.astype(BF16), gw_ref[...]) + gb_ref[...])
    yd_ref[...] = out.astype(yd_ref.dtype)

    if chained:
        @pl.when(c == pl.num_programs(1) - 1)
        def _():
            sre_ref[0] = car_ref[...]
            sim_ref[0] = cai_ref[...]


def ssm_mixer(z, xs_block, row0, Bt, T, consts, h0):
    (bre3, bim3, cre3, cim3, core3, coim3, lre3, lim3, pre3, pim3, dvec, gw, gb) = consts
    NB = bre3.shape[0]
    W = dvec.shape[1]
    chained = h0 is None
    R = SSM_CHUNK
    rb0 = row0 // R

    def full(a):
        nd = a.ndim
        return pl.BlockSpec(a.shape, lambda *_: (0,) * nd)

    if chained:
        nchunk = T // R
        grid = (Bt, nchunk)
        xs_spec = pl.BlockSpec((R, W), lambda b, c: (rb0 + b * nchunk + c, xs_block))
        st_args, st_specs = [pre3, pim3], [full(pre3), full(pim3)]
        yd_spec = pl.BlockSpec((R, W), lambda b, c: (b * nchunk + c, 0))
        s_spec = pl.BlockSpec((1, NB, LANES), lambda b, c: (b, 0, 0))
        scratch = [pltpu.VMEM((NB, LANES), F32), pltpu.VMEM((NB, LANES), F32)]
        sem = ("arbitrary", "arbitrary")
        Tk = R
    else:
        G = R // T
        grid = (Bt // G,)
        xs_spec = pl.BlockSpec((R, W), lambda i: (rb0 + i, xs_block))
        st_args = list(h0)
        st_specs = [pl.BlockSpec((G, NB, LANES), lambda i: (i, 0, 0))] * 2
        yd_spec = pl.BlockSpec((R, W), lambda i: (i, 0))
        s_spec = pl.BlockSpec((G, NB, LANES), lambda i: (i, 0, 0))
        scratch = []
        sem = ("arbitrary",)
        Tk = T
    shared = [bre3, bim3, cre3, cim3, core3, coim3, lre3, lim3]
    tailc = [dvec, gw, gb]
    s_shape = jax.ShapeDtypeStruct((Bt, NB, LANES), F32)
    return pl.pallas_call(
        functools.partial(_ssm_kernel, R=R, T=Tk, NB=NB, chained=chained),
        grid=grid,
        in_specs=[xs_spec] + [full(a) for a in shared] + st_specs + [full(a) for a in tailc],
        out_specs=[yd_spec, s_spec, s_spec],
        out_shape=[jax.ShapeDtypeStruct((Bt * T, W), BF16), s_shape, s_shape],
        scratch_shapes=scratch,
        compiler_params=_cparams(sem),
        name="ssm_chained" if chained else "ssm_stateful",
    )(z, *shared, *st_args, *tailc)


def _sort_keys(score):
    b = pltpu.bitcast(score, I32)
    key = jnp.where(b < 0, b ^ jnp.int32(0x7FFFFFFF), b)
    return jnp.where(key == -1, 0, key)


def _kth_largest_key(key, k):
    rows = key.shape[0]

    def body(i, t):
        cand = t + (jnp.int32(1) << (31 - i))
        cnt = jnp.sum(jnp.where(key >= cand, 1.0, 0.0), axis=-1, keepdims=True)
        return jnp.where(cnt >= float(k), cand, t)

    return lax.fori_loop(0, 32, body, jnp.full((rows, 1), -2 ** 31, I32))


def _blocked_prefix(eq, tri, offset):
    outs = []
    run = offset
    for j in range(eq.shape[1] // LANES):
        blk = eq[:, j * LANES:(j + 1) * LANES].astype(BF16)
        pj = _dot(blk, tri) + run
        outs.append(pj)
        run = pj[:, LANES - 1:LANES]
    return outs, run


def _attn_prompt_kernel(q_ref, iq_ref, tail_ref, k_ref, v_ref, ik_ref, gq_ref, e_ref, tri_ref, o_ref,
                        kb, vb, ikb, bias, *, TQ, L, dh, iw_off, n_sel, n_buckets):
    qb = pl.program_id(1)

    @pl.when(qb == 0)
    def _():
        kb[...] = k_ref[0].astype(BF16)
        vb[...] = v_ref[0].astype(BF16)
        ikb[...] = ik_ref[0].astype(BF16)

    scale = dh ** -0.5
    q = q_ref[...]
    q2 = q * q
    hi = q2.astype(BF16)
    lo = (q2 - hi.astype(F32)).astype(BF16)
    ss = _dot(hi, e_ref[...]) + _dot(lo, e_ref[...])
    qn = (q * lax.rsqrt(ss * (1.0 / dh) + EPS) * gq_ref[...] * scale).astype(BF16)
    iqs = (iq_ref[...] * scale).astype(BF16)
    iw = tail_ref[:, iw_off:iw_off + IDX_HEADS] * (IDX_HEADS ** -0.5)

    def attend(Lk):
        score = None
        for h in range(IDX_HEADS):
            lg = _dot_nt(iqs[:, h * dh:(h + 1) * dh], ikb[0:Lk, :])
            t = jnp.maximum(lg, 0.0) * iw[:, h:h + 1]
            score = t if score is None else score + t
        col = lax.broadcasted_iota(I32, (TQ, Lk), 1)
        qpos = qb * TQ + lax.broadcasted_iota(I32, (TQ, Lk), 0)
        causal = col <= qpos
        score = jnp.where(causal, score, NEG_INF)

        key = _sort_keys(score)
        thr = _kth_largest_key(key, n_sel)
        gt = key > thr
        eq = key == thr
        need = float(n_sel) - jnp.sum(jnp.where(gt, 1.0, 0.0), axis=-1, keepdims=True)
        pref, _ = _blocked_prefix(jnp.where(eq, 1.0, 0.0), tri_ref[...], jnp.zeros((TQ, 1), F32))
        pref = jnp.concatenate(pref, axis=1)
        sel = (gt | (eq & (pref <= need))) & causal
        bias[:, 0:Lk] = jnp.where(sel, 0.0, NEG_INF)

        for h in range(N_HEADS_C):
            lg = _dot_nt(qn[:, h * dh:(h + 1) * dh], kb[0:Lk, :]) + bias[:, 0:Lk]
            m = jnp.max(lg, axis=-1, keepdims=True)
            e = jnp.exp(lg - m)
            den = jnp.sum(e, axis=-1, keepdims=True)
            o = _dot(e.astype(BF16), vb[0:Lk, :]) / den
            o_ref[:, h * dh:(h + 1) * dh] = o.astype(o_ref.dtype)

    per = (L // TQ) // n_buckets
    for bk in range(n_buckets):
        pl.when(qb // per == bk)(functools.partial(attend, (bk + 1) * per * TQ))


def attn_prompt(z, q_block, iq_block, tail_block, tail_w, iw_off, kn, v, ik, gq_tiled, B, T, W, dh, TQ=256,
                n_buckets=4):
    nq = T // TQ
    n_sel = min(TOPK_MAX, T // 4)
    eye = (jnp.arange(W)[:, None] // dh == jnp.arange(W)[None, :] // dh).astype(BF16)
    tri = (jnp.arange(LANES)[:, None] <= jnp.arange(LANES)[None, :]).astype(BF16)
    kspec = pl.BlockSpec((1, T, dh), lambda b, i: (b, 0, 0))
    return pl.pallas_call(
        functools.partial(_attn_prompt_kernel, TQ=TQ, L=T, dh=dh, iw_off=iw_off, n_sel=n_sel, n_buckets=n_buckets),
        grid=(B, nq),
        in_specs=[pl.BlockSpec((TQ, W), lambda b, i: (b * nq + i, q_block)),
                  pl.BlockSpec((TQ, W), lambda b, i: (b * nq + i, iq_block)),
                  pl.BlockSpec((TQ, tail_w), lambda b, i: (b * nq + i, tail_block)),
                  kspec, kspec, kspec,
                  pl.BlockSpec((1, W), lambda b, i: (0, 0)),
                  pl.BlockSpec((W, W), lambda b, i: (0, 0)),
                  pl.BlockSpec((LANES, LANES), lambda b, i: (0, 0))],
        out_specs=pl.BlockSpec((TQ, W), lambda b, i: (b * nq + i, 0)),
        out_shape=jax.ShapeDtypeStruct((B * T, W), BF16),
        scratch_shapes=[pltpu.VMEM((T, dh), BF16), pltpu.VMEM((T, dh), BF16), pltpu.VMEM((T, dh), BF16),
                        pltpu.VMEM((TQ, T), F32)],
        compiler_params=_cparams(("arbitrary", "arbitrary")),
        name="attn_prompt",
    )(z, z, z, kn, v, ik, gq_tiled, eye, tri)


def _xattn_kernel(q_ref, mk_ref, mv_ref, g_ref, o_ref, *, G, T, dh, scale):
    g = g_ref[...]
    for s in range(G):
        rows = slice(s * T, (s + 1) * T)
        for h in range(X_HEADS):
            cols = slice(h * dh, (h + 1) * dh)
            qh = q_ref[rows, cols]
            ms = jnp.mean(qh * qh, axis=-1, keepdims=True)
            qn = (qh * lax.rsqrt(ms + EPS) * g).astype(BF16)
            lg = _dot_nt(qn, mk_ref[s, :, cols].astype(BF16)) * scale
            m = jnp.max(lg, axis=-1, keepdims=True)
            e = jnp.exp(lg - m)
            den = jnp.sum(e, axis=-1, keepdims=True)
            o = _dot(e.astype(BF16), mv_ref[s, :, cols].astype(BF16)) / den
            o_ref[rows, cols] = o.astype(o_ref.dtype)


def _xattn_rows_kernel(q_ref, mk_ref, mv_ref, g_ref, o_ref, *, G, T, dh, scale):
    g = g_ref[...]
    H = X_HEADS
    n = mk_ref.shape[1]
    own = (lax.broadcasted_iota(I32, (H * T, n), 1) % H) == (lax.broadcasted_iota(I32, (H * T, n), 0) // T)
    for s in range(G):
        rows = slice(s * T, (s + 1) * T)
        parts = []
        for h in range(H):
            qh = q_ref[rows, h * dh:(h + 1) * dh]
            ms = jnp.mean(qh * qh, axis=-1, keepdims=True)
            parts.append(qh * lax.rsqrt(ms + EPS) * g)
        qs = jnp.concatenate(parts, axis=0).astype(BF16)
        lg = jnp.where(own, _dot_nt(qs, mk_ref[s].astype(BF16)) * scale, NEG_INF)
        m = jnp.max(lg, axis=-1, keepdims=True)
        e = jnp.exp(lg - m)
        den = jnp.sum(e, axis=-1, keepdims=True)
        o = _dot(e.astype(BF16), mv_ref[s].astype(BF16)) / den
        for h in range(H):
            o_ref[rows, h * dh:(h + 1) * dh] = o[h * T:(h + 1) * T, :].astype(o_ref.dtype)


def cross_attention(qx, row0, Bt, T, G, TQ, mem_k, mem_v, layer, gq):
    Wx = qx.shape[1]
    dh = Wx // X_HEADS
    n_rows, wm = mem_k.shape[2:]
    assert G == 1 or TQ == T
    R = G * TQ
    nt = T // TQ
    rb0 = row0 // R
    mspec = pl.BlockSpec((None, G, n_rows, wm), lambda i, t: (layer, i, 0, 0))
    body = _xattn_kernel if wm == Wx else _xattn_rows_kernel
    return pl.pallas_call(
        functools.partial(body, G=G, T=TQ, dh=dh, scale=dh ** -0.5),
        grid=(Bt // G, nt),
        in_specs=[pl.BlockSpec((R, Wx), lambda i, t: (rb0 + i * nt + t, 0)), mspec, mspec,
                  pl.BlockSpec((1, dh), lambda i, t: (0, 0))],
        out_specs=pl.BlockSpec((R, Wx), lambda i, t: (i * nt + t, 0)),
        out_shape=jax.ShapeDtypeStruct((Bt * T, Wx), BF16),
        compiler_params=_cparams(("arbitrary", "arbitrary")),
        name="cross_attention",
    )(qx, mem_k, mem_v, gq.reshape(1, dh))


def _attn_sample_kernel(pt_ref, q_ref, iq_ref, tail_ref, kn_ref, vn_ref, ikn_ref, ck_hbm, cv_hbm, cik_hbm,
                        gq_ref, e_ref, tri_ref, o_ref,
                        kp, vp, ikp, knp, vnp, iknp, qs_scr, iqs_scr, iw_scr, score, bias, sems,
                        *, G, T, dh, layer, n_pages, hp, iw_off, n_sel):
    step = pl.program_id(0)
    R = G * T
    H = N_HEADS_C
    LP = n_pages * hp
    LC = LP + LANES

    def page_copies(g, p):
        page = pt_ref[step * G + g, p]
        dst_cols = pl.ds(pl.multiple_of(p * hp, hp), hp)
        return [pltpu.make_async_copy(src.at[layer, page], dst.at[g, :, dst_cols], sems.at[s, g])
                for s, (src, dst) in enumerate(((ck_hbm, kp), (cv_hbm, vp), (cik_hbm, ikp)))]

    def start_all(i, c):
        g = i // n_pages
        for cp in page_copies(g, i - g * n_pages):
            cp.start()
        return c

    def wait_seq(g):
        def wait_page(p, c):
            for cp in page_copies(g, p):
                cp.wait()
            return c
        lax.fori_loop(0, n_pages, wait_page, 0)

    lax.fori_loop(0, G * n_pages, start_all, 0)

    @pl.when(step == 0)
    def _():
        knp[...] = jnp.zeros(knp.shape, F32)
        vnp[...] = jnp.zeros(vnp.shape, F32)
        iknp[...] = jnp.zeros(iknp.shape, F32)

    scale = dh ** -0.5
    q = q_ref[...]
    q2 = q * q
    hi = q2.astype(BF16)
    lo = (q2 - hi.astype(F32)).astype(BF16)
    ss = _dot(hi, e_ref[...]) + _dot(lo, e_ref[...])
    qs_scr[...] = q * lax.rsqrt(ss * (1.0 / dh) + EPS) * gq_ref[...] * scale
    iqs_scr[...] = iq_ref[...] * scale
    iw_blk = (iw_off // LANES) * LANES
    iw_lane = iw_off - iw_blk
    iw_scr[...] = tail_ref[:, iw_blk:iw_blk + LANES]

    new_ok = lax.broadcasted_iota(I32, (R, LANES), 1) <= lax.broadcasted_iota(I32, (R, LANES), 0) % T

    def stack_heads(ref, rows):
        return jnp.concatenate([ref[rows, h * dh:(h + 1) * dh] for h in range(H)], axis=0).astype(BF16)

    def score_body(g, c):
        wait_seq(g)
        rows = pl.ds(pl.multiple_of(g * T, T), T)
        iknp[0:T, :] = ikn_ref[rows, :]
        qs = stack_heads(iqs_scr, rows)
        lg_p = _dot(qs, ikp[g].astype(BF16))
        lg_n = _dot_nt(qs, iknp[...].astype(BF16))
        iw = iw_scr[rows, :] * (IDX_HEADS ** -0.5)
        sp = sn = None
        for h in range(IDX_HEADS):
            w = iw[:, iw_lane + h:iw_lane + h + 1]
            hs = slice(h * T, (h + 1) * T)
            tp = jnp.maximum(lg_p[hs], 0.0) * w
            tn = jnp.maximum(lg_n[hs], 0.0) * w
            sp, sn = (tp, tn) if sp is None else (sp + tp, sn + tn)
        score[rows, 0:LP] = sp
        score[rows, LP:LC] = sn
        return c

    lax.fori_loop(0, G, score_body, 0)
    score[:, LP:LC] = jnp.where(new_ok, score[:, LP:LC], NEG_INF)

    key = _sort_keys(score[...])
    thr = _kth_largest_key(key, n_sel)
    gt = key > thr
    eq = key == thr
    need = float(n_sel) - jnp.sum(jnp.where(gt, 1.0, 0.0), axis=-1, keepdims=True)
    pref, _ = _blocked_prefix(jnp.where(eq, 1.0, 0.0), tri_ref[...], jnp.zeros((R, 1), F32))
    pref = jnp.concatenate(pref, axis=1)
    bias[...] = jnp.where(gt | (eq & (pref <= need)), 0.0, NEG_INF)
    bias[:, LP:LC] = jnp.where(new_ok, bias[:, LP:LC], NEG_INF)

    def attn_body(g, c):
        rows = pl.ds(pl.multiple_of(g * T, T), T)
        knp[0:T, :] = kn_ref[rows, :]
        vnp[0:T, :] = vn_ref[rows, :]
        qs = stack_heads(qs_scr, rows)
        bh = jnp.concatenate([bias[rows, :]] * H, axis=0)
        lg_p = _dot(qs, kp[g].astype(BF16)) + bh[:, 0:LP]
        lg_n = _dot_nt(qs, knp[...].astype(BF16)) + bh[:, LP:LC]
        m = jnp.maximum(jnp.max(lg_p, axis=-1, keepdims=True), jnp.max(lg_n, axis=-1, keepdims=True))
        ep = jnp.exp(lg_p - m)
        en = jnp.exp(lg_n - m)
        den = jnp.sum(ep, axis=-1, keepdims=True) + jnp.sum(en, axis=-1, keepdims=True)
        o = (_dot_nt(ep.astype(BF16), vp[g].astype(BF16)) + _dot(en.astype(BF16), vnp[...].astype(BF16))) / den
        for h in range(H):
            o_ref[rows, h * dh:(h + 1) * dh] = o[h * T:(h + 1) * T, :].astype(o_ref.dtype)
        return c

    lax.fori_loop(0, G, attn_body, 0)


def attn_sample(z, row0, q_block, iq_block, tail_block, tail_w, iw_off, kn, v, ik, caches, page_table, layer,
                gq_tiled, Bs, T, W, dh, G=16):
    n_pages = page_table.shape[1]
    page = caches[0].shape[2]
    caches_t = [c.transpose(0, 1, 3, 2) for c in caches]
    R = G * T
    rb0 = row0 // R
    LP = n_pages * page
    LC = LP + LANES
    n_sel = min(TOPK_MAX, (LP + T) // 4)
    eye = (jnp.arange(W)[:, None] // dh == jnp.arange(W)[None, :] // dh).astype(BF16)
    tri = (jnp.arange(LANES)[:, None] <= jnp.arange(LANES)[None, :]).astype(BF16)
    any_spec = pl.BlockSpec(memory_space=pl.ANY)
    nspec = pl.BlockSpec((R, dh), lambda i, pt: (rb0 + i, 0))
    grid_spec = pltpu.PrefetchScalarGridSpec(
        num_scalar_prefetch=1,
        grid=(Bs // G,),
        in_specs=[pl.BlockSpec((R, W), lambda i, pt: (rb0 + i, q_block)),
                  pl.BlockSpec((R, W), lambda i, pt: (rb0 + i, iq_block)),
                  pl.BlockSpec((R, tail_w), lambda i, pt: (rb0 + i, tail_block)),
                  nspec, nspec, nspec, any_spec, any_spec, any_spec,
                  pl.BlockSpec((1, W), lambda i, pt: (0, 0)),
                  pl.BlockSpec((W, W), lambda i, pt: (0, 0)),
                  pl.BlockSpec((LANES, LANES), lambda i, pt: (0, 0))],
        out_specs=pl.BlockSpec((R, W), lambda i, pt: (i, 0)),
        scratch_shapes=[pltpu.VMEM((G, dh, LP), F32), pltpu.VMEM((G, dh, LP), F32), pltpu.VMEM((G, dh, LP), F32),
                        pltpu.VMEM((LANES, dh), F32), pltpu.VMEM((LANES, dh), F32), pltpu.VMEM((LANES, dh), F32),
                        pltpu.VMEM((R, W), F32), pltpu.VMEM((R, W), F32), pltpu.VMEM((R, LANES), F32),
                        pltpu.VMEM((R, LC), F32), pltpu.VMEM((R, LC), F32),
                        pltpu.SemaphoreType.DMA((3, G))],
    )
    return pl.pallas_call(
        functools.partial(_attn_sample_kernel, G=G, T=T, dh=dh, layer=layer, n_pages=n_pages, hp=page,
                          iw_off=iw_off, n_sel=n_sel),
        grid_spec=grid_spec,
        out_shape=jax.ShapeDtypeStruct((Bs * T, W), BF16),
        compiler_params=_cparams(("arbitrary",)),
        name="attn_sample",
    )(page_table, z, z, z, kn, v, ik, *caches_t, gq_tiled, eye, tri)


def _block_diag(blocks):
    G, a, b = blocks.shape
    eye = jnp.eye(G, dtype=blocks.dtype)
    return (eye[:, None, :, None] * blocks[:, :, None, :]).reshape(G * a, G * b)


def _lane_blocks(a):
    rows, n = a.shape
    return a.reshape(rows, n // LANES, LANES).transpose(1, 0, 2)


def kernel(x_prompt, x_sample, cache_attn_k, cache_attn_v, cache_idx_k, cache_mem_k, cache_mem_v, state_pool,
           state_conv, state_ssm_re, state_ssm_im, page_table, mem_prompt, norm_g, ffn_in, ffn_out, w_in,
           q_norm_g, k_norm_g, pool_mix, pool_scale, conv_w, ssm_a_re, ssm_a_im, ssm_log_step, ssm_b_re,
           ssm_b_im, ssm_c_re, ssm_c_im, ssm_d, ssm_glu_w, ssm_glu_b, w_branch, w_gate, b_gate, w_o,
           mem_norm_g, w_xq, w_xk, w_xv, xq_norm_g, xk_norm_g, w_xo):
    B, T, D = x_prompt.shape
    Bs, Ts, _ = x_sample.shape
    depth = norm_g.shape[0]
    Mp, Ms = B * T, Bs * Ts
    W = pool_scale.shape[1]
    dh = k_norm_g.shape[1]
    d_ff = ffn_out.shape[2]
    n_mem = mem_prompt.shape[1]
    Wx = w_xq.shape[2]
    SG, SN = ssm_a_re.shape[1:]
    NB = SG * SN // LANES
    past_len = page_table.shape[1] * cache_attn_k.shape[2]
    assert cache_idx_k.shape[-1] == dh and W == N_HEADS_C * dh == IDX_HEADS * dh
    TM = 1024

    o_k = 5 * W
    o_iq = o_k + 2 * dh
    o_ik = o_iq + W
    o_xs = o_ik + dh + IDX_HEADS
    tail_w = 2 * LANES
    Q_BLK, IQ_BLK, XS_BLK = 4, 5, 6
    TAIL_BLK = 7 * W // tail_w
    IW_OFF = 3 * dh
    n_z = 7 * W + tail_w

    h = jnp.concatenate([x_prompt.reshape(Mp, D), x_sample.reshape(Ms, D)], axis=0)
    caches = (cache_attn_k, cache_attn_v, cache_idx_k)
    mem_rows = mem_prompt.reshape(B * n_mem, D)
    outs = [[] for _ in range(16)]

    def ffn(h, l, i, g):
        n = rmsnorm_rows(h, g)
        act = matmul(n, [(ffn_in, (l, i), 0), (ffn_in, (l, i), d_ff // 512)], n_out=d_ff, tn=512, tm=TM,
                     mode="swiglu", out_dtype=BF16, name="ffn_in")
        return matmul(act, [(ffn_out, (l, i), 0)], n_out=D, tn=512, tm=TM, mode="residual", aux=h, scale=0.5,
                      name="ffn_out")

    for l in range(depth):
        h = ffn(h, l, 0, norm_g[l, 0])

        u = rmsnorm_rows(h, norm_g[l, 1])
        wl = w_in[l]
        w_in2 = jnp.concatenate([wl[:, 0:o_k], wl[:, o_iq:o_iq + W], wl[:, o_xs:o_xs + W], wl[:, o_k:o_k + 2 * dh],
                                 wl[:, o_ik:o_ik + dh + IDX_HEADS],
                                 jnp.zeros((D, tail_w - 3 * dh - IDX_HEADS), F32)], axis=1)
        z = matmul(u, [(w_in2, (), 0)], n_out=n_z, tn=768, tm=TM, name="in_proj")
        gates = matmul(u, [(w_gate, (l,), 0)], n_out=4 * D, tn=512, tm=TM, mode="bias_sigmoid",
                       aux=b_gate[l].reshape(1, 4 * D), out_dtype=BF16, name="gates")

        wmix = pool_mix[l].astype(BF16)
        ya_p, yb_p, npool_p, nconv_p = pool_conv(z, 0, B, T, 1, 512, 0, None, None, wmix, pool_scale[l], conv_w[l], W)
        ya_s, yb_s, npool_s, nconv_s = pool_conv(z, Mp, Bs, Ts, 16, Ts, past_len, state_pool[l], state_conv[l],
                                                 wmix, pool_scale[l], conv_w[l], W)

        kn, vv, ik = kv_prep(z, TAIL_BLK, tail_w, k_norm_g[l], dh)
        gq_tiled = jnp.tile(q_norm_g[l], N_HEADS_C).reshape(1, W)
        yc_p = attn_prompt(z, Q_BLK, IQ_BLK, TAIL_BLK, tail_w, IW_OFF, kn[:Mp].reshape(B, T, dh),
                           vv[:Mp].reshape(B, T, dh), ik[:Mp].reshape(B, T, dh), gq_tiled, B, T, W, dh)
        yc_s = attn_sample(z, Mp, Q_BLK, IQ_BLK, TAIL_BLK, tail_w, IW_OFF, kn, vv, ik, caches, page_table, l,
                           gq_tiled, Bs, Ts, W, dh)

        ab_re, ab_im, co_re, co_im = ssm_params(ssm_a_re[l], ssm_a_im[l], ssm_log_step[l])
        flat = lambda a: a.reshape(1, SG * SN)
        lev_re, lev_im, pw_re, pw_im = ssm_tables(flat(ab_re), flat(ab_im), SSM_CHUNK)
        rep = lambda a: _lane_blocks(jnp.broadcast_to(flat(a), (SUBLANES, SG * SN)))
        consts = (
            _lane_blocks(_block_diag(ssm_b_re[l].transpose(0, 2, 1))).astype(BF16),
            _lane_blocks(_block_diag(ssm_b_im[l].transpose(0, 2, 1))).astype(BF16),
            _block_diag(ssm_c_re[l].transpose(0, 2, 1)).reshape(NB, LANES, W).astype(BF16),
            _block_diag(ssm_c_im[l].transpose(0, 2, 1)).reshape(NB, LANES, W).astype(BF16),
            rep(co_re), rep(co_im), _lane_blocks(lev_re), _lane_blocks(lev_im),
            _lane_blocks(pw_re), _lane_blocks(pw_im),
            ssm_d[l].reshape(1, W), ssm_glu_w[l].astype(BF16), ssm_glu_b[l].reshape(1, W))
        yd_p, sre_p, sim_p = ssm_mixer(z, XS_BLK, 0, B, T, consts, None)
        h0 = (state_ssm_re[l].reshape(Bs, NB, LANES), state_ssm_im[l].reshape(Bs, NB, LANES))
        yd_s, sre_s, sim_s = ssm_mixer(z, XS_BLK, Mp, Bs, Ts, consts, h0)

        cat = lambda a, b: jnp.concatenate([a, b], axis=0)
        merged = gated_merge([cat(ya_p, ya_s), cat(yb_p, yb_s), cat(yc_p, yc_s), cat(yd_p, yd_s)], gates, w_branch, l)
        h = matmul(merged, [(w_o, (l,), 0)], n_out=D, tn=512, tm=TM, mode="residual", aux=h, scale=1.0, name="w_o")

        mn = rmsnorm_rows(mem_rows, mem_norm_g[l])
        mk = matmul(mn, [(w_xk, (l,), 0)], n_out=Wx, tn=Wx, tm=B * n_mem, mode="headnorm",
                    aux=jnp.tile(xk_norm_g[l], X_HEADS).reshape(1, Wx), group=Wx // X_HEADS, name="mem_k")
        mv = matmul(mn, [(w_xv, (l,), 0)], n_out=Wx, tn=Wx, tm=B * n_mem, name="mem_v")
        n2 = rmsnorm_rows(h, norm_g[l, 2])
        qx = matmul(n2, [(w_xq, (l,), 0)], n_out=Wx, tn=Wx, tm=TM, name="w_xq")
        xa_p = cross_attention(qx, 0, B, T, 1, 512, mk.reshape(1, B, n_mem, Wx), mv.reshape(1, B, n_mem, Wx), 0,
                               xq_norm_g[l])
        xa_s = cross_attention(qx, Mp, Bs, Ts, 8, Ts, cache_mem_k.reshape(depth, Bs, n_mem * X_HEADS, Wx // X_HEADS),
                               cache_mem_v.reshape(depth, Bs, n_mem * X_HEADS, Wx // X_HEADS), l, xq_norm_g[l])
        h = matmul(cat(xa_p, xa_s), [(w_xo, (l,), 0)], n_out=D, tn=512, tm=TM, mode="residual", aux=h, scale=1.0,
                   name="w_xo")

        h = ffn(h, l, 1, norm_g[l, 3])

        xh = Wx // X_HEADS
        layer_out = (kn[:Mp].reshape(B, T, dh), vv[:Mp].reshape(B, T, dh), ik[:Mp].reshape(B, T, dh),
                     mk.reshape(B, n_mem, X_HEADS, xh), mv.reshape(B, n_mem, X_HEADS, xh), npool_p, nconv_p,
                     sre_p.reshape(B, SG, SN), sim_p.reshape(B, SG, SN),
                     kn[Mp:].reshape(Bs, Ts, dh), vv[Mp:].reshape(Bs, Ts, dh), ik[Mp:].reshape(Bs, Ts, dh),
                     npool_s, nconv_s, sre_s.reshape(Bs, SG, SN), sim_s.reshape(Bs, SG, SN))
        for acc, val in zip(outs, layer_out):
            acc.append(val)

    return (h[:Mp].reshape(B, T, D), h[Mp:].reshape(Bs, Ts, D)) + tuple(jnp.stack(o) for o in outs)
```

```python
import functools
import math

import jax
import jax.numpy as jnp
from jax import lax
from jax.experimental import pallas as pl
from jax.experimental.pallas import tpu as pltpu

F32, BF16, I32 = jnp.float32, jnp.bfloat16, jnp.int32
EPS = 1e-6
NEG_INF = float("-inf")

V7X_VMEM_BYTES = 64 * 1024 * 1024
VMEM_LIMIT = V7X_VMEM_BYTES - 8 * 1024 * 1024
LANES = 128
SUBLANES = 8

POOL_WINDOWS = (2, 4, 8, 16)
POOL_HIST = 16
CONV_WIDTH = 3
CONV_HIST = 8
N_HEADS_C = 8
IDX_HEADS = 8
TOPK_MAX = 256
X_HEADS = 4
SSM_CHUNK = 256
SSM_SEQ_ROWS = 128


def _cparams(sem, vmem=VMEM_LIMIT):
    return pltpu.CompilerParams(dimension_semantics=sem, vmem_limit_bytes=vmem)


def _dot(a, b):
    return jnp.dot(a, b, preferred_element_type=F32)


def _dot_nt(a, b):
    return lax.dot_general(a, b, (((1,), (1,)), ((), ())), preferred_element_type=F32)


def _rmsnorm_kernel(x_ref, g_ref, o_ref):
    x = x_ref[...]
    ms = jnp.mean(x * x, axis=-1, keepdims=True)
    o_ref[...] = (x * lax.rsqrt(ms + EPS) * g_ref[...]).astype(o_ref.dtype)


def rmsnorm_rows(x, g, tm=512):
    M, D = x.shape
    return pl.pallas_call(
        _rmsnorm_kernel,
        grid=(M // tm,),
        in_specs=[pl.BlockSpec((tm, D), lambda i: (i, 0)), pl.BlockSpec((1, D), lambda i: (0, 0))],
        out_specs=pl.BlockSpec((tm, D), lambda i: (i, 0)),
        out_shape=jax.ShapeDtypeStruct((M, D), BF16),
        compiler_params=_cparams(("arbitrary",)),
        name="rmsnorm",
    )(x, g.reshape(1, D))


def _mm_kernel(*refs, nw, mode, scale, sub, group):
    x_ref = refs[0]
    w_refs = refs[1:1 + nw]
    p = 1 + nw
    aux_ref = None
    if mode in ("bias_sigmoid", "residual", "headnorm"):
        aux_ref = refs[p]
        p += 1
    o_ref = refs[p]
    scr = refs[p + 1:p + 1 + nw]

    @pl.when(pl.program_id(1) == 0)
    def _():
        for w_ref, s in zip(w_refs, scr):
            s[...] = w_ref[...].astype(BF16)

    tm = x_ref.shape[0]

    def body(r, carry):
        rows = pl.ds(pl.multiple_of(r * sub, sub), sub)
        x = x_ref[rows, :]
        acc = [_dot(x, s[...]) for s in scr]
        if mode == "swiglu":
            a, b = acc
            y = (a * jax.nn.sigmoid(a)) * b
        elif mode == "bias_sigmoid":
            y = jax.nn.sigmoid(acc[0] + aux_ref[...])
        elif mode == "residual":
            y = aux_ref[rows, :] + scale * acc[0]
        elif mode == "headnorm":
            a = acc[0]
            parts = []
            for h in range(a.shape[1] // group):
                ah = a[:, h * group:(h + 1) * group]
                ms = jnp.mean(ah * ah, axis=-1, keepdims=True)
                parts.append(ah * lax.rsqrt(ms + EPS))
            y = jnp.concatenate(parts, axis=1) * aux_ref[...]
        else:
            y = acc[0]
        o_ref[rows, :] = y.astype(o_ref.dtype)
        return carry

    lax.fori_loop(0, tm // sub, body, 0, unroll=True)


def matmul(x, weights, *, n_out, tn, tm, mode="plain", aux=None, scale=1.0, out_dtype=F32, group=LANES,
           name="matmul"):
    M, K = x.shape
    nw = len(weights)
    sub = min(tm, 256)
    in_specs = [pl.BlockSpec((tm, K), lambda j, i: (i, 0))]
    args = [x]
    for arr, lead, coff in weights:
        nl = len(lead)
        in_specs.append(pl.BlockSpec((None,) * nl + (K, tn),
                                     functools.partial(lambda j, i, lead, coff: (*lead, 0, coff + j), lead=lead, coff=coff)))
        args.append(arr)
    if mode in ("bias_sigmoid", "headnorm"):
        in_specs.append(pl.BlockSpec((1, tn), lambda j, i: (0, j)))
        args.append(aux)
    elif mode == "residual":
        in_specs.append(pl.BlockSpec((tm, tn), lambda j, i: (i, j)))
        args.append(aux)
    return pl.pallas_call(
        functools.partial(_mm_kernel, nw=nw, mode=mode, scale=scale, sub=sub, group=group),
        grid=(n_out // tn, M // tm),
        in_specs=in_specs,
        out_specs=pl.BlockSpec((tm, tn), lambda j, i: (i, j)),
        out_shape=jax.ShapeDtypeStruct((M, n_out), out_dtype),
        scratch_shapes=[pltpu.VMEM((K, tn), BF16) for _ in range(nw)],
        compiler_params=_cparams(("arbitrary", "arbitrary")),
        name=name,
    )(*args)


def _merge_kernel(*refs, nb):
    br = refs[0:nb]
    gt = refs[nb:2 * nb]
    ws = refs[2 * nb:3 * nb]
    o_ref = refs[3 * nb]
    scr = refs[3 * nb + 1:3 * nb + 1 + nb]

    @pl.when(pl.program_id(1) == 0)
    def _():
        for w_ref, s in zip(ws, scr):
            s[...] = w_ref[...].astype(BF16)

    tm = o_ref.shape[0]
    sub = min(tm, 256)

    def body(r, carry):
        rows = pl.ds(pl.multiple_of(r * sub, sub), sub)
        acc = None
        for k in range(nb):
            t = gt[k][rows, :].astype(F32) * _dot(br[k][rows, :], scr[k][...])
            acc = t if acc is None else acc + t
        o_ref[rows, :] = acc.astype(o_ref.dtype)
        return carry

    lax.fori_loop(0, tm // sub, body, 0, unroll=True)


def gated_merge(branches, gates, w_branch, layer, *, tn=512, tm=1024):
    nb = len(branches)
    M, W = branches[0].shape
    D = w_branch.shape[-1]
    nj = D // tn
    in_specs = [pl.BlockSpec((tm, W), lambda j, i: (i, 0)) for _ in range(nb)]
    in_specs += [pl.BlockSpec((tm, tn), functools.partial(lambda j, i, k: (i, k * nj + j), k=k)) for k in range(nb)]
    in_specs += [pl.BlockSpec((None, None, W, tn), functools.partial(lambda j, i, k: (layer, k, 0, j), k=k))
                 for k in range(nb)]
    return pl.pallas_call(
        functools.partial(_merge_kernel, nb=nb),
        grid=(nj, M // tm),
        in_specs=in_specs,
        out_specs=pl.BlockSpec((tm, tn), lambda j, i: (i, j)),
        out_shape=jax.ShapeDtypeStruct((M, D), BF16),
        scratch_shapes=[pltpu.VMEM((W, tn), BF16) for _ in range(nb)],
        compiler_params=_cparams(("arbitrary", "arbitrary")),
        name="gated_merge",
    )(*branches, *([gates] * nb), *([w_branch] * nb))


def _kvprep_kernel(t_ref, g_ref, k_ref, v_ref, ik_ref, *, dh):
    t = t_ref[...]
    k = t[:, 0:dh]
    ms = jnp.mean(k * k, axis=-1, keepdims=True)
    k_ref[...] = k * lax.rsqrt(ms + EPS) * g_ref[...]
    v_ref[...] = t[:, dh:2 * dh]
    ik_ref[...] = t[:, 2 * dh:3 * dh]


def kv_prep(z, tail_block, tail_w, gk, dh, tm=1024):
    M = z.shape[0]
    out = jax.ShapeDtypeStruct((M, dh), F32)
    return pl.pallas_call(
        functools.partial(_kvprep_kernel, dh=dh),
        grid=(M // tm,),
        in_specs=[pl.BlockSpec((tm, tail_w), lambda i: (i, tail_block)), pl.BlockSpec((1, dh), lambda i: (0, 0))],
        out_specs=[pl.BlockSpec((tm, dh), lambda i: (i, 0))] * 3,
        out_shape=[out, out, out],
        compiler_params=_cparams(("arbitrary",)),
        name="kv_prep",
    )(z, gk.reshape(1, dh))


def _poolconv_kernel(*refs, G, T, W, pos0, has_state):
    xp_ref, xc_ref, bg_ref, cg_ref = refs[0:4]
    p = 4
    if has_state:
        pbuf_ref, cbuf_ref = refs[4:6]
        p = 6
    wmix_ref, pscale_ref, convw_ref = refs[p:p + 3]
    ya_ref, yb_ref, npool_ref, nconv_ref = refs[p + 3:p + 7]
    fullp, fullc = refs[p + 7:p + 9]
    PH, CH = POOL_HIST, CONV_HIST
    nh = CONV_WIDTH - 1
    gw = W // len(POOL_WINDOWS)
    c = pl.program_id(1)

    @pl.when(c == 0)
    def _():
        if has_state:
            fullp[:, 1:PH, :] = pbuf_ref[...]
            fullc[:, CH - nh:CH, :] = cbuf_ref[...]
        else:
            fullp[:, 0:PH, :] = jnp.zeros((G, PH, W), F32)
            fullc[:, 0:CH, :] = jnp.zeros((G, CH, W), F32)

    @pl.when(c > 0)
    def _():
        fullp[:, 0:PH, :] = fullp[:, T:T + PH, :]
        fullc[:, 0:CH, :] = fullc[:, T:T + CH, :]

    fullp[:, PH:PH + T, :] = xp_ref[...].reshape(G, T, W)
    t_idx = lax.broadcasted_iota(I32, (1, T, 1), 1) + (c * T + (pos0 + 1))
    for gi, w in enumerate(POOL_WINDOWS):
        cols = slice(gi * gw, (gi + 1) * gw)
        acc = fullp[:, PH:PH + T, cols]
        for j in range(1, w):
            acc = acc + fullp[:, PH - j:PH - j + T, cols]
        cnt = jnp.minimum(t_idx, w).astype(F32)
        d = acc / cnt - fullp[:, PH:PH + T, cols]
        y = _dot(d.reshape(G * T, gw).astype(BF16), wmix_ref[gi]) * pscale_ref[:, cols]
        ya_ref[:, cols] = y.astype(ya_ref.dtype)
    npool_ref[...] = fullp[:, T + 1:T + PH, :]

    fullc[:, CH:CH + T, :] = (cg_ref[...] * xc_ref[...]).reshape(G, T, W)
    y = None
    for j in range(CONV_WIDTH):
        wj = convw_ref[j:j + 1, :].reshape(1, 1, W)
        term = wj * fullc[:, CH - nh + j:CH - nh + j + T, :]
        y = term if y is None else y + term
    yb_ref[...] = (bg_ref[...] * y.reshape(G * T, W)).astype(yb_ref.dtype)
    nconv_ref[...] = fullc[:, CH + T - nh:CH + T, :]


def pool_conv(z, row0, Bt, T, G, TC, pos0, pool_buf, conv_buf, wmix_bf16, pool_scale, conv_w, W):
    has_state = pool_buf is not None
    assert G == 1 or TC == T
    R = G * TC
    nc = T // TC
    rb0 = row0 // R
    nh = CONV_WIDTH - 1

    def zspec(cb):
        return pl.BlockSpec((R, W), functools.partial(lambda i, c, cb: (rb0 + i * nc + c, cb), cb=cb))

    in_specs = [zspec(0), zspec(1), zspec(2), zspec(3)]
    args = [z, z, z, z]
    if has_state:
        in_specs += [pl.BlockSpec((G, POOL_HIST - 1, W), lambda i, c: (i, 0, 0)),
                     pl.BlockSpec((G, nh, W), lambda i, c: (i, 0, 0))]
        args += [pool_buf, conv_buf]
    nwin = len(POOL_WINDOWS)
    in_specs += [pl.BlockSpec((nwin, W // nwin, W // nwin), lambda i, c: (0, 0, 0)),
                 pl.BlockSpec((1, W), lambda i, c: (0, 0)),
                 pl.BlockSpec((CONV_WIDTH, W), lambda i, c: (0, 0))]
    args += [wmix_bf16, pool_scale.reshape(1, W), conv_w]
    return pl.pallas_call(
        functools.partial(_poolconv_kernel, G=G, T=TC, W=W, pos0=pos0, has_state=has_state),
        grid=(Bt // G, nc),
        in_specs=in_specs,
        out_specs=[pl.BlockSpec((R, W), lambda i, c: (i * nc + c, 0)), pl.BlockSpec((R, W), lambda i, c: (i * nc + c, 0)),
                   pl.BlockSpec((G, POOL_HIST - 1, W), lambda i, c: (i, 0, 0)),
                   pl.BlockSpec((G, nh, W), lambda i, c: (i, 0, 0))],
        out_shape=[jax.ShapeDtypeStruct((Bt * T, W), BF16), jax.ShapeDtypeStruct((Bt * T, W), BF16),
                   jax.ShapeDtypeStruct((Bt, POOL_HIST - 1, W), F32), jax.ShapeDtypeStruct((Bt, nh, W), F32)],
        scratch_shapes=[pltpu.VMEM((G, POOL_HIST + TC, W), F32), pltpu.VMEM((G, CONV_HIST + TC, W), F32)],
        compiler_params=_cparams(("arbitrary", "arbitrary")),
        name="pool_conv",
    )(*args)


def _ssm_params_kernel(are_ref, aim_ref, ls_ref, abre_ref, abim_ref, core_ref, coim_ref):
    a_re = are_ref[...]
    a_im = aim_ref[...]
    step = jnp.exp(ls_ref[...])
    decay = jnp.exp(step * a_re)
    ab_re = decay * jnp.cos(step * a_im)
    ab_im = decay * jnp.sin(step * a_im)
    den = a_re * a_re + a_im * a_im
    nr = ab_re - 1.0
    abre_ref[...] = ab_re
    abim_ref[...] = ab_im
    core_ref[...] = (nr * a_re + ab_im * a_im) / den
    coim_ref[...] = (ab_im * a_re - nr * a_im) / den


def ssm_params(a_re, a_im, log_step):
    G, N = a_re.shape
    out = jax.ShapeDtypeStruct((G, N), F32)
    return pl.pallas_call(_ssm_params_kernel, out_shape=[out, out, out, out], name="ssm_params")(
        a_re, a_im, log_step.reshape(G, 1))


def _scan_levels(C):
    return [1 << k for k in range(int(math.log2(C)))]


def _ssm_tables_kernel(abre_ref, abim_ref, lre_ref, lim_ref, pre_ref, pim_ref, *, C):
    ar = abre_ref[...]
    ai = abim_ref[...]
    N = ar.shape[1]
    row = lax.broadcasted_iota(I32, (C, N), 0)
    hr = jnp.where(row == 0, ar, 0.0)
    hi = jnp.where(row == 0, ai, 0.0)
    lre_ref[...] = jnp.zeros(lre_ref.shape, F32)
    lim_ref[...] = jnp.zeros(lim_ref.shape, F32)
    for k, s in enumerate(_scan_levels(C)):
        lre_ref[k:k + 1, :] = ar
        lim_ref[k:k + 1, :] = ai
        sr = jnp.where(row >= s, pltpu.roll(hr, s, 0), 0.0)
        si = jnp.where(row >= s, pltpu.roll(hi, s, 0), 0.0)
        hr, hi = hr + ar * sr - ai * si, hi + ar * si + ai * sr
        ar, ai = ar * ar - ai * ai, 2.0 * ar * ai
    pre_ref[...] = hr
    pim_ref[...] = hi


def ssm_tables(ab_re, ab_im, C):
    N = ab_re.shape[1]
    nlev = len(_scan_levels(C))
    lev = jax.ShapeDtypeStruct((SUBLANES * ((nlev + SUBLANES - 1) // SUBLANES), N), F32)
    pw = jax.ShapeDtypeStruct((C, N), F32)
    return pl.pallas_call(functools.partial(_ssm_tables_kernel, C=C), out_shape=[lev, lev, pw, pw],
                          name="ssm_tables")(ab_re, ab_im)


def _gelu_tanh(x):
    return 0.5 * x * (1.0 + jnp.tanh(math.sqrt(2.0 / math.pi) * (x + 0.044715 * (x * x * x))))


def _ssm_kernel(*refs, R, T, NB, chained):
    (xs_ref, bre_ref, bim_ref, cre_ref, cim_ref, core_ref, coim_ref, lre_ref, lim_ref) = refs[0:9]
    p = 9
    if chained:
        pre_ref, pim_ref = refs[p:p + 2]
        p += 2
    else:
        h0r_ref, h0i_ref = refs[p:p + 2]
        p += 2
    d_ref, gw_ref, gb_ref = refs[p:p + 3]
    yd_ref, sre_ref, sim_ref = refs[p + 3:p + 6]
    p += 6
    if chained:
        car_ref, cai_ref = refs[p:p + 2]
    G = R // T
    xs = xs_ref[...]
    xb = xs.astype(BF16)
    tpos = lax.broadcasted_iota(I32, (R, LANES), 0) % SUBLANES
    levels = _scan_levels(SUBLANES)

    if chained:
        c = pl.program_id(1)

        @pl.when(c == 0)
        def _():
            car_ref[...] = jnp.zeros(car_ref.shape, F32)
            cai_ref[...] = jnp.zeros(cai_ref.shape, F32)

    y = d_ref[...] * xs
    for cb in range(NB):
        pr = _dot(xb, bre_ref[cb])
        pi = _dot(xb, bim_ref[cb])
        cor = core_ref[cb][0:1, :]
        coi = coim_ref[cb][0:1, :]
        hr = cor * pr - coi * pi
        hi = cor * pi + coi * pr
        if not chained:
            ar = lre_ref[cb][0:1, :]
            ai = lim_ref[cb][0:1, :]
            h0r = jnp.broadcast_to(h0r_ref[:, cb:cb + 1, :], (G, T, LANES)).reshape(R, LANES)
            h0i = jnp.broadcast_to(h0i_ref[:, cb:cb + 1, :], (G, T, LANES)).reshape(R, LANES)
            first = tpos == 0
            hr = hr + jnp.where(first, ar * h0r - ai * h0i, 0.0)
            hi = hi + jnp.where(first, ar * h0i + ai * h0r, 0.0)
        for k, s in enumerate(levels):
            ar = lre_ref[cb][k:k + 1, :]
            ai = lim_ref[cb][k:k + 1, :]
            sr = jnp.where(tpos >= s, pltpu.roll(hr, s, 0), 0.0)
            si = jnp.where(tpos >= s, pltpu.roll(hi, s, 0), 0.0)
            hr, hi = hr + ar * sr - ai * si, hi + ar * si + ai * sr
        if chained:
            cr = car_ref[cb:cb + 1, :]
            ci = cai_ref[cb:cb + 1, :]
            pwr = pre_ref[cb]
            pwi = pim_ref[cb]
            grs, gis = [], []
            for v in range(R // SUBLANES):
                rows = slice(v * SUBLANES, (v + 1) * SUBLANES)
                cbr = jnp.broadcast_to(cr, (SUBLANES, LANES))
                cbi = jnp.broadcast_to(ci, (SUBLANES, LANES))
                gr = hr[rows] + pwr * cbr - pwi * cbi
                gi = hi[rows] + pwr * cbi + pwi * cbr
                cr = gr[SUBLANES - 1:SUBLANES, :]
                ci = gi[SUBLANES - 1:SUBLANES, :]
                grs.append(gr)
                gis.append(gi)
            hr = jnp.concatenate(grs, axis=0)
            hi = jnp.concatenate(gis, axis=0)
            car_ref[cb:cb + 1, :] = cr
            cai_ref[cb:cb + 1, :] = ci
        else:
            sre_ref[:, cb:cb + 1, :] = hr.reshape(G, T, LANES)[:, T - 1:T, :]
            sim_ref[:, cb:cb + 1, :] = hi.reshape(G, T, LANES)[:, T - 1:T, :]
        y = y + _dot(hr.astype(BF16), cre_ref[cb]) - _dot(hi.astype(BF16), cim_ref[cb])

    z = _gelu_tanh(y)
    out = z * jax.nn.sigmoid(_dot(z.astype(BF16), gw_ref[...]) + gb_ref[...])
    yd_ref[...] = out.astype(yd_ref.dtype)

    if chained:
        @pl.when(c == pl.num_programs(1) - 1)
        def _():
            sre_ref[0] = car_ref[...]
            sim_ref[0] = cai_ref[...]


def ssm_mixer(z, xs_block, row0, Bt, T, consts, h0):
    (bre3, bim3, cre3, cim3, core3, coim3, lre3, lim3, pre3, pim3, dvec, gw, gb) = consts
    NB = bre3.shape[0]
    W = dvec.shape[1]
    chained = h0 is None
    assert chained or T == SUBLANES
    R = SSM_CHUNK if chained else SSM_SEQ_ROWS
    rb0 = row0 // R

    def full(a):
        nd = a.ndim
        return pl.BlockSpec(a.shape, lambda *_: (0,) * nd)

    if chained:
        nchunk = T // R
        grid = (Bt, nchunk)
        xs_spec = pl.BlockSpec((R, W), lambda b, c: (rb0 + b * nchunk + c, xs_block))
        st_args, st_specs = [pre3, pim3], [full(pre3), full(pim3)]
        yd_spec = pl.BlockSpec((R, W), lambda b, c: (b * nchunk + c, 0))
        s_spec = pl.BlockSpec((1, NB, LANES), lambda b, c: (b, 0, 0))
        scratch = [pltpu.VMEM((NB, LANES), F32), pltpu.VMEM((NB, LANES), F32)]
        sem = ("arbitrary", "arbitrary")
        Tk = R
    else:
        G = R // T
        grid = (Bt // G,)
        xs_spec = pl.BlockSpec((R, W), lambda i: (rb0 + i, xs_block))
        st_args = list(h0)
        st_specs = [pl.BlockSpec((G, NB, LANES), lambda i: (i, 0, 0))] * 2
        yd_spec = pl.BlockSpec((R, W), lambda i: (i, 0))
        s_spec = pl.BlockSpec((G, NB, LANES), lambda i: (i, 0, 0))
        scratch = []
        sem = ("arbitrary",)
        Tk = T
    shared = [bre3, bim3, cre3, cim3, core3, coim3, lre3, lim3]
    tailc = [dvec, gw, gb]
    s_shape = jax.ShapeDtypeStruct((Bt, NB, LANES), F32)
    return pl.pallas_call(
        functools.partial(_ssm_kernel, R=R, T=Tk, NB=NB, chained=chained),
        grid=grid,
        in_specs=[xs_spec] + [full(a) for a in shared] + st_specs + [full(a) for a in tailc],
        out_specs=[yd_spec, s_spec, s_spec],
        out_shape=[jax.ShapeDtypeStruct((Bt * T, W), BF16), s_shape, s_shape],
        scratch_shapes=scratch,
        compiler_params=_cparams(sem),
        name="ssm_chained" if chained else "ssm_stateful",
    )(z, *shared, *st_args, *tailc)


def _sort_keys(score):
    b = pltpu.bitcast(score, I32)
    key = jnp.where(b < 0, b ^ jnp.int32(0x7FFFFFFF), b)
    return jnp.where(key == -1, 0, key)


def _kth_largest_key(key, k):
    rows = key.shape[0]

    def body(i, t):
        cand = t + (jnp.int32(1) << (31 - i))
        cnt = jnp.sum(jnp.where(key >= cand, 1.0, 0.0), axis=-1, keepdims=True)
        return jnp.where(cnt >= float(k), cand, t)

    return lax.fori_loop(0, 32, body, jnp.full((rows, 1), -2 ** 31, I32))


def _blocked_prefix(eq, tri, offset):
    outs = []
    run = offset
    for j in range(eq.shape[1] // LANES):
        blk = eq[:, j * LANES:(j + 1) * LANES].astype(BF16)
        pj = _dot(blk, tri) + run
        outs.append(pj)
        run = pj[:, LANES - 1:LANES]
    return outs, run


def _attn_prompt_kernel(q_ref, iq_ref, tail_ref, k_ref, v_ref, ik_ref, gq_ref, e_ref, tri_ref, o_ref,
                        kb, vb, ikb, bias, *, TQ, L, dh, iw_off, n_sel, n_buckets):
    qb = pl.program_id(1)

    @pl.when(qb == 0)
    def _():
        kb[...] = k_ref[0].astype(BF16)
        vb[...] = v_ref[0].astype(BF16)
        ikb[...] = ik_ref[0].astype(BF16)

    scale = dh ** -0.5
    q = q_ref[...]
    q2 = q * q
    hi = q2.astype(BF16)
    lo = (q2 - hi.astype(F32)).astype(BF16)
    ss = _dot(hi, e_ref[...]) + _dot(lo, e_ref[...])
    qn = (q * lax.rsqrt(ss * (1.0 / dh) + EPS) * gq_ref[...] * scale).astype(BF16)
    iqs = (iq_ref[...] * scale).astype(BF16)
    iw = tail_ref[:, iw_off:iw_off + IDX_HEADS] * (IDX_HEADS ** -0.5)

    def attend(Lk):
        score = None
        for h in range(IDX_HEADS):
            lg = _dot_nt(iqs[:, h * dh:(h + 1) * dh], ikb[0:Lk, :])
            t = jnp.maximum(lg, 0.0) * iw[:, h:h + 1]
            score = t if score is None else score + t
        col = lax.broadcasted_iota(I32, (TQ, Lk), 1)
        qpos = qb * TQ + lax.broadcasted_iota(I32, (TQ, Lk), 0)
        causal = col <= qpos
        score = jnp.where(causal, score, NEG_INF)

        key = _sort_keys(score)
        thr = _kth_largest_key(key, n_sel)
        gt = key > thr
        eq = key == thr
        need = float(n_sel) - jnp.sum(jnp.where(gt, 1.0, 0.0), axis=-1, keepdims=True)
        pref, _ = _blocked_prefix(jnp.where(eq, 1.0, 0.0), tri_ref[...], jnp.zeros((TQ, 1), F32))
        pref = jnp.concatenate(pref, axis=1)
        sel = (gt | (eq & (pref <= need))) & causal
        bias[:, 0:Lk] = jnp.where(sel, 0.0, NEG_INF)

        for h in range(N_HEADS_C):
            lg = _dot_nt(qn[:, h * dh:(h + 1) * dh], kb[0:Lk, :]) + bias[:, 0:Lk]
            m = jnp.max(lg, axis=-1, keepdims=True)
            e = jnp.exp(lg - m)
            den = jnp.sum(e, axis=-1, keepdims=True)
            o = _dot(e.astype(BF16), vb[0:Lk, :]) / den
            o_ref[:, h * dh:(h + 1) * dh] = o.astype(o_ref.dtype)

    per = (L // TQ) // n_buckets
    for bk in range(n_buckets):
        pl.when(qb // per == bk)(functools.partial(attend, (bk + 1) * per * TQ))


def attn_prompt(z, q_block, iq_block, tail_block, tail_w, iw_off, kn, v, ik, gq_tiled, B, T, W, dh, TQ=256,
                n_buckets=8):
    nq = T // TQ
    n_sel = min(TOPK_MAX, T // 4)
    eye = (jnp.arange(W)[:, None] // dh == jnp.arange(W)[None, :] // dh).astype(BF16)
    tri = (jnp.arange(LANES)[:, None] <= jnp.arange(LANES)[None, :]).astype(BF16)
    kspec = pl.BlockSpec((1, T, dh), lambda b, i: (b, 0, 0))
    return pl.pallas_call(
        functools.partial(_attn_prompt_kernel, TQ=TQ, L=T, dh=dh, iw_off=iw_off, n_sel=n_sel, n_buckets=n_buckets),
        grid=(B, nq),
        in_specs=[pl.BlockSpec((TQ, W), lambda b, i: (b * nq + i, q_block)),
                  pl.BlockSpec((TQ, W), lambda b, i: (b * nq + i, iq_block)),
                  pl.BlockSpec((TQ, tail_w), lambda b, i: (b * nq + i, tail_block)),
                  kspec, kspec, kspec,
                  pl.BlockSpec((1, W), lambda b, i: (0, 0)),
                  pl.BlockSpec((W, W), lambda b, i: (0, 0)),
                  pl.BlockSpec((LANES, LANES), lambda b, i: (0, 0))],
        out_specs=pl.BlockSpec((TQ, W), lambda b, i: (b * nq + i, 0)),
        out_shape=jax.ShapeDtypeStruct((B * T, W), BF16),
        scratch_shapes=[pltpu.VMEM((T, dh), BF16), pltpu.VMEM((T, dh), BF16), pltpu.VMEM((T, dh), BF16),
                        pltpu.VMEM((TQ, T), F32)],
        compiler_params=_cparams(("arbitrary", "arbitrary")),
        name="attn_prompt",
    )(z, z, z, kn, v, ik, gq_tiled, eye, tri)


def _xattn_kernel(q_ref, mk_ref, mv_ref, g_ref, o_ref, *, G, T, dh, scale):
    g = g_ref[...]
    for s in range(G):
        rows = slice(s * T, (s + 1) * T)
        for h in range(X_HEADS):
            cols = slice(h * dh, (h + 1) * dh)
            qh = q_ref[rows, cols]
            ms = jnp.mean(qh * qh, axis=-1, keepdims=True)
            qn = (qh * lax.rsqrt(ms + EPS) * g).astype(BF16)
            lg = _dot_nt(qn, mk_ref[s, :, cols].astype(BF16)) * scale
            m = jnp.max(lg, axis=-1, keepdims=True)
            e = jnp.exp(lg - m)
            den = jnp.sum(e, axis=-1, keepdims=True)
            o = _dot(e.astype(BF16), mv_ref[s, :, cols].astype(BF16)) / den
            o_ref[rows, cols] = o.astype(o_ref.dtype)


def _xattn_rows_kernel(q_ref, mk_ref, mv_ref, g_ref, o_ref, *, G, T, dh, scale):
    g = g_ref[...]
    H = X_HEADS
    n = mk_ref.shape[1]
    own = (lax.broadcasted_iota(I32, (H * T, n), 1) % H) == (lax.broadcasted_iota(I32, (H * T, n), 0) // T)
    for s in range(G):
        rows = slice(s * T, (s + 1) * T)
        parts = []
        for h in range(H):
            qh = q_ref[rows, h * dh:(h + 1) * dh]
            ms = jnp.mean(qh * qh, axis=-1, keepdims=True)
            parts.append(qh * lax.rsqrt(ms + EPS) * g)
        qs = jnp.concatenate(parts, axis=0).astype(BF16)
        lg = jnp.where(own, _dot_nt(qs, mk_ref[s].astype(BF16)) * scale, NEG_INF)
        m = jnp.max(lg, axis=-1, keepdims=True)
        e = jnp.exp(lg - m)
        den = jnp.sum(e, axis=-1, keepdims=True)
        o = _dot(e.astype(BF16), mv_ref[s].astype(BF16)) / den
        for h in range(H):
            o_ref[rows, h * dh:(h + 1) * dh] = o[h * T:(h + 1) * T, :].astype(o_ref.dtype)


def cross_attention(qx, row0, Bt, T, G, TQ, mem_k, mem_v, layer, gq):
    Wx = qx.shape[1]
    dh = Wx // X_HEADS
    n_rows, wm = mem_k.shape[2:]
    assert G == 1 or TQ == T
    R = G * TQ
    nt = T // TQ
    rb0 = row0 // R
    mspec = pl.BlockSpec((None, G, n_rows, wm), lambda i, t: (layer, i, 0, 0))
    body = _xattn_kernel if wm == Wx else _xattn_rows_kernel
    return pl.pallas_call(
        functools.partial(body, G=G, T=TQ, dh=dh, scale=dh ** -0.5),
        grid=(Bt // G, nt),
        in_specs=[pl.BlockSpec((R, Wx), lambda i, t: (rb0 + i * nt + t, 0)), mspec, mspec,
                  pl.BlockSpec((1, dh), lambda i, t: (0, 0))],
        out_specs=pl.BlockSpec((R, Wx), lambda i, t: (i * nt + t, 0)),
        out_shape=jax.ShapeDtypeStruct((Bt * T, Wx), BF16),
        compiler_params=_cparams(("arbitrary", "arbitrary")),
        name="cross_attention",
    )(qx, mem_k, mem_v, gq.reshape(1, dh))


def _attn_sample_kernel(pt_ref, q_ref, iq_ref, tail_ref, kn_ref, vn_ref, ikn_ref, ck_hbm, cv_hbm, cik_hbm,
                        gq_ref, e_ref, tri_ref, o_ref,
                        kp, vp, ikp, knp, vnp, iknp, qs_scr, iqs_scr, iw_scr, score, bias, sems,
                        *, G, T, dh, layer, n_pages, hp, iw_off, n_sel):
    step = pl.program_id(0)
    R = G * T
    H = N_HEADS_C
    LP = n_pages * hp
    LC = LP + LANES

    def page_copies(g, p):
        page = pt_ref[step * G + g, p]
        dst_cols = pl.ds(pl.multiple_of(p * hp, hp), hp)
        return [pltpu.make_async_copy(src.at[layer, page], dst.at[g, :, dst_cols], sems.at[s, g])
                for s, (src, dst) in enumerate(((ck_hbm, kp), (cv_hbm, vp), (cik_hbm, ikp)))]

    def start_all(i, c):
        g = i // n_pages
        for cp in page_copies(g, i - g * n_pages):
            cp.start()
        return c

    def wait_seq(g):
        def wait_page(p, c):
            for cp in page_copies(g, p):
                cp.wait()
            return c
        lax.fori_loop(0, n_pages, wait_page, 0)

    lax.fori_loop(0, G * n_pages, start_all, 0)

    @pl.when(step == 0)
    def _():
        knp[...] = jnp.zeros(knp.shape, F32)
        vnp[...] = jnp.zeros(vnp.shape, F32)
        iknp[...] = jnp.zeros(iknp.shape, F32)

    scale = dh ** -0.5
    q = q_ref[...]
    q2 = q * q
    hi = q2.astype(BF16)
    lo = (q2 - hi.astype(F32)).astype(BF16)
    ss = _dot(hi, e_ref[...]) + _dot(lo, e_ref[...])
    qs_scr[...] = q * lax.rsqrt(ss * (1.0 / dh) + EPS) * gq_ref[...] * scale
    iqs_scr[...] = iq_ref[...] * scale
    iw_blk = (iw_off // LANES) * LANES
    iw_lane = iw_off - iw_blk
    iw_scr[...] = tail_ref[:, iw_blk:iw_blk + LANES]

    new_ok = lax.broadcasted_iota(I32, (R, LANES), 1) <= lax.broadcasted_iota(I32, (R, LANES), 0) % T

    def stack_heads(ref, rows):
        return jnp.concatenate([ref[rows, h * dh:(h + 1) * dh] for h in range(H)], axis=0).astype(BF16)

    def score_body(g, c):
        wait_seq(g)
        rows = pl.ds(pl.multiple_of(g * T, T), T)
        iknp[0:T, :] = ikn_ref[rows, :]
        qs = stack_heads(iqs_scr, rows)
        lg_p = _dot(qs, ikp[g].astype(BF16))
        lg_n = _dot_nt(qs, iknp[...].astype(BF16))
        iw = iw_scr[rows, :] * (IDX_HEADS ** -0.5)
        sp = sn = None
        for h in range(IDX_HEADS):
            w = iw[:, iw_lane + h:iw_lane + h + 1]
            hs = slice(h * T, (h + 1) * T)
            tp = jnp.maximum(lg_p[hs], 0.0) * w
            tn = jnp.maximum(lg_n[hs], 0.0) * w
            sp, sn = (tp, tn) if sp is None else (sp + tp, sn + tn)
        score[rows, 0:LP] = sp
        score[rows, LP:LC] = sn
        return c

    lax.fori_loop(0, G, score_body, 0)
    score[:, LP:LC] = jnp.where(new_ok, score[:, LP:LC], NEG_INF)

    key = _sort_keys(score[...])
    thr = _kth_largest_key(key, n_sel)
    gt = key > thr
    eq = key == thr
    need = float(n_sel) - jnp.sum(jnp.where(gt, 1.0, 0.0), axis=-1, keepdims=True)
    pref, _ = _blocked_prefix(jnp.where(eq, 1.0, 0.0), tri_ref[...], jnp.zeros((R, 1), F32))
    pref = jnp.concatenate(pref, axis=1)
    bias[...] = jnp.where(gt | (eq & (pref <= need)), 0.0, NEG_INF)
    bias[:, LP:LC] = jnp.where(new_ok, bias[:, LP:LC], NEG_INF)

    def attn_body(g, c):
        rows = pl.ds(pl.multiple_of(g * T, T), T)
        knp[0:T, :] = kn_ref[rows, :]
        vnp[0:T, :] = vn_ref[rows, :]
        qs = stack_heads(qs_scr, rows)
        bh = jnp.concatenate([bias[rows, :]] * H, axis=0)
        lg_p = _dot(qs, kp[g].astype(BF16)) + bh[:, 0:LP]
        lg_n = _dot_nt(qs, knp[...].astype(BF16)) + bh[:, LP:LC]
        m = jnp.maximum(jnp.max(lg_p, axis=-1, keepdims=True), jnp.max(lg_n, axis=-1, keepdims=True))
        ep = jnp.exp(lg_p - m)
        en = jnp.exp(lg_n - m)
        den = jnp.sum(ep, axis=-1, keepdims=True) + jnp.sum(en, axis=-1, keepdims=True)
        o = (_dot_nt(ep.astype(BF16), vp[g].astype(BF16)) + _dot(en.astype(BF16), vnp[...].astype(BF16))) / den
        for h in range(H):
            o_ref[rows, h * dh:(h + 1) * dh] = o[h * T:(h + 1) * T, :].astype(o_ref.dtype)
        return c

    lax.fori_loop(0, G, attn_body, 0)


def attn_sample(z, row0, q_block, iq_block, tail_block, tail_w, iw_off, kn, v, ik, caches, page_table, layer,
                gq_tiled, Bs, T, W, dh, G=16):
    n_pages = page_table.shape[1]
    page = caches[0].shape[2]
    caches_t = [c.transpose(0, 1, 3, 2) for c in caches]
    R = G * T
    rb0 = row0 // R
    LP = n_pages * page
    LC = LP + LANES
    n_sel = min(TOPK_MAX, (LP + T) // 4)
    eye = (jnp.arange(W)[:, None] // dh == jnp.arange(W)[None, :] // dh).astype(BF16)
    tri = (jnp.arange(LANES)[:, None] <= jnp.arange(LANES)[None, :]).astype(BF16)
    any_spec = pl.BlockSpec(memory_space=pl.ANY)
    nspec = pl.BlockSpec((R, dh), lambda i, pt: (rb0 + i, 0))
    grid_spec = pltpu.PrefetchScalarGridSpec(
        num_scalar_prefetch=1,
        grid=(Bs // G,),
        in_specs=[pl.BlockSpec((R, W), lambda i, pt: (rb0 + i, q_block)),
                  pl.BlockSpec((R, W), lambda i, pt: (rb0 + i, iq_block)),
                  pl.BlockSpec((R, tail_w), lambda i, pt: (rb0 + i, tail_block)),
                  nspec, nspec, nspec, any_spec, any_spec, any_spec,
                  pl.BlockSpec((1, W), lambda i, pt: (0, 0)),
                  pl.BlockSpec((W, W), lambda i, pt: (0, 0)),
                  pl.BlockSpec((LANES, LANES), lambda i, pt: (0, 0))],
        out_specs=pl.BlockSpec((R, W), lambda i, pt: (i, 0)),
        scratch_shapes=[pltpu.VMEM((G, dh, LP), F32), pltpu.VMEM((G, dh, LP), F32), pltpu.VMEM((G, dh, LP), F32),
                        pltpu.VMEM((LANES, dh), F32), pltpu.VMEM((LANES, dh), F32), pltpu.VMEM((LANES, dh), F32),
                        pltpu.VMEM((R, W), F32), pltpu.VMEM((R, W), F32), pltpu.VMEM((R, LANES), F32),
                        pltpu.VMEM((R, LC), F32), pltpu.VMEM((R, LC), F32),
                        pltpu.SemaphoreType.DMA((3, G))],
    )
    return pl.pallas_call(
        functools.partial(_attn_sample_kernel, G=G, T=T, dh=dh, layer=layer, n_pages=n_pages, hp=page,
                          iw_off=iw_off, n_sel=n_sel),
        grid_spec=grid_spec,
        out_shape=jax.ShapeDtypeStruct((Bs * T, W), BF16),
        compiler_params=_cparams(("arbitrary",)),
        name="attn_sample",
    )(page_table, z, z, z, kn, v, ik, *caches_t, gq_tiled, eye, tri)


def _block_diag(blocks):
    G, a, b = blocks.shape
    eye = jnp.eye(G, dtype=blocks.dtype)
    return (eye[:, None, :, None] * blocks[:, :, None, :]).reshape(G * a, G * b)


def _lane_blocks(a):
    rows, n = a.shape
    return a.reshape(rows, n // LANES, LANES).transpose(1, 0, 2)


def kernel(x_prompt, x_sample, cache_attn_k, cache_attn_v, cache_idx_k, cache_mem_k, cache_mem_v, state_pool,
           state_conv, state_ssm_re, state_ssm_im, page_table, mem_prompt, norm_g, ffn_in, ffn_out, w_in,
           q_norm_g, k_norm_g, pool_mix, pool_scale, conv_w, ssm_a_re, ssm_a_im, ssm_log_step, ssm_b_re,
           ssm_b_im, ssm_c_re, ssm_c_im, ssm_d, ssm_glu_w, ssm_glu_b, w_branch, w_gate, b_gate, w_o,
           mem_norm_g, w_xq, w_xk, w_xv, xq_norm_g, xk_norm_g, w_xo):
    B, T, D = x_prompt.shape
    Bs, Ts, _ = x_sample.shape
    depth = norm_g.shape[0]
    Mp, Ms = B * T, Bs * Ts
    W = pool_scale.shape[1]
    dh = k_norm_g.shape[1]
    d_ff = ffn_out.shape[2]
    n_mem = mem_prompt.shape[1]
    Wx = w_xq.shape[2]
    SG, SN = ssm_a_re.shape[1:]
    NB = SG * SN // LANES
    past_len = page_table.shape[1] * cache_attn_k.shape[2]
    assert cache_idx_k.shape[-1] == dh and W == N_HEADS_C * dh == IDX_HEADS * dh
    TM = 1024

    o_k = 5 * W
    o_iq = o_k + 2 * dh
    o_ik = o_iq + W
    o_xs = o_ik + dh + IDX_HEADS
    tail_w = 2 * LANES
    Q_BLK, IQ_BLK, XS_BLK = 4, 5, 6
    TAIL_BLK = 7 * W // tail_w
    IW_OFF = 3 * dh
    n_z = 7 * W + tail_w

    h = jnp.concatenate([x_prompt.reshape(Mp, D), x_sample.reshape(Ms, D)], axis=0)
    caches = (cache_attn_k, cache_attn_v, cache_idx_k)
    mem_rows = mem_prompt.reshape(B * n_mem, D)
    outs = [[] for _ in range(16)]

    def ffn(h, l, i, g):
        n = rmsnorm_rows(h, g)
        act = matmul(n, [(ffn_in, (l, i), 0), (ffn_in, (l, i), d_ff // 512)], n_out=d_ff, tn=512, tm=TM,
                     mode="swiglu", out_dtype=BF16, name="ffn_in")
        return matmul(act, [(ffn_out, (l, i), 0)], n_out=D, tn=512, tm=TM, mode="residual", aux=h, scale=0.5,
                      name="ffn_out")

    for l in range(depth):
        h = ffn(h, l, 0, norm_g[l, 0])

        u = rmsnorm_rows(h, norm_g[l, 1])
        wl = w_in[l]
        w_in2 = jnp.concatenate([wl[:, 0:o_k], wl[:, o_iq:o_iq + W], wl[:, o_xs:o_xs + W], wl[:, o_k:o_k + 2 * dh],
                                 wl[:, o_ik:o_ik + dh + IDX_HEADS],
                                 jnp.zeros((D, tail_w - 3 * dh - IDX_HEADS), F32)], axis=1)
        z = matmul(u, [(w_in2, (), 0)], n_out=n_z, tn=768, tm=TM, name="in_proj")
        gates = matmul(u, [(w_gate, (l,), 0)], n_out=4 * D, tn=1024, tm=TM, mode="bias_sigmoid",
                       aux=b_gate[l].reshape(1, 4 * D), out_dtype=BF16, name="gates")

        wmix = pool_mix[l].astype(BF16)
        ya_p, yb_p, npool_p, nconv_p = pool_conv(z, 0, B, T, 1, 512, 0, None, None, wmix, pool_scale[l], conv_w[l], W)
        ya_s, yb_s, npool_s, nconv_s = pool_conv(z, Mp, Bs, Ts, 16, Ts, past_len, state_pool[l], state_conv[l],
                                                 wmix, pool_scale[l], conv_w[l], W)

        kn, vv, ik = kv_prep(z, TAIL_BLK, tail_w, k_norm_g[l], dh)
        gq_tiled = jnp.tile(q_norm_g[l], N_HEADS_C).reshape(1, W)
        yc_p = attn_prompt(z, Q_BLK, IQ_BLK, TAIL_BLK, tail_w, IW_OFF, kn[:Mp].reshape(B, T, dh),
                           vv[:Mp].reshape(B, T, dh), ik[:Mp].reshape(B, T, dh), gq_tiled, B, T, W, dh)
        yc_s = attn_sample(z, Mp, Q_BLK, IQ_BLK, TAIL_BLK, tail_w, IW_OFF, kn, vv, ik, caches, page_table, l,
                           gq_tiled, Bs, Ts, W, dh)

        ab_re, ab_im, co_re, co_im = ssm_params(ssm_a_re[l], ssm_a_im[l], ssm_log_step[l])
        flat = lambda a: a.reshape(1, SG * SN)
        lev_re, lev_im, pw_re, pw_im = ssm_tables(flat(ab_re), flat(ab_im), SUBLANES)
        rep = lambda a: _lane_blocks(jnp.broadcast_to(flat(a), (SUBLANES, SG * SN)))
        consts = (
            _lane_blocks(_block_diag(ssm_b_re[l].transpose(0, 2, 1))).astype(BF16),
            _lane_blocks(_block_diag(ssm_b_im[l].transpose(0, 2, 1))).astype(BF16),
            _block_diag(ssm_c_re[l].transpose(0, 2, 1)).reshape(NB, LANES, W).astype(BF16),
            _block_diag(ssm_c_im[l].transpose(0, 2, 1)).reshape(NB, LANES, W).astype(BF16),
            rep(co_re), rep(co_im), _lane_blocks(lev_re), _lane_blocks(lev_im),
            _lane_blocks(pw_re), _lane_blocks(pw_im),
            ssm_d[l].reshape(1, W), ssm_glu_w[l].astype(BF16), ssm_glu_b[l].reshape(1, W))
        yd_p, sre_p, sim_p = ssm_mixer(z, XS_BLK, 0, B, T, consts, None)
        h0 = (state_ssm_re[l].reshape(Bs, NB, LANES), state_ssm_im[l].reshape(Bs, NB, LANES))
        yd_s, sre_s, sim_s = ssm_mixer(z, XS_BLK, Mp, Bs, Ts, consts, h0)

        cat = lambda a, b: jnp.concatenate([a, b], axis=0)
        merged = gated_merge([cat(ya_p, ya_s), cat(yb_p, yb_s), cat(yc_p, yc_s), cat(yd_p, yd_s)], gates, w_branch, l)
        h = matmul(merged, [(w_o, (l,), 0)], n_out=D, tn=512, tm=TM, mode="residual", aux=h, scale=1.0, name="w_o")

        mn = rmsnorm_rows(mem_rows, mem_norm_g[l])
        mk = matmul(mn, [(w_xk, (l,), 0)], n_out=Wx, tn=Wx, tm=B * n_mem, mode="headnorm",
                    aux=jnp.tile(xk_norm_g[l], X_HEADS).reshape(1, Wx), group=Wx // X_HEADS, name="mem_k")
        mv = matmul(mn, [(w_xv, (l,), 0)], n_out=Wx, tn=Wx, tm=B * n_mem, name="mem_v")
        n2 = rmsnorm_rows(h, norm_g[l, 2])
        qx = matmul(n2, [(w_xq, (l,), 0)], n_out=Wx, tn=Wx, tm=TM, name="w_xq")
        xa_p = cross_attention(qx, 0, B, T, 1, 512, mk.reshape(1, B, n_mem, Wx), mv.reshape(1, B, n_mem, Wx), 0,
                               xq_norm_g[l])
        xa_s = cross_attention(qx, Mp, Bs, Ts, 8, Ts, cache_mem_k.reshape(depth, Bs, n_mem * X_HEADS, Wx // X_HEADS),
                               cache_mem_v.reshape(depth, Bs, n_mem * X_HEADS, Wx // X_HEADS), l, xq_norm_g[l])
        h = matmul(cat(xa_p, xa_s), [(w_xo, (l,), 0)], n_out=D, tn=512, tm=TM, mode="residual", aux=h, scale=1.0,
                   name="w_xo")

        h = ffn(h, l, 1, norm_g[l, 3])

        xh = Wx // X_HEADS
        layer_out = (kn[:Mp].reshape(B, T, dh), vv[:Mp].reshape(B, T, dh), ik[:Mp].reshape(B, T, dh),
                     mk.reshape(B, n_mem, X_HEADS, xh), mv.reshape(B, n_mem, X_HEADS, xh), npool_p, nconv_p,
                     sre_p.reshape(B, SG, SN), sim_p.reshape(B, SG, SN),
                     kn[Mp:].reshape(Bs, Ts, dh), vv[Mp:].reshape(Bs, Ts, dh), ik[Mp:].reshape(Bs, Ts, dh),
                     npool_s, nconv_s, sre_s.reshape(Bs, SG, SN), sim_s.reshape(Bs, SG, SN))
        for acc, val in zip(outs, layer_out):
            acc.append(val)

    return (h[:Mp].reshape(B, T, D), h[Mp:].reshape(Bs, Ts, D)) + tuple(jnp.stack(o) for o in outs)
```

```python
import functools
import math

import jax
import jax.numpy as jnp
from jax import lax
from jax.experimental import pallas as pl
from jax.experimental.pallas import tpu as pltpu

F32, BF16, I32 = jnp.float32, jnp.bfloat16, jnp.int32
EPS = 1e-6
NEG_INF = float("-inf")

V7X_VMEM_BYTES = 64 * 1024 * 1024
VMEM_LIMIT = V7X_VMEM_BYTES - 8 * 1024 * 1024
LANES = 128
SUBLANES = 8

POOL_WINDOWS = (2, 4, 8, 16)
POOL_HIST = 16
CONV_WIDTH = 3
CONV_HIST = 8
N_HEADS_C = 8
IDX_HEADS = 8
TOPK_MAX = 256
X_HEADS = 4
SSM_CHUNK = 256
SSM_SEQ_ROWS = 128


def _cparams(sem, vmem=VMEM_LIMIT):
    return pltpu.CompilerParams(dimension_semantics=sem, vmem_limit_bytes=vmem)


def _dot(a, b):
    return jnp.dot(a, b, preferred_element_type=F32)


def _dot_nt(a, b):
    return lax.dot_general(a, b, (((1,), (1,)), ((), ())), preferred_element_type=F32)


def _rmsnorm_kernel(x_ref, g_ref, o_ref):
    x = x_ref[...]
    ms = jnp.mean(x * x, axis=-1, keepdims=True)
    o_ref[...] = (x * lax.rsqrt(ms + EPS) * g_ref[...]).astype(o_ref.dtype)


def rmsnorm_rows(x, g, tm=512):
    M, D = x.shape
    return pl.pallas_call(
        _rmsnorm_kernel,
        grid=(M // tm,),
        in_specs=[pl.BlockSpec((tm, D), lambda i: (i, 0)), pl.BlockSpec((1, D), lambda i: (0, 0))],
        out_specs=pl.BlockSpec((tm, D), lambda i: (i, 0)),
        out_shape=jax.ShapeDtypeStruct((M, D), BF16),
        compiler_params=_cparams(("arbitrary",)),
        name="rmsnorm",
    )(x, g.reshape(1, D))


def _rms_rows_bf16(x, g):
    ms = jnp.mean(x * x, axis=-1, keepdims=True)
    return (x * lax.rsqrt(ms + EPS) * g).astype(BF16)


def _mm_kernel(*refs, nw, mode, scale, sub, group, norm):
    x_ref = refs[0]
    w_refs = refs[1:1 + nw]
    p = 1 + nw
    if norm:
        g_ref = refs[p]
        p += 1
    aux_ref = None
    if mode in ("bias_sigmoid", "residual", "headnorm"):
        aux_ref = refs[p]
        p += 1
    o_ref = refs[p]
    scr = refs[p + 1:p + 1 + nw]

    @pl.when(pl.program_id(1) == 0)
    def _():
        for w_ref, s in zip(w_refs, scr):
            s[...] = w_ref[...].astype(BF16)

    tm = x_ref.shape[0]

    def body(r, carry):
        rows = pl.ds(pl.multiple_of(r * sub, sub), sub)
        x = x_ref[rows, :]
        if norm:
            x = _rms_rows_bf16(x, g_ref[...])
        acc = [_dot(x, s[...]) for s in scr]
        if mode == "swiglu":
            a, b = acc
            y = (a * jax.nn.sigmoid(a)) * b
        elif mode == "bias_sigmoid":
            y = jax.nn.sigmoid(acc[0] + aux_ref[...])
        elif mode == "residual":
            y = aux_ref[rows, :] + scale * acc[0]
        elif mode == "headnorm":
            a = acc[0]
            parts = []
            for h in range(a.shape[1] // group):
                ah = a[:, h * group:(h + 1) * group]
                ms = jnp.mean(ah * ah, axis=-1, keepdims=True)
                parts.append(ah * lax.rsqrt(ms + EPS))
            y = jnp.concatenate(parts, axis=1) * aux_ref[...]
        else:
            y = acc[0]
        o_ref[rows, :] = y.astype(o_ref.dtype)
        return carry

    lax.fori_loop(0, tm // sub, body, 0, unroll=True)


def matmul(x, weights, *, n_out, tn, tm, mode="plain", aux=None, scale=1.0, out_dtype=F32, group=LANES,
           norm_g=None, name="matmul"):
    M, K = x.shape
    nw = len(weights)
    sub = min(tm, 256)
    in_specs = [pl.BlockSpec((tm, K), lambda j, i: (i, 0))]
    args = [x]
    for arr, lead, coff in weights:
        nl = len(lead)
        in_specs.append(pl.BlockSpec((None,) * nl + (K, tn),
                                     functools.partial(lambda j, i, lead, coff: (*lead, 0, coff + j), lead=lead, coff=coff)))
        args.append(arr)
    if norm_g is not None:
        in_specs.append(pl.BlockSpec((1, K), lambda j, i: (0, 0)))
        args.append(norm_g.reshape(1, K))
    if mode in ("bias_sigmoid", "headnorm"):
        in_specs.append(pl.BlockSpec((1, tn), lambda j, i: (0, j)))
        args.append(aux)
    elif mode == "residual":
        in_specs.append(pl.BlockSpec((tm, tn), lambda j, i: (i, j)))
        args.append(aux)
    return pl.pallas_call(
        functools.partial(_mm_kernel, nw=nw, mode=mode, scale=scale, sub=sub, group=group, norm=norm_g is not None),
        grid=(n_out // tn, M // tm),
        in_specs=in_specs,
        out_specs=pl.BlockSpec((tm, tn), lambda j, i: (i, j)),
        out_shape=jax.ShapeDtypeStruct((M, n_out), out_dtype),
        scratch_shapes=[pltpu.VMEM((K, tn), BF16) for _ in range(nw)],
        compiler_params=_cparams(("arbitrary", "arbitrary")),
        name=name,
    )(*args)


def _merge_kernel(*refs, nb):
    h_ref, g_ref = refs[0:2]
    br = refs[2:2 + nb]
    wg = refs[2 + nb:2 + 2 * nb]
    bg = refs[2 + 2 * nb:2 + 3 * nb]
    wb = refs[2 + 3 * nb:2 + 4 * nb]
    o_ref = refs[2 + 4 * nb]
    sg = refs[3 + 4 * nb:3 + 5 * nb]
    sb = refs[3 + 5 * nb:3 + 6 * nb]

    @pl.when(pl.program_id(1) == 0)
    def _():
        for w_ref, s in zip(wg + wb, sg + sb):
            s[...] = w_ref[...].astype(BF16)

    tm = o_ref.shape[0]
    sub = min(tm, 256)

    def body(r, carry):
        rows = pl.ds(pl.multiple_of(r * sub, sub), sub)
        u = _rms_rows_bf16(h_ref[rows, :], g_ref[...])
        acc = None
        for k in range(nb):
            gate = jax.nn.sigmoid(_dot(u, sg[k][...]) + bg[k][...])
            t = gate * _dot(br[k][rows, :], sb[k][...])
            acc = t if acc is None else acc + t
        o_ref[rows, :] = acc.astype(o_ref.dtype)
        return carry

    lax.fori_loop(0, tm // sub, body, 0, unroll=True)


def gated_merge(h, norm_g, branches, w_gate, b_gate, w_branch, layer, *, tn=256, tm=1024):
    nb = len(branches)
    M, W = branches[0].shape
    D = w_branch.shape[-1]
    nj = D // tn
    bias = b_gate.reshape(b_gate.shape[0], 1, nb * D)

    def per_branch(shape):
        return [pl.BlockSpec(shape, functools.partial(lambda j, i, k: (layer, 0, k * nj + j), k=k)) for k in range(nb)]

    in_specs = [pl.BlockSpec((tm, D), lambda j, i: (i, 0)), pl.BlockSpec((1, D), lambda j, i: (0, 0))]
    in_specs += [pl.BlockSpec((tm, W), lambda j, i: (i, 0)) for _ in range(nb)]
    in_specs += per_branch((None, D, tn)) + per_branch((None, 1, tn))
    in_specs += [pl.BlockSpec((None, None, W, tn), functools.partial(lambda j, i, k: (layer, k, 0, j), k=k))
                 for k in range(nb)]
    return pl.pallas_call(
        functools.partial(_merge_kernel, nb=nb),
        grid=(nj, M // tm),
        in_specs=in_specs,
        out_specs=pl.BlockSpec((tm, tn), lambda j, i: (i, j)),
        out_shape=jax.ShapeDtypeStruct((M, D), BF16),
        scratch_shapes=[pltpu.VMEM((D, tn), BF16) for _ in range(nb)] + [pltpu.VMEM((W, tn), BF16) for _ in range(nb)],
        compiler_params=_cparams(("arbitrary", "arbitrary")),
        name="gated_merge",
    )(h, norm_g.reshape(1, D), *branches, *([w_gate] * nb), *([bias] * nb), *([w_branch] * nb))


def _kvprep_kernel(t_ref, g_ref, k_ref, v_ref, ik_ref, *, dh):
    t = t_ref[...]
    k = t[:, 0:dh]
    ms = jnp.mean(k * k, axis=-1, keepdims=True)
    k_ref[...] = k * lax.rsqrt(ms + EPS) * g_ref[...]
    v_ref[...] = t[:, dh:2 * dh]
    ik_ref[...] = t[:, 2 * dh:3 * dh]


def kv_prep(z, tail_block, tail_w, gk, dh, tm=1024):
    M = z.shape[0]
    out = jax.ShapeDtypeStruct((M, dh), F32)
    return pl.pallas_call(
        functools.partial(_kvprep_kernel, dh=dh),
        grid=(M // tm,),
        in_specs=[pl.BlockSpec((tm, tail_w), lambda i: (i, tail_block)), pl.BlockSpec((1, dh), lambda i: (0, 0))],
        out_specs=[pl.BlockSpec((tm, dh), lambda i: (i, 0))] * 3,
        out_shape=[out, out, out],
        compiler_params=_cparams(("arbitrary",)),
        name="kv_prep",
    )(z, gk.reshape(1, dh))


def _poolconv_kernel(*refs, G, T, W, pos0, has_state):
    xp_ref, xc_ref, bg_ref, cg_ref = refs[0:4]
    p = 4
    if has_state:
        pbuf_ref, cbuf_ref = refs[4:6]
        p = 6
    wmix_ref, pscale_ref, convw_ref = refs[p:p + 3]
    ya_ref, yb_ref, npool_ref, nconv_ref = refs[p + 3:p + 7]
    fullp, fullc = refs[p + 7:p + 9]
    PH, CH = POOL_HIST, CONV_HIST
    nh = CONV_WIDTH - 1
    gw = W // len(POOL_WINDOWS)
    c = pl.program_id(1)

    @pl.when(c == 0)
    def _():
        if has_state:
            fullp[:, 1:PH, :] = pbuf_ref[...]
            fullc[:, CH - nh:CH, :] = cbuf_ref[...]
        else:
            fullp[:, 0:PH, :] = jnp.zeros((G, PH, W), F32)
            fullc[:, 0:CH, :] = jnp.zeros((G, CH, W), F32)

    @pl.when(c > 0)
    def _():
        fullp[:, 0:PH, :] = fullp[:, T:T + PH, :]
        fullc[:, 0:CH, :] = fullc[:, T:T + CH, :]

    fullp[:, PH:PH + T, :] = xp_ref[...].reshape(G, T, W)
    t_idx = lax.broadcasted_iota(I32, (1, T, 1), 1) + (c * T + (pos0 + 1))
    for gi, w in enumerate(POOL_WINDOWS):
        cols = slice(gi * gw, (gi + 1) * gw)
        acc = fullp[:, PH:PH + T, cols]
        for j in range(1, w):
            acc = acc + fullp[:, PH - j:PH - j + T, cols]
        cnt = jnp.minimum(t_idx, w).astype(F32)
        d = acc / cnt - fullp[:, PH:PH + T, cols]
        y = _dot(d.reshape(G * T, gw).astype(BF16), wmix_ref[gi]) * pscale_ref[:, cols]
        ya_ref[:, cols] = y.astype(ya_ref.dtype)
    npool_ref[...] = fullp[:, T + 1:T + PH, :]

    fullc[:, CH:CH + T, :] = (cg_ref[...] * xc_ref[...]).reshape(G, T, W)
    y = None
    for j in range(CONV_WIDTH):
        wj = convw_ref[j:j + 1, :].reshape(1, 1, W)
        term = wj * fullc[:, CH - nh + j:CH - nh + j + T, :]
        y = term if y is None else y + term
    yb_ref[...] = (bg_ref[...] * y.reshape(G * T, W)).astype(yb_ref.dtype)
    nconv_ref[...] = fullc[:, CH + T - nh:CH + T, :]


def pool_conv(z, row0, Bt, T, G, TC, pos0, pool_buf, conv_buf, wmix_bf16, pool_scale, conv_w, W):
    has_state = pool_buf is not None
    assert G == 1 or TC == T
    R = G * TC
    nc = T // TC
    rb0 = row0 // R
    nh = CONV_WIDTH - 1

    def zspec(cb):
        return pl.BlockSpec((R, W), functools.partial(lambda i, c, cb: (rb0 + i * nc + c, cb), cb=cb))

    in_specs = [zspec(0), zspec(1), zspec(2), zspec(3)]
    args = [z, z, z, z]
    if has_state:
        in_specs += [pl.BlockSpec((G, POOL_HIST - 1, W), lambda i, c: (i, 0, 0)),
                     pl.BlockSpec((G, nh, W), lambda i, c: (i, 0, 0))]
        args += [pool_buf, conv_buf]
    nwin = len(POOL_WINDOWS)
    in_specs += [pl.BlockSpec((nwin, W // nwin, W // nwin), lambda i, c: (0, 0, 0)),
                 pl.BlockSpec((1, W), lambda i, c: (0, 0)),
                 pl.BlockSpec((CONV_WIDTH, W), lambda i, c: (0, 0))]
    args += [wmix_bf16, pool_scale.reshape(1, W), conv_w]
    return pl.pallas_call(
        functools.partial(_poolconv_kernel, G=G, T=TC, W=W, pos0=pos0, has_state=has_state),
        grid=(Bt // G, nc),
        in_specs=in_specs,
        out_specs=[pl.BlockSpec((R, W), lambda i, c: (i * nc + c, 0)), pl.BlockSpec((R, W), lambda i, c: (i * nc + c, 0)),
                   pl.BlockSpec((G, POOL_HIST - 1, W), lambda i, c: (i, 0, 0)),
                   pl.BlockSpec((G, nh, W), lambda i, c: (i, 0, 0))],
        out_shape=[jax.ShapeDtypeStruct((Bt * T, W), BF16), jax.ShapeDtypeStruct((Bt * T, W), BF16),
                   jax.ShapeDtypeStruct((Bt, POOL_HIST - 1, W), F32), jax.ShapeDtypeStruct((Bt, nh, W), F32)],
        scratch_shapes=[pltpu.VMEM((G, POOL_HIST + TC, W), F32), pltpu.VMEM((G, CONV_HIST + TC, W), F32)],
        compiler_params=_cparams(("arbitrary", "arbitrary")),
        name="pool_conv",
    )(*args)


def _ssm_params_kernel(are_ref, aim_ref, ls_ref, abre_ref, abim_ref, core_ref, coim_ref):
    a_re = are_ref[...]
    a_im = aim_ref[...]
    step = jnp.exp(ls_ref[...])
    decay = jnp.exp(step * a_re)
    ab_re = decay * jnp.cos(step * a_im)
    ab_im = decay * jnp.sin(step * a_im)
    den = a_re * a_re + a_im * a_im
    nr = ab_re - 1.0
    abre_ref[...] = ab_re
    abim_ref[...] = ab_im
    core_ref[...] = (nr * a_re + ab_im * a_im) / den
    coim_ref[...] = (ab_im * a_re - nr * a_im) / den


def ssm_params(a_re, a_im, log_step):
    G, N = a_re.shape
    out = jax.ShapeDtypeStruct((G, N), F32)
    return pl.pallas_call(_ssm_params_kernel, out_shape=[out, out, out, out], name="ssm_params")(
        a_re, a_im, log_step.reshape(G, 1))


def _scan_levels(C):
    return [1 << k for k in range(int(math.log2(C)))]


def _ssm_tables_kernel(abre_ref, abim_ref, lre_ref, lim_ref, pre_ref, pim_ref, *, C):
    ar = abre_ref[...]
    ai = abim_ref[...]
    N = ar.shape[1]
    row = lax.broadcasted_iota(I32, (C, N), 0)
    hr = jnp.where(row == 0, ar, 0.0)
    hi = jnp.where(row == 0, ai, 0.0)
    lre_ref[...] = jnp.zeros(lre_ref.shape, F32)
    lim_ref[...] = jnp.zeros(lim_ref.shape, F32)
    for k, s in enumerate(_scan_levels(C)):
        lre_ref[k:k + 1, :] = ar
        lim_ref[k:k + 1, :] = ai
        sr = jnp.where(row >= s, pltpu.roll(hr, s, 0), 0.0)
        si = jnp.where(row >= s, pltpu.roll(hi, s, 0), 0.0)
        hr, hi = hr + ar * sr - ai * si, hi + ar * si + ai * sr
        ar, ai = ar * ar - ai * ai, 2.0 * ar * ai
    pre_ref[...] = hr
    pim_ref[...] = hi


def ssm_tables(ab_re, ab_im, C):
    N = ab_re.shape[1]
    nlev = len(_scan_levels(C))
    lev = jax.ShapeDtypeStruct((SUBLANES * ((nlev + SUBLANES - 1) // SUBLANES), N), F32)
    pw = jax.ShapeDtypeStruct((C, N), F32)
    return pl.pallas_call(functools.partial(_ssm_tables_kernel, C=C), out_shape=[lev, lev, pw, pw],
                          name="ssm_tables")(ab_re, ab_im)


def _gelu_tanh(x):
    return 0.5 * x * (1.0 + jnp.tanh(math.sqrt(2.0 / math.pi) * (x + 0.044715 * (x * x * x))))


def _ssm_kernel(*refs, R, T, NB, chained):
    (xs_ref, bre_ref, bim_ref, cre_ref, cim_ref, core_ref, coim_ref, lre_ref, lim_ref) = refs[0:9]
    p = 9
    if chained:
        pre_ref, pim_ref = refs[p:p + 2]
        p += 2
    else:
        h0r_ref, h0i_ref = refs[p:p + 2]
        p += 2
    d_ref, gw_ref, gb_ref = refs[p:p + 3]
    yd_ref, sre_ref, sim_ref = refs[p + 3:p + 6]
    p += 6
    if chained:
        car_ref, cai_ref = refs[p:p + 2]
    G = R // T
    xs = xs_ref[...]
    xb = xs.astype(BF16)
    tpos = lax.broadcasted_iota(I32, (R, LANES), 0) % SUBLANES
    levels = _scan_levels(SUBLANES)

    if chained:
        c = pl.program_id(1)

        @pl.when(c == 0)
        def _():
            car_ref[...] = jnp.zeros(car_ref.shape, F32)
            cai_ref[...] = jnp.zeros(cai_ref.shape, F32)

    y = d_ref[...] * xs
    for cb in range(NB):
        pr = _dot(xb, bre_ref[cb])
        pi = _dot(xb, bim_ref[cb])
        cor = core_ref[cb][0:1, :]
        coi = coim_ref[cb][0:1, :]
        hr = cor * pr - coi * pi
        hi = cor * pi + coi * pr
        if not chained:
            ar = lre_ref[cb][0:1, :]
            ai = lim_ref[cb][0:1, :]
            h0r = jnp.broadcast_to(h0r_ref[:, cb:cb + 1, :], (G, T, LANES)).reshape(R, LANES)
            h0i = jnp.broadcast_to(h0i_ref[:, cb:cb + 1, :], (G, T, LANES)).reshape(R, LANES)
            first = tpos == 0
            hr = hr + jnp.where(first, ar * h0r - ai * h0i, 0.0)
            hi = hi + jnp.where(first, ar * h0i + ai * h0r, 0.0)
        for k, s in enumerate(levels):
            ar = lre_ref[cb][k:k + 1, :]
            ai = lim_ref[cb][k:k + 1, :]
            sr = jnp.where(tpos >= s, pltpu.roll(hr, s, 0), 0.0)
            si = jnp.where(tpos >= s, pltpu.roll(hi, s, 0), 0.0)
            hr, hi = hr + ar * sr - ai * si, hi + ar * si + ai * sr
        if chained:
            cr = car_ref[cb:cb + 1, :]
            ci = cai_ref[cb:cb + 1, :]
            pwr = pre_ref[cb]
            pwi = pim_ref[cb]
            grs, gis = [], []
            for v in range(R // SUBLANES):
                rows = slice(v * SUBLANES, (v + 1) * SUBLANES)
                cbr = jnp.broadcast_to(cr, (SUBLANES, LANES))
                cbi = jnp.broadcast_to(ci, (SUBLANES, LANES))
                gr = hr[rows] + pwr * cbr - pwi * cbi
                gi = hi[rows] + pwr * cbi + pwi * cbr
                cr = gr[SUBLANES - 1:SUBLANES, :]
                ci = gi[SUBLANES - 1:SUBLANES, :]
                grs.append(gr)
                gis.append(gi)
            hr = jnp.concatenate(grs, axis=0)
            hi = jnp.concatenate(gis, axis=0)
            car_ref[cb:cb + 1, :] = cr
            cai_ref[cb:cb + 1, :] = ci
        else:
            sre_ref[:, cb:cb + 1, :] = hr.reshape(G, T, LANES)[:, T - 1:T, :]
            sim_ref[:, cb:cb + 1, :] = hi.reshape(G, T, LANES)[:, T - 1:T, :]
        y = y + _dot(hr.astype(BF16), cre_ref[cb]) - _dot(hi.astype(BF16), cim_ref[cb])

    z = _gelu_tanh(y)
    out = z * jax.nn.sigmoid(_dot(z.astype(BF16), gw_ref[...]) + gb_ref[...])
    yd_ref[...] = out.astype(yd_ref.dtype)

    if chained:
        @pl.when(c == pl.num_programs(1) - 1)
        def _():
            sre_ref[0] = car_ref[...]
            sim_ref[0] = cai_ref[...]


def ssm_mixer(z, xs_block, row0, Bt, T, consts, h0):
    (bre3, bim3, cre3, cim3, core3, coim3, lre3, lim3, pre3, pim3, dvec, gw, gb) = consts
    NB = bre3.shape[0]
    W = dvec.shape[1]
    chained = h0 is None
    assert chained or T == SUBLANES
    R = SSM_CHUNK if chained else SSM_SEQ_ROWS
    rb0 = row0 // R

    def full(a):
        nd = a.ndim
        return pl.BlockSpec(a.shape, lambda *_: (0,) * nd)

    if chained:
        nchunk = T // R
        grid = (Bt, nchunk)
        xs_spec = pl.BlockSpec((R, W), lambda b, c: (rb0 + b * nchunk + c, xs_block))
        st_args, st_specs = [pre3, pim3], [full(pre3), full(pim3)]
        yd_spec = pl.BlockSpec((R, W), lambda b, c: (b * nchunk + c, 0))
        s_spec = pl.BlockSpec((1, NB, LANES), lambda b, c: (b, 0, 0))
        scratch = [pltpu.VMEM((NB, LANES), F32), pltpu.VMEM((NB, LANES), F32)]
        sem = ("arbitrary", "arbitrary")
        Tk = R
    else:
        G = R // T
        grid = (Bt // G,)
        xs_spec = pl.BlockSpec((R, W), lambda i: (rb0 + i, xs_block))
        st_args = list(h0)
        st_specs = [pl.BlockSpec((G, NB, LANES), lambda i: (i, 0, 0))] * 2
        yd_spec = pl.BlockSpec((R, W), lambda i: (i, 0))
        s_spec = pl.BlockSpec((G, NB, LANES), lambda i: (i, 0, 0))
        scratch = []
        sem = ("arbitrary",)
        Tk = T
    shared = [bre3, bim3, cre3, cim3, core3, coim3, lre3, lim3]
    tailc = [dvec, gw, gb]
    s_shape = jax.ShapeDtypeStruct((Bt, NB, LANES), F32)
    return pl.pallas_call(
        functools.partial(_ssm_kernel, R=R, T=Tk, NB=NB, chained=chained),
        grid=grid,
        in_specs=[xs_spec] + [full(a) for a in shared] + st_specs + [full(a) for a in tailc],
        out_specs=[yd_spec, s_spec, s_spec],
        out_shape=[jax.ShapeDtypeStruct((Bt * T, W), BF16), s_shape, s_shape],
        scratch_shapes=scratch,
        compiler_params=_cparams(sem),
        name="ssm_chained" if chained else "ssm_stateful",
    )(z, *shared, *st_args, *tailc)


def _sort_keys(score):
    b = pltpu.bitcast(score, I32)
    key = jnp.where(b < 0, b ^ jnp.int32(0x7FFFFFFF), b)
    return jnp.where(key == -1, 0, key)


def _kth_largest_key(key, k):
    rows = key.shape[0]

    def body(i, t):
        cand = t + (jnp.int32(1) << (31 - i))
        cnt = jnp.sum(jnp.where(key >= cand, 1.0, 0.0), axis=-1, keepdims=True)
        return jnp.where(cnt >= float(k), cand, t)

    return lax.fori_loop(0, 32, body, jnp.full((rows, 1), -2 ** 31, I32))


def _blocked_prefix(eq, tri, offset):
    outs = []
    run = offset
    for j in range(eq.shape[1] // LANES):
        blk = eq[:, j * LANES:(j + 1) * LANES].astype(BF16)
        pj = _dot(blk, tri) + run
        outs.append(pj)
        run = pj[:, LANES - 1:LANES]
    return outs, run


def _attn_prompt_kernel(q_ref, iq_ref, tail_ref, k_ref, v_ref, ik_ref, gq_ref, e_ref, tri_ref, o_ref,
                        kb, vb, ikb, bias, *, TQ, L, dh, iw_off, n_sel, n_buckets):
    qb = pl.program_id(1)

    @pl.when(qb == 0)
    def _():
        kb[...] = k_ref[0].astype(BF16)
        vb[...] = v_ref[0].astype(BF16)
        ikb[...] = ik_ref[0].astype(BF16)

    scale = dh ** -0.5
    q = q_ref[...]
    q2 = q * q
    hi = q2.astype(BF16)
    lo = (q2 - hi.astype(F32)).astype(BF16)
    ss = _dot(hi, e_ref[...]) + _dot(lo, e_ref[...])
    qn = (q * lax.rsqrt(ss * (1.0 / dh) + EPS) * gq_ref[...] * scale).astype(BF16)
    iqs = (iq_ref[...] * scale).astype(BF16)
    iw = tail_ref[:, iw_off:iw_off + IDX_HEADS] * (IDX_HEADS ** -0.5)

    def attend(Lk):
        score = None
        for h in range(IDX_HEADS):
            lg = _dot_nt(iqs[:, h * dh:(h + 1) * dh], ikb[0:Lk, :])
            t = jnp.maximum(lg, 0.0) * iw[:, h:h + 1]
            score = t if score is None else score + t
        col = lax.broadcasted_iota(I32, (TQ, Lk), 1)
        qpos = qb * TQ + lax.broadcasted_iota(I32, (TQ, Lk), 0)
        causal = col <= qpos
        score = jnp.where(causal, score, NEG_INF)

        key = _sort_keys(score)
        thr = _kth_largest_key(key, n_sel)
        gt = key > thr
        eq = key == thr
        need = float(n_sel) - jnp.sum(jnp.where(gt, 1.0, 0.0), axis=-1, keepdims=True)
        pref, _ = _blocked_prefix(jnp.where(eq, 1.0, 0.0), tri_ref[...], jnp.zeros((TQ, 1), F32))
        pref = jnp.concatenate(pref, axis=1)
        sel = (gt | (eq & (pref <= need))) & causal
        bias[:, 0:Lk] = jnp.where(sel, 0.0, NEG_INF)

        for h in range(N_HEADS_C):
            lg = _dot_nt(qn[:, h * dh:(h + 1) * dh], kb[0:Lk, :]) + bias[:, 0:Lk]
            m = jnp.max(lg, axis=-1, keepdims=True)
            e = jnp.exp(lg - m)
            den = jnp.sum(e, axis=-1, keepdims=True)
            o = _dot(e.astype(BF16), vb[0:Lk, :]) / den
            o_ref[:, h * dh:(h + 1) * dh] = o.astype(o_ref.dtype)

    per = (L // TQ) // n_buckets
    for bk in range(n_buckets):
        pl.when(qb // per == bk)(functools.partial(attend, (bk + 1) * per * TQ))


def attn_prompt(z, q_block, iq_block, tail_block, tail_w, iw_off, kn, v, ik, gq_tiled, B, T, W, dh, TQ=256,
                n_buckets=4):
    nq = T // TQ
    n_sel = min(TOPK_MAX, T // 4)
    eye = (jnp.arange(W)[:, None] // dh == jnp.arange(W)[None, :] // dh).astype(BF16)
    tri = (jnp.arange(LANES)[:, None] <= jnp.arange(LANES)[None, :]).astype(BF16)
    kspec = pl.BlockSpec((1, T, dh), lambda b, i: (b, 0, 0))
    return pl.pallas_call(
        functools.partial(_attn_prompt_kernel, TQ=TQ, L=T, dh=dh, iw_off=iw_off, n_sel=n_sel, n_buckets=n_buckets),
        grid=(B, nq),
        in_specs=[pl.BlockSpec((TQ, W), lambda b, i: (b * nq + i, q_block)),
                  pl.BlockSpec((TQ, W), lambda b, i: (b * nq + i, iq_block)),
                  pl.BlockSpec((TQ, tail_w), lambda b, i: (b * nq + i, tail_block)),
                  kspec, kspec, kspec,
                  pl.BlockSpec((1, W), lambda b, i: (0, 0)),
                  pl.BlockSpec((W, W), lambda b, i: (0, 0)),
                  pl.BlockSpec((LANES, LANES), lambda b, i: (0, 0))],
        out_specs=pl.BlockSpec((TQ, W), lambda b, i: (b * nq + i, 0)),
        out_shape=jax.ShapeDtypeStruct((B * T, W), BF16),
        scratch_shapes=[pltpu.VMEM((T, dh), BF16), pltpu.VMEM((T, dh), BF16), pltpu.VMEM((T, dh), BF16),
                        pltpu.VMEM((TQ, T), F32)],
        compiler_params=_cparams(("arbitrary", "arbitrary")),
        name="attn_prompt",
    )(z, z, z, kn, v, ik, gq_tiled, eye, tri)


def _xattn_kernel(q_ref, mk_ref, mv_ref, g_ref, o_ref, *, G, T, dh, scale):
    g = g_ref[...]
    for s in range(G):
        rows = slice(s * T, (s + 1) * T)
        for h in range(X_HEADS):
            cols = slice(h * dh, (h + 1) * dh)
            qh = q_ref[rows, cols]
            ms = jnp.mean(qh * qh, axis=-1, keepdims=True)
            qn = (qh * lax.rsqrt(ms + EPS) * g).astype(BF16)
            lg = _dot_nt(qn, mk_ref[s, :, cols].astype(BF16)) * scale
            m = jnp.max(lg, axis=-1, keepdims=True)
            e = jnp.exp(lg - m)
            den = jnp.sum(e, axis=-1, keepdims=True)
            o = _dot(e.astype(BF16), mv_ref[s, :, cols].astype(BF16)) / den
            o_ref[rows, cols] = o.astype(o_ref.dtype)


def _xattn_rows_kernel(q_ref, mk_ref, mv_ref, g_ref, o_ref, *, G, T, dh, scale):
    g = g_ref[...]
    H = X_HEADS
    n = mk_ref.shape[1]
    own = (lax.broadcasted_iota(I32, (H * T, n), 1) % H) == (lax.broadcasted_iota(I32, (H * T, n), 0) // T)
    for s in range(G):
        rows = slice(s * T, (s + 1) * T)
        parts = []
        for h in range(H):
            qh = q_ref[rows, h * dh:(h + 1) * dh]
            ms = jnp.mean(qh * qh, axis=-1, keepdims=True)
            parts.append(qh * lax.rsqrt(ms + EPS) * g)
        qs = jnp.concatenate(parts, axis=0).astype(BF16)
        lg = jnp.where(own, _dot_nt(qs, mk_ref[s].astype(BF16)) * scale, NEG_INF)
        m = jnp.max(lg, axis=-1, keepdims=True)
        e = jnp.exp(lg - m)
        den = jnp.sum(e, axis=-1, keepdims=True)
        o = _dot(e.astype(BF16), mv_ref[s].astype(BF16)) / den
        for h in range(H):
            o_ref[rows, h * dh:(h + 1) * dh] = o[h * T:(h + 1) * T, :].astype(o_ref.dtype)


def cross_attention(qx, row0, Bt, T, G, TQ, mem_k, mem_v, layer, gq):
    Wx = qx.shape[1]
    dh = Wx // X_HEADS
    n_rows, wm = mem_k.shape[2:]
    assert G == 1 or TQ == T
    R = G * TQ
    nt = T // TQ
    rb0 = row0 // R
    mspec = pl.BlockSpec((None, G, n_rows, wm), lambda i, t: (layer, i, 0, 0))
    body = _xattn_kernel if wm == Wx else _xattn_rows_kernel
    return pl.pallas_call(
        functools.partial(body, G=G, T=TQ, dh=dh, scale=dh ** -0.5),
        grid=(Bt // G, nt),
        in_specs=[pl.BlockSpec((R, Wx), lambda i, t: (rb0 + i * nt + t, 0)), mspec, mspec,
                  pl.BlockSpec((1, dh), lambda i, t: (0, 0))],
        out_specs=pl.BlockSpec((R, Wx), lambda i, t: (i * nt + t, 0)),
        out_shape=jax.ShapeDtypeStruct((Bt * T, Wx), BF16),
        compiler_params=_cparams(("arbitrary", "arbitrary")),
        name="cross_attention",
    )(qx, mem_k, mem_v, gq.reshape(1, dh))


def _attn_sample_kernel(pt_ref, q_ref, iq_ref, tail_ref, kn_ref, vn_ref, ikn_ref, ck_hbm, cv_hbm, cik_hbm,
                        gq_ref, e_ref, tri_ref, o_ref,
                        kp, vp, ikp, knp, vnp, iknp, qs_scr, iqs_scr, iw_scr, score, bias, sems,
                        *, G, T, dh, layer, n_pages, hp, iw_off, n_sel):
    step = pl.program_id(0)
    R = G * T
    H = N_HEADS_C
    LP = n_pages * hp
    LC = LP + LANES

    def page_copies(g, p):
        page = pt_ref[step * G + g, p]
        dst_cols = pl.ds(pl.multiple_of(p * hp, hp), hp)
        return [pltpu.make_async_copy(src.at[layer, page], dst.at[g, :, dst_cols], sems.at[s, g])
                for s, (src, dst) in enumerate(((ck_hbm, kp), (cv_hbm, vp), (cik_hbm, ikp)))]

    def start_all(i, c):
        g = i // n_pages
        for cp in page_copies(g, i - g * n_pages):
            cp.start()
        return c

    def wait_seq(g):
        def wait_page(p, c):
            for cp in page_copies(g, p):
                cp.wait()
            return c
        lax.fori_loop(0, n_pages, wait_page, 0)

    lax.fori_loop(0, G * n_pages, start_all, 0)

    @pl.when(step == 0)
    def _():
        knp[...] = jnp.zeros(knp.shape, F32)
        vnp[...] = jnp.zeros(vnp.shape, F32)
        iknp[...] = jnp.zeros(iknp.shape, F32)

    scale = dh ** -0.5
    q = q_ref[...]
    q2 = q * q
    hi = q2.astype(BF16)
    lo = (q2 - hi.astype(F32)).astype(BF16)
    ss = _dot(hi, e_ref[...]) + _dot(lo, e_ref[...])
    qs_scr[...] = q * lax.rsqrt(ss * (1.0 / dh) + EPS) * gq_ref[...] * scale
    iqs_scr[...] = iq_ref[...] * scale
    iw_blk = (iw_off // LANES) * LANES
    iw_lane = iw_off - iw_blk
    iw_scr[...] = tail_ref[:, iw_blk:iw_blk + LANES]

    new_ok = lax.broadcasted_iota(I32, (R, LANES), 1) <= lax.broadcasted_iota(I32, (R, LANES), 0) % T

    def stack_heads(ref, rows):
        return jnp.concatenate([ref[rows, h * dh:(h + 1) * dh] for h in range(H)], axis=0).astype(BF16)

    def score_body(g, c):
        wait_seq(g)
        rows = pl.ds(pl.multiple_of(g * T, T), T)
        iknp[0:T, :] = ikn_ref[rows, :]
        qs = stack_heads(iqs_scr, rows)
        lg_p = _dot(qs, ikp[g].astype(BF16))
        lg_n = _dot_nt(qs, iknp[...].astype(BF16))
        iw = iw_scr[rows, :] * (IDX_HEADS ** -0.5)
        sp = sn = None
        for h in range(IDX_HEADS):
            w = iw[:, iw_lane + h:iw_lane + h + 1]
            hs = slice(h * T, (h + 1) * T)
            tp = jnp.maximum(lg_p[hs], 0.0) * w
            tn = jnp.maximum(lg_n[hs], 0.0) * w
            sp, sn = (tp, tn) if sp is None else (sp + tp, sn + tn)
        score[rows, 0:LP] = sp
        score[rows, LP:LC] = sn
        return c

    lax.fori_loop(0, G, score_body, 0)
    score[:, LP:LC] = jnp.where(new_ok, score[:, LP:LC], NEG_INF)

    key = _sort_keys(score[...])
    thr = _kth_largest_key(key, n_sel)
    gt = key > thr
    eq = key == thr
    need = float(n_sel) - jnp.sum(jnp.where(gt, 1.0, 0.0), axis=-1, keepdims=True)
    pref, _ = _blocked_prefix(jnp.where(eq, 1.0, 0.0), tri_ref[...], jnp.zeros((R, 1), F32))
    pref = jnp.concatenate(pref, axis=1)
    bias[...] = jnp.where(gt | (eq & (pref <= need)), 0.0, NEG_INF)
    bias[:, LP:LC] = jnp.where(new_ok, bias[:, LP:LC], NEG_INF)

    def attn_body(g, c):
        rows = pl.ds(pl.multiple_of(g * T, T), T)
        knp[0:T, :] = kn_ref[rows, :]
        vnp[0:T, :] = vn_ref[rows, :]
        qs = stack_heads(qs_scr, rows)
        bh = jnp.concatenate([bias[rows, :]] * H, axis=0)
        lg_p = _dot(qs, kp[g].astype(BF16)) + bh[:, 0:LP]
        lg_n = _dot_nt(qs, knp[...].astype(BF16)) + bh[:, LP:LC]
        m = jnp.maximum(jnp.max(lg_p, axis=-1, keepdims=True), jnp.max(lg_n, axis=-1, keepdims=True))
        ep = jnp.exp(lg_p - m)
        en = jnp.exp(lg_n - m)
        den = jnp.sum(ep, axis=-1, keepdims=True) + jnp.sum(en, axis=-1, keepdims=True)
        o = (_dot_nt(ep.astype(BF16), vp[g].astype(BF16)) + _dot(en.astype(BF16), vnp[...].astype(BF16))) / den
        for h in range(H):
            o_ref[rows, h * dh:(h + 1) * dh] = o[h * T:(h + 1) * T, :].astype(o_ref.dtype)
        return c

    lax.fori_loop(0, G, attn_body, 0)


def attn_sample(z, row0, q_block, iq_block, tail_block, tail_w, iw_off, kn, v, ik, caches, page_table, layer,
                gq_tiled, Bs, T, W, dh, G=16):
    n_pages = page_table.shape[1]
    page = caches[0].shape[2]
    caches_t = [c.transpose(0, 1, 3, 2) for c in caches]
    R = G * T
    rb0 = row0 // R
    LP = n_pages * page
    LC = LP + LANES
    n_sel = min(TOPK_MAX, (LP + T) // 4)
    eye = (jnp.arange(W)[:, None] // dh == jnp.arange(W)[None, :] // dh).astype(BF16)
    tri = (jnp.arange(LANES)[:, None] <= jnp.arange(LANES)[None, :]).astype(BF16)
    any_spec = pl.BlockSpec(memory_space=pl.ANY)
    nspec = pl.BlockSpec((R, dh), lambda i, pt: (rb0 + i, 0))
    grid_spec = pltpu.PrefetchScalarGridSpec(
        num_scalar_prefetch=1,
        grid=(Bs // G,),
        in_specs=[pl.BlockSpec((R, W), lambda i, pt: (rb0 + i, q_block)),
                  pl.BlockSpec((R, W), lambda i, pt: (rb0 + i, iq_block)),
                  pl.BlockSpec((R, tail_w), lambda i, pt: (rb0 + i, tail_block)),
                  nspec, nspec, nspec, any_spec, any_spec, any_spec,
                  pl.BlockSpec((1, W), lambda i, pt: (0, 0)),
                  pl.BlockSpec((W, W), lambda i, pt: (0, 0)),
                  pl.BlockSpec((LANES, LANES), lambda i, pt: (0, 0))],
        out_specs=pl.BlockSpec((R, W), lambda i, pt: (i, 0)),
        scratch_shapes=[pltpu.VMEM((G, dh, LP), F32), pltpu.VMEM((G, dh, LP), F32), pltpu.VMEM((G, dh, LP), F32),
                        pltpu.VMEM((LANES, dh), F32), pltpu.VMEM((LANES, dh), F32), pltpu.VMEM((LANES, dh), F32),
                        pltpu.VMEM((R, W), F32), pltpu.VMEM((R, W), F32), pltpu.VMEM((R, LANES), F32),
                        pltpu.VMEM((R, LC), F32), pltpu.VMEM((R, LC), F32),
                        pltpu.SemaphoreType.DMA((3, G))],
    )
    return pl.pallas_call(
        functools.partial(_attn_sample_kernel, G=G, T=T, dh=dh, layer=layer, n_pages=n_pages, hp=page,
                          iw_off=iw_off, n_sel=n_sel),
        grid_spec=grid_spec,
        out_shape=jax.ShapeDtypeStruct((Bs * T, W), BF16),
        compiler_params=_cparams(("arbitrary",)),
        name="attn_sample",
    )(page_table, z, z, z, kn, v, ik, *caches_t, gq_tiled, eye, tri)


def _block_diag(blocks):
    G, a, b = blocks.shape
    eye = jnp.eye(G, dtype=blocks.dtype)
    return (eye[:, None, :, None] * blocks[:, :, None, :]).reshape(G * a, G * b)


def _lane_blocks(a):
    rows, n = a.shape
    return a.reshape(rows, n // LANES, LANES).transpose(1, 0, 2)


def kernel(x_prompt, x_sample, cache_attn_k, cache_attn_v, cache_idx_k, cache_mem_k, cache_mem_v, state_pool,
           state_conv, state_ssm_re, state_ssm_im, page_table, mem_prompt, norm_g, ffn_in, ffn_out, w_in,
           q_norm_g, k_norm_g, pool_mix, pool_scale, conv_w, ssm_a_re, ssm_a_im, ssm_log_step, ssm_b_re,
           ssm_b_im, ssm_c_re, ssm_c_im, ssm_d, ssm_glu_w, ssm_glu_b, w_branch, w_gate, b_gate, w_o,
           mem_norm_g, w_xq, w_xk, w_xv, xq_norm_g, xk_norm_g, w_xo):
    B, T, D = x_prompt.shape
    Bs, Ts, _ = x_sample.shape
    depth = norm_g.shape[0]
    Mp, Ms = B * T, Bs * Ts
    W = pool_scale.shape[1]
    dh = k_norm_g.shape[1]
    d_ff = ffn_out.shape[2]
    n_mem = mem_prompt.shape[1]
    Wx = w_xq.shape[2]
    SG, SN = ssm_a_re.shape[1:]
    NB = SG * SN // LANES
    past_len = page_table.shape[1] * cache_attn_k.shape[2]
    assert cache_idx_k.shape[-1] == dh and W == N_HEADS_C * dh == IDX_HEADS * dh
    TM = 1024

    o_k = 5 * W
    o_iq = o_k + 2 * dh
    o_ik = o_iq + W
    o_xs = o_ik + dh + IDX_HEADS
    tail_w = 2 * LANES
    Q_BLK, IQ_BLK, XS_BLK = 4, 5, 6
    TAIL_BLK = 7 * W // tail_w
    IW_OFF = 3 * dh
    n_z = 7 * W + tail_w

    h = jnp.concatenate([x_prompt.reshape(Mp, D), x_sample.reshape(Ms, D)], axis=0)
    caches = (cache_attn_k, cache_attn_v, cache_idx_k)
    mem_rows = mem_prompt.reshape(B * n_mem, D)
    outs = [[] for _ in range(16)]

    def ffn(h, l, i, g):
        act = matmul(h, [(ffn_in, (l, i), 0), (ffn_in, (l, i), d_ff // 512)], n_out=d_ff, tn=512, tm=TM,
                     mode="swiglu", out_dtype=BF16, norm_g=g, name="ffn_in")
        return matmul(act, [(ffn_out, (l, i), 0)], n_out=D, tn=512, tm=TM, mode="residual", aux=h, scale=0.5,
                      name="ffn_out")

    for l in range(depth):
        h = ffn(h, l, 0, norm_g[l, 0])

        wl = w_in[l]
        w_in2 = jnp.concatenate([wl[:, 0:o_k], wl[:, o_iq:o_iq + W], wl[:, o_xs:o_xs + W], wl[:, o_k:o_k + 2 * dh],
                                 wl[:, o_ik:o_ik + dh + IDX_HEADS],
                                 jnp.zeros((D, tail_w - 3 * dh - IDX_HEADS), F32)], axis=1)
        z = matmul(h, [(w_in2, (), 0)], n_out=n_z, tn=768, tm=TM, norm_g=norm_g[l, 1], name="in_proj")

        wmix = pool_mix[l].astype(BF16)
        ya_p, yb_p, npool_p, nconv_p = pool_conv(z, 0, B, T, 1, 512, 0, None, None, wmix, pool_scale[l], conv_w[l], W)
        ya_s, yb_s, npool_s, nconv_s = pool_conv(z, Mp, Bs, Ts, 16, Ts, past_len, state_pool[l], state_conv[l],
                                                 wmix, pool_scale[l], conv_w[l], W)

        kn, vv, ik = kv_prep(z, TAIL_BLK, tail_w, k_norm_g[l], dh)
        gq_tiled = jnp.tile(q_norm_g[l], N_HEADS_C).reshape(1, W)
        yc_p = attn_prompt(z, Q_BLK, IQ_BLK, TAIL_BLK, tail_w, IW_OFF, kn[:Mp].reshape(B, T, dh),
                           vv[:Mp].reshape(B, T, dh), ik[:Mp].reshape(B, T, dh), gq_tiled, B, T, W, dh)
        yc_s = attn_sample(z, Mp, Q_BLK, IQ_BLK, TAIL_BLK, tail_w, IW_OFF, kn, vv, ik, caches, page_table, l,
                           gq_tiled, Bs, Ts, W, dh)

        ab_re, ab_im, co_re, co_im = ssm_params(ssm_a_re[l], ssm_a_im[l], ssm_log_step[l])
        flat = lambda a: a.reshape(1, SG * SN)
        lev_re, lev_im, pw_re, pw_im = ssm_tables(flat(ab_re), flat(ab_im), SUBLANES)
        rep = lambda a: _lane_blocks(jnp.broadcast_to(flat(a), (SUBLANES, SG * SN)))
        consts = (
            _lane_blocks(_block_diag(ssm_b_re[l].transpose(0, 2, 1))).astype(BF16),
            _lane_blocks(_block_diag(ssm_b_im[l].transpose(0, 2, 1))).astype(BF16),
            _block_diag(ssm_c_re[l].transpose(0, 2, 1)).reshape(NB, LANES, W).astype(BF16),
            _block_diag(ssm_c_im[l].transpose(0, 2, 1)).reshape(NB, LANES, W).astype(BF16),
            rep(co_re), rep(co_im), _lane_blocks(lev_re), _lane_blocks(lev_im),
            _lane_blocks(pw_re), _lane_blocks(pw_im),
            ssm_d[l].reshape(1, W), ssm_glu_w[l].astype(BF16), ssm_glu_b[l].reshape(1, W))
        yd_p, sre_p, sim_p = ssm_mixer(z, XS_BLK, 0, B, T, consts, None)
        h0 = (state_ssm_re[l].reshape(Bs, NB, LANES), state_ssm_im[l].reshape(Bs, NB, LANES))
        yd_s, sre_s, sim_s = ssm_mixer(z, XS_BLK, Mp, Bs, Ts, consts, h0)

        cat = lambda a, b: jnp.concatenate([a, b], axis=0)
        merged = gated_merge(h, norm_g[l, 1], [cat(ya_p, ya_s), cat(yb_p, yb_s), cat(yc_p, yc_s), cat(yd_p, yd_s)],
                             w_gate, b_gate, w_branch, l)
        h = matmul(merged, [(w_o, (l,), 0)], n_out=D, tn=512, tm=TM, mode="residual", aux=h, scale=1.0, name="w_o")

        mn = rmsnorm_rows(mem_rows, mem_norm_g[l])
        mk = matmul(mn, [(w_xk, (l,), 0)], n_out=Wx, tn=Wx, tm=B * n_mem, mode="headnorm",
                    aux=jnp.tile(xk_norm_g[l], X_HEADS).reshape(1, Wx), group=Wx // X_HEADS, name="mem_k")
        mv = matmul(mn, [(w_xv, (l,), 0)], n_out=Wx, tn=Wx, tm=B * n_mem, name="mem_v")
        qx = matmul(h, [(w_xq, (l,), 0)], n_out=Wx, tn=Wx, tm=TM, norm_g=norm_g[l, 2], name="w_xq")
        xa_p = cross_attention(qx, 0, B, T, 1, 512, mk.reshape(1, B, n_mem, Wx), mv.reshape(1, B, n_mem, Wx), 0,
                               xq_norm_g[l])
        xa_s = cross_attention(qx, Mp, Bs, Ts, 8, Ts, cache_mem_k.reshape(depth, Bs, n_mem * X_HEADS, Wx // X_HEADS),
                               cache_mem_v.reshape(depth, Bs, n_mem * X_HEADS, Wx // X_HEADS), l, xq_norm_g[l])
        h = matmul(cat(xa_p, xa_s), [(w_xo, (l,), 0)], n_out=D, tn=512, tm=TM, mode="residual", aux=h, scale=1.0,
                   name="w_xo")

        h = ffn(h, l, 1, norm_g[l, 3])

        xh = Wx // X_HEADS
        layer_out = (kn[:Mp].reshape(B, T, dh), vv[:Mp].reshape(B, T, dh), ik[:Mp].reshape(B, T, dh),
                     mk.reshape(B, n_mem, X_HEADS, xh), mv.reshape(B, n_mem, X_HEADS, xh), npool_p, nconv_p,
                     sre_p.reshape(B, SG, SN), sim_p.reshape(B, SG, SN),
                     kn[Mp:].reshape(Bs, Ts, dh), vv[Mp:].reshape(Bs, Ts, dh), ik[Mp:].reshape(Bs, Ts, dh),
                     npool_s, nconv_s, sre_s.reshape(Bs, SG, SN), sim_s.reshape(Bs, SG, SN))
        for acc, val in zip(outs, layer_out):
            acc.append(val)

    return (h[:Mp].reshape(B, T, D), h[Mp:].reshape(Bs, Ts, D)) + tuple(jnp.stack(o) for o in outs)
```

```python
import functools
import math

import jax
import jax.numpy as jnp
from jax import lax
from jax.experimental import pallas as pl
from jax.experimental.pallas import tpu as pltpu

F32, BF16, I32 = jnp.float32, jnp.bfloat16, jnp.int32
EPS = 1e-6
NEG_INF = float("-inf")

V7X_VMEM_BYTES = 64 * 1024 * 1024
VMEM_LIMIT = V7X_VMEM_BYTES - 8 * 1024 * 1024
LANES = 128
SUBLANES = 8

POOL_WINDOWS = (2, 4, 8, 16)
POOL_HIST = 16
CONV_WIDTH = 3
CONV_HIST = 8
N_HEADS_C = 8
IDX_HEADS = 8
TOPK_MAX = 256
X_HEADS = 4
SSM_CHUNK = 256
SSM_SEQ_ROWS = 128


def _cparams(sem, vmem=VMEM_LIMIT):
    return pltpu.CompilerParams(dimension_semantics=sem, vmem_limit_bytes=vmem)


def _dot(a, b):
    return jnp.dot(a, b, preferred_element_type=F32)


def _dot_nt(a, b):
    return lax.dot_general(a, b, (((1,), (1,)), ((), ())), preferred_element_type=F32)


def _rmsnorm_kernel(x_ref, g_ref, o_ref):
    x = x_ref[...]
    ms = jnp.mean(x * x, axis=-1, keepdims=True)
    o_ref[...] = (x * lax.rsqrt(ms + EPS) * g_ref[...]).astype(o_ref.dtype)


def rmsnorm_rows(x, g, tm=512):
    M, D = x.shape
    return pl.pallas_call(
        _rmsnorm_kernel,
        grid=(M // tm,),
        in_specs=[pl.BlockSpec((tm, D), lambda i: (i, 0)), pl.BlockSpec((1, D), lambda i: (0, 0))],
        out_specs=pl.BlockSpec((tm, D), lambda i: (i, 0)),
        out_shape=jax.ShapeDtypeStruct((M, D), BF16),
        compiler_params=_cparams(("arbitrary",)),
        name="rmsnorm",
    )(x, g.reshape(1, D))


def _rms_rows_bf16(x, g):
    ms = jnp.mean(x * x, axis=-1, keepdims=True)
    return (x * lax.rsqrt(ms + EPS) * g).astype(BF16)


def _pick_rows(ref_a, ref_b, rows, n_a_tiles):
    if ref_b is None:
        return ref_a[rows, :]
    return jnp.where(pl.program_id(1) >= n_a_tiles, ref_b[rows, :], ref_a[rows, :])


def _split_row_specs(xa, xb, tm):
    na = xa.shape[0] // tm
    spec_a = pl.BlockSpec((tm, xa.shape[1]), lambda j, i: (jnp.minimum(i, na - 1), 0))
    spec_b = pl.BlockSpec((tm, xb.shape[1]), lambda j, i: (jnp.maximum(i - na, 0), 0),
                          pipeline_mode=pl.Buffered(1) if xb.shape[0] == tm else None)
    return spec_a, spec_b, na


def _mm_kernel(*refs, nw, mode, scale, sub, group, norm, n_a_tiles):
    x_ref = refs[0]
    xb_ref = None
    if n_a_tiles is not None:
        xb_ref = refs[1]
        refs = refs[1:]
    w_refs = refs[1:1 + nw]
    p = 1 + nw
    if norm:
        g_ref = refs[p]
        p += 1
    aux_ref = None
    if mode in ("bias_sigmoid", "residual", "headnorm"):
        aux_ref = refs[p]
        p += 1
    o_ref = refs[p]
    scr = refs[p + 1:p + 1 + nw]

    @pl.when(pl.program_id(1) == 0)
    def _():
        for w_ref, s in zip(w_refs, scr):
            s[...] = w_ref[...].astype(BF16)

    tm = x_ref.shape[0]

    def body(r, carry):
        rows = pl.ds(pl.multiple_of(r * sub, sub), sub)
        x = _pick_rows(x_ref, xb_ref, rows, n_a_tiles)
        if norm:
            x = _rms_rows_bf16(x, g_ref[...])
        acc = [_dot(x, s[...]) for s in scr]
        if mode == "swiglu":
            a, b = acc
            y = (a * jax.nn.sigmoid(a)) * b
        elif mode == "bias_sigmoid":
            y = jax.nn.sigmoid(acc[0] + aux_ref[...])
        elif mode == "residual":
            y = aux_ref[rows, :] + scale * acc[0]
        elif mode == "headnorm":
            a = acc[0]
            parts = []
            for h in range(a.shape[1] // group):
                ah = a[:, h * group:(h + 1) * group]
                ms = jnp.mean(ah * ah, axis=-1, keepdims=True)
                parts.append(ah * lax.rsqrt(ms + EPS))
            y = jnp.concatenate(parts, axis=1) * aux_ref[...]
        else:
            y = acc[0]
        o_ref[rows, :] = y.astype(o_ref.dtype)
        return carry

    lax.fori_loop(0, tm // sub, body, 0, unroll=True)


def matmul(x, weights, *, n_out, tn, tm, mode="plain", aux=None, scale=1.0, out_dtype=F32, group=LANES,
           norm_g=None, name="matmul"):
    nw = len(weights)
    sub = min(tm, 256)
    n_a_tiles = None
    if isinstance(x, tuple):
        xa, xb = x
        spec_a, spec_b, n_a_tiles = _split_row_specs(xa, xb, tm)
        M, K = xa.shape[0] + xb.shape[0], xa.shape[1]
        in_specs = [spec_a, spec_b]
        args = [xa, xb]
    else:
        M, K = x.shape
        in_specs = [pl.BlockSpec((tm, K), lambda j, i: (i, 0))]
        args = [x]
    for arr, lead, coff in weights:
        nl = len(lead)
        in_specs.append(pl.BlockSpec((None,) * nl + (K, tn),
                                     functools.partial(lambda j, i, lead, coff: (*lead, 0, coff + j), lead=lead, coff=coff)))
        args.append(arr)
    if norm_g is not None:
        in_specs.append(pl.BlockSpec((1, K), lambda j, i: (0, 0)))
        args.append(norm_g.reshape(1, K))
    if mode in ("bias_sigmoid", "headnorm"):
        in_specs.append(pl.BlockSpec((1, tn), lambda j, i: (0, j)))
        args.append(aux)
    elif mode == "residual":
        in_specs.append(pl.BlockSpec((tm, tn), lambda j, i: (i, j)))
        args.append(aux)
    return pl.pallas_call(
        functools.partial(_mm_kernel, nw=nw, mode=mode, scale=scale, sub=sub, group=group, norm=norm_g is not None,
                          n_a_tiles=n_a_tiles),
        grid=(n_out // tn, M // tm),
        in_specs=in_specs,
        out_specs=pl.BlockSpec((tm, tn), lambda j, i: (i, j)),
        out_shape=jax.ShapeDtypeStruct((M, n_out), out_dtype),
        scratch_shapes=[pltpu.VMEM((K, tn), BF16) for _ in range(nw)],
        compiler_params=_cparams(("arbitrary", "arbitrary")),
        name=name,
    )(*args)


def _merge_kernel(*refs, nb, n_a_tiles):
    h_ref, g_ref = refs[0:2]
    br = refs[2:2 + nb]
    brb = refs[2 + nb:2 + 2 * nb]
    refs = refs[nb:]
    wg = refs[2 + nb:2 + 2 * nb]
    bg = refs[2 + 2 * nb:2 + 3 * nb]
    wb = refs[2 + 3 * nb:2 + 4 * nb]
    o_ref = refs[2 + 4 * nb]
    sg = refs[3 + 4 * nb:3 + 5 * nb]
    sb = refs[3 + 5 * nb:3 + 6 * nb]

    @pl.when(pl.program_id(1) == 0)
    def _():
        for w_ref, s in zip(wg + wb, sg + sb):
            s[...] = w_ref[...].astype(BF16)

    tm = o_ref.shape[0]
    sub = min(tm, 256)

    def body(r, carry):
        rows = pl.ds(pl.multiple_of(r * sub, sub), sub)
        u = _rms_rows_bf16(h_ref[rows, :], g_ref[...])
        acc = None
        for k in range(nb):
            gate = jax.nn.sigmoid(_dot(u, sg[k][...]) + bg[k][...])
            t = gate * _dot(_pick_rows(br[k], brb[k], rows, n_a_tiles), sb[k][...])
            acc = t if acc is None else acc + t
        o_ref[rows, :] = acc.astype(o_ref.dtype)
        return carry

    lax.fori_loop(0, tm // sub, body, 0, unroll=True)


def gated_merge(h, norm_g, branches, w_gate, b_gate, w_branch, layer, *, tn=256, tm=1024):
    nb = len(branches)
    M = h.shape[0]
    W = branches[0][0].shape[1]
    D = w_branch.shape[-1]
    nj = D // tn
    bias = b_gate.reshape(b_gate.shape[0], 1, nb * D)

    def per_branch(shape):
        return [pl.BlockSpec(shape, functools.partial(lambda j, i, k: (layer, 0, k * nj + j), k=k)) for k in range(nb)]

    row_specs = [_split_row_specs(a, b, tm) for a, b in branches]
    n_a_tiles = row_specs[0][2]
    in_specs = [pl.BlockSpec((tm, D), lambda j, i: (i, 0)), pl.BlockSpec((1, D), lambda j, i: (0, 0))]
    in_specs += [s[0] for s in row_specs] + [s[1] for s in row_specs]
    in_specs += per_branch((None, D, tn)) + per_branch((None, 1, tn))
    in_specs += [pl.BlockSpec((None, None, W, tn), functools.partial(lambda j, i, k: (layer, k, 0, j), k=k))
                 for k in range(nb)]
    return pl.pallas_call(
        functools.partial(_merge_kernel, nb=nb, n_a_tiles=n_a_tiles),
        grid=(nj, M // tm),
        in_specs=in_specs,
        out_specs=pl.BlockSpec((tm, tn), lambda j, i: (i, j)),
        out_shape=jax.ShapeDtypeStruct((M, D), BF16),
        scratch_shapes=[pltpu.VMEM((D, tn), BF16) for _ in range(nb)] + [pltpu.VMEM((W, tn), BF16) for _ in range(nb)],
        compiler_params=_cparams(("arbitrary", "arbitrary")),
        name="gated_merge",
    )(h, norm_g.reshape(1, D), *[a for a, _ in branches], *[b for _, b in branches],
      *([w_gate] * nb), *([bias] * nb), *([w_branch] * nb))


def _kvprep_kernel(t_ref, g_ref, k_ref, v_ref, ik_ref, *, dh):
    t = t_ref[...]
    k = t[:, 0:dh]
    ms = jnp.mean(k * k, axis=-1, keepdims=True)
    k_ref[...] = k * lax.rsqrt(ms + EPS) * g_ref[...]
    v_ref[...] = t[:, dh:2 * dh]
    ik_ref[...] = t[:, 2 * dh:3 * dh]


def kv_prep(z, tail_block, tail_w, gk, dh, tm=1024):
    M = z.shape[0]
    out = jax.ShapeDtypeStruct((M, dh), F32)
    return pl.pallas_call(
        functools.partial(_kvprep_kernel, dh=dh),
        grid=(M // tm,),
        in_specs=[pl.BlockSpec((tm, tail_w), lambda i: (i, tail_block)), pl.BlockSpec((1, dh), lambda i: (0, 0))],
        out_specs=[pl.BlockSpec((tm, dh), lambda i: (i, 0))] * 3,
        out_shape=[out, out, out],
        compiler_params=_cparams(("arbitrary",)),
        name="kv_prep",
    )(z, gk.reshape(1, dh))


def _poolconv_kernel(*refs, G, T, W, pos0, has_state):
    xp_ref, xc_ref, bg_ref, cg_ref = refs[0:4]
    p = 4
    if has_state:
        pbuf_ref, cbuf_ref = refs[4:6]
        p = 6
    wmix_ref, pscale_ref, convw_ref = refs[p:p + 3]
    ya_ref, yb_ref, npool_ref, nconv_ref = refs[p + 3:p + 7]
    fullp, fullc = refs[p + 7:p + 9]
    PH, CH = POOL_HIST, CONV_HIST
    nh = CONV_WIDTH - 1
    gw = W // len(POOL_WINDOWS)
    c = pl.program_id(1)

    @pl.when(c == 0)
    def _():
        if has_state:
            fullp[:, 1:PH, :] = pbuf_ref[...]
            fullc[:, CH - nh:CH, :] = cbuf_ref[...]
        else:
            fullp[:, 0:PH, :] = jnp.zeros((G, PH, W), F32)
            fullc[:, 0:CH, :] = jnp.zeros((G, CH, W), F32)

    @pl.when(c > 0)
    def _():
        fullp[:, 0:PH, :] = fullp[:, T:T + PH, :]
        fullc[:, 0:CH, :] = fullc[:, T:T + CH, :]

    fullp[:, PH:PH + T, :] = xp_ref[...].reshape(G, T, W)
    t_idx = lax.broadcasted_iota(I32, (1, T, 1), 1) + (c * T + (pos0 + 1))
    for gi, w in enumerate(POOL_WINDOWS):
        cols = slice(gi * gw, (gi + 1) * gw)
        acc = fullp[:, PH:PH + T, cols]
        for j in range(1, w):
            acc = acc + fullp[:, PH - j:PH - j + T, cols]
        cnt = jnp.minimum(t_idx, w).astype(F32)
        d = acc / cnt - fullp[:, PH:PH + T, cols]
        y = _dot(d.reshape(G * T, gw).astype(BF16), wmix_ref[gi]) * pscale_ref[:, cols]
        ya_ref[:, cols] = y.astype(ya_ref.dtype)
    npool_ref[...] = fullp[:, T + 1:T + PH, :]

    fullc[:, CH:CH + T, :] = (cg_ref[...] * xc_ref[...]).reshape(G, T, W)
    y = None
    for j in range(CONV_WIDTH):
        wj = convw_ref[j:j + 1, :].reshape(1, 1, W)
        term = wj * fullc[:, CH - nh + j:CH - nh + j + T, :]
        y = term if y is None else y + term
    yb_ref[...] = (bg_ref[...] * y.reshape(G * T, W)).astype(yb_ref.dtype)
    nconv_ref[...] = fullc[:, CH + T - nh:CH + T, :]


def pool_conv(z, row0, Bt, T, G, TC, pos0, pool_buf, conv_buf, wmix_bf16, pool_scale, conv_w, W):
    has_state = pool_buf is not None
    assert G == 1 or TC == T
    R = G * TC
    nc = T // TC
    rb0 = row0 // R
    nh = CONV_WIDTH - 1

    def zspec(cb):
        return pl.BlockSpec((R, W), functools.partial(lambda i, c, cb: (rb0 + i * nc + c, cb), cb=cb))

    in_specs = [zspec(0), zspec(1), zspec(2), zspec(3)]
    args = [z, z, z, z]
    if has_state:
        in_specs += [pl.BlockSpec((G, POOL_HIST - 1, W), lambda i, c: (i, 0, 0)),
                     pl.BlockSpec((G, nh, W), lambda i, c: (i, 0, 0))]
        args += [pool_buf, conv_buf]
    nwin = len(POOL_WINDOWS)
    in_specs += [pl.BlockSpec((nwin, W // nwin, W // nwin), lambda i, c: (0, 0, 0)),
                 pl.BlockSpec((1, W), lambda i, c: (0, 0)),
                 pl.BlockSpec((CONV_WIDTH, W), lambda i, c: (0, 0))]
    args += [wmix_bf16, pool_scale.reshape(1, W), conv_w]
    return pl.pallas_call(
        functools.partial(_poolconv_kernel, G=G, T=TC, W=W, pos0=pos0, has_state=has_state),
        grid=(Bt // G, nc),
        in_specs=in_specs,
        out_specs=[pl.BlockSpec((R, W), lambda i, c: (i * nc + c, 0)), pl.BlockSpec((R, W), lambda i, c: (i * nc + c, 0)),
                   pl.BlockSpec((G, POOL_HIST - 1, W), lambda i, c: (i, 0, 0)),
                   pl.BlockSpec((G, nh, W), lambda i, c: (i, 0, 0))],
        out_shape=[jax.ShapeDtypeStruct((Bt * T, W), BF16), jax.ShapeDtypeStruct((Bt * T, W), BF16),
                   jax.ShapeDtypeStruct((Bt, POOL_HIST - 1, W), F32), jax.ShapeDtypeStruct((Bt, nh, W), F32)],
        scratch_shapes=[pltpu.VMEM((G, POOL_HIST + TC, W), F32), pltpu.VMEM((G, CONV_HIST + TC, W), F32)],
        compiler_params=_cparams(("arbitrary", "arbitrary")),
        name="pool_conv",
    )(*args)


def _ssm_params_kernel(are_ref, aim_ref, ls_ref, abre_ref, abim_ref, core_ref, coim_ref):
    a_re = are_ref[...]
    a_im = aim_ref[...]
    step = jnp.exp(ls_ref[...])
    decay = jnp.exp(step * a_re)
    ab_re = decay * jnp.cos(step * a_im)
    ab_im = decay * jnp.sin(step * a_im)
    den = a_re * a_re + a_im * a_im
    nr = ab_re - 1.0
    abre_ref[...] = ab_re
    abim_ref[...] = ab_im
    core_ref[...] = (nr * a_re + ab_im * a_im) / den
    coim_ref[...] = (ab_im * a_re - nr * a_im) / den


def ssm_params(a_re, a_im, log_step):
    G, N = a_re.shape
    out = jax.ShapeDtypeStruct((G, N), F32)
    return pl.pallas_call(_ssm_params_kernel, out_shape=[out, out, out, out], name="ssm_params")(
        a_re, a_im, log_step.reshape(G, 1))


def _scan_levels(C):
    return [1 << k for k in range(int(math.log2(C)))]


def _ssm_tables_kernel(abre_ref, abim_ref, lre_ref, lim_ref, pre_ref, pim_ref, *, C):
    ar = abre_ref[...]
    ai = abim_ref[...]
    N = ar.shape[1]
    row = lax.broadcasted_iota(I32, (C, N), 0)
    hr = jnp.where(row == 0, ar, 0.0)
    hi = jnp.where(row == 0, ai, 0.0)
    lre_ref[...] = jnp.zeros(lre_ref.shape, F32)
    lim_ref[...] = jnp.zeros(lim_ref.shape, F32)
    for k, s in enumerate(_scan_levels(C)):
        lre_ref[k:k + 1, :] = ar
        lim_ref[k:k + 1, :] = ai
        sr = jnp.where(row >= s, pltpu.roll(hr, s, 0), 0.0)
        si = jnp.where(row >= s, pltpu.roll(hi, s, 0), 0.0)
        hr, hi = hr + ar * sr - ai * si, hi + ar * si + ai * sr
        ar, ai = ar * ar - ai * ai, 2.0 * ar * ai
    pre_ref[...] = hr
    pim_ref[...] = hi


def ssm_tables(ab_re, ab_im, C):
    N = ab_re.shape[1]
    nlev = len(_scan_levels(C))
    lev = jax.ShapeDtypeStruct((SUBLANES * ((nlev + SUBLANES - 1) // SUBLANES), N), F32)
    pw = jax.ShapeDtypeStruct((C, N), F32)
    return pl.pallas_call(functools.partial(_ssm_tables_kernel, C=C), out_shape=[lev, lev, pw, pw],
                          name="ssm_tables")(ab_re, ab_im)


def _gelu_tanh(x):
    return 0.5 * x * (1.0 + jnp.tanh(math.sqrt(2.0 / math.pi) * (x + 0.044715 * (x * x * x))))


def _ssm_kernel(*refs, R, T, NB, chained):
    (xs_ref, bre_ref, bim_ref, cre_ref, cim_ref, core_ref, coim_ref, lre_ref, lim_ref) = refs[0:9]
    p = 9
    if chained:
        pre_ref, pim_ref = refs[p:p + 2]
        p += 2
    else:
        h0r_ref, h0i_ref = refs[p:p + 2]
        p += 2
    d_ref, gw_ref, gb_ref = refs[p:p + 3]
    yd_ref, sre_ref, sim_ref = refs[p + 3:p + 6]
    p += 6
    if chained:
        car_ref, cai_ref = refs[p:p + 2]
    G = R // T
    xs = xs_ref[...]
    xb = xs.astype(BF16)
    tpos = lax.broadcasted_iota(I32, (R, LANES), 0) % SUBLANES
    levels = _scan_levels(SUBLANES)

    if chained:
        c = pl.program_id(1)

        @pl.when(c == 0)
        def _():
            car_ref[...] = jnp.zeros(car_ref.shape, F32)
            cai_ref[...] = jnp.zeros(cai_ref.shape, F32)

    y = d_ref[...] * xs
    for cb in range(NB):
        pr = _dot(xb, bre_ref[cb])
        pi = _dot(xb, bim_ref[cb])
        cor = core_ref[cb][0:1, :]
        coi = coim_ref[cb][0:1, :]
        hr = cor * pr - coi * pi
        hi = cor * pi + coi * pr
        if not chained:
            ar = lre_ref[cb][0:1, :]
            ai = lim_ref[cb][0:1, :]
            h0r = jnp.broadcast_to(h0r_ref[:, cb:cb + 1, :], (G, T, LANES)).reshape(R, LANES)
            h0i = jnp.broadcast_to(h0i_ref[:, cb:cb + 1, :], (G, T, LANES)).reshape(R, LANES)
            first = tpos == 0
            hr = hr + jnp.where(first, ar * h0r - ai * h0i, 0.0)
            hi = hi + jnp.where(first, ar * h0i + ai * h0r, 0.0)
        for k, s in enumerate(levels):
            ar = lre_ref[cb][k:k + 1, :]
            ai = lim_ref[cb][k:k + 1, :]
            sr = jnp.where(tpos >= s, pltpu.roll(hr, s, 0), 0.0)
            si = jnp.where(tpos >= s, pltpu.roll(hi, s, 0), 0.0)
            hr, hi = hr + ar * sr - ai * si, hi + ar * si + ai * sr
        if chained:
            cr = car_ref[cb:cb + 1, :]
            ci = cai_ref[cb:cb + 1, :]
            pwr = pre_ref[cb]
            pwi = pim_ref[cb]
            grs, gis = [], []
            for v in range(R // SUBLANES):
                rows = slice(v * SUBLANES, (v + 1) * SUBLANES)
                cbr = jnp.broadcast_to(cr, (SUBLANES, LANES))
                cbi = jnp.broadcast_to(ci, (SUBLANES, LANES))
                gr = hr[rows] + pwr * cbr - pwi * cbi
                gi = hi[rows] + pwr * cbi + pwi * cbr
                cr = gr[SUBLANES - 1:SUBLANES, :]
                ci = gi[SUBLANES - 1:SUBLANES, :]
                grs.append(gr)
                gis.append(gi)
            hr = jnp.concatenate(grs, axis=0)
            hi = jnp.concatenate(gis, axis=0)
            car_ref[cb:cb + 1, :] = cr
            cai_ref[cb:cb + 1, :] = ci
        else:
            sre_ref[:, cb:cb + 1, :] = hr.reshape(G, T, LANES)[:, T - 1:T, :]
            sim_ref[:, cb:cb + 1, :] = hi.reshape(G, T, LANES)[:, T - 1:T, :]
        y = y + _dot(hr.astype(BF16), cre_ref[cb]) - _dot(hi.astype(BF16), cim_ref[cb])

    z = _gelu_tanh(y)
    out = z * jax.nn.sigmoid(_dot(z.astype(BF16), gw_ref[...]) + gb_ref[...])
    yd_ref[...] = out.astype(yd_ref.dtype)

    if chained:
        @pl.when(c == pl.num_programs(1) - 1)
        def _():
            sre_ref[0] = car_ref[...]
            sim_ref[0] = cai_ref[...]


def ssm_mixer(z, xs_block, row0, Bt, T, consts, h0):
    (bre3, bim3, cre3, cim3, core3, coim3, lre3, lim3, pre3, pim3, dvec, gw, gb) = consts
    NB = bre3.shape[0]
    W = dvec.shape[1]
    chained = h0 is None
    assert chained or T == SUBLANES
    R = SSM_CHUNK if chained else SSM_SEQ_ROWS
    rb0 = row0 // R

    def full(a):
        nd = a.ndim
        return pl.BlockSpec(a.shape, lambda *_: (0,) * nd)

    if chained:
        nchunk = T // R
        grid = (Bt, nchunk)
        xs_spec = pl.BlockSpec((R, W), lambda b, c: (rb0 + b * nchunk + c, xs_block))
        st_args, st_specs = [pre3, pim3], [full(pre3), full(pim3)]
        yd_spec = pl.BlockSpec((R, W), lambda b, c: (b * nchunk + c, 0))
        s_spec = pl.BlockSpec((1, NB, LANES), lambda b, c: (b, 0, 0))
        scratch = [pltpu.VMEM((NB, LANES), F32), pltpu.VMEM((NB, LANES), F32)]
        sem = ("arbitrary", "arbitrary")
        Tk = R
    else:
        G = R // T
        grid = (Bt // G,)
        xs_spec = pl.BlockSpec((R, W), lambda i: (rb0 + i, xs_block))
        st_args = list(h0)
        st_specs = [pl.BlockSpec((G, NB, LANES), lambda i: (i, 0, 0))] * 2
        yd_spec = pl.BlockSpec((R, W), lambda i: (i, 0))
        s_spec = pl.BlockSpec((G, NB, LANES), lambda i: (i, 0, 0))
        scratch = []
        sem = ("arbitrary",)
        Tk = T
    shared = [bre3, bim3, cre3, cim3, core3, coim3, lre3, lim3]
    tailc = [dvec, gw, gb]
    s_shape = jax.ShapeDtypeStruct((Bt, NB, LANES), F32)
    return pl.pallas_call(
        functools.partial(_ssm_kernel, R=R, T=Tk, NB=NB, chained=chained),
        grid=grid,
        in_specs=[xs_spec] + [full(a) for a in shared] + st_specs + [full(a) for a in tailc],
        out_specs=[yd_spec, s_spec, s_spec],
        out_shape=[jax.ShapeDtypeStruct((Bt * T, W), BF16), s_shape, s_shape],
        scratch_shapes=scratch,
        compiler_params=_cparams(sem),
        name="ssm_chained" if chained else "ssm_stateful",
    )(z, *shared, *st_args, *tailc)


def _sort_keys(score):
    b = pltpu.bitcast(score, I32)
    key = jnp.where(b < 0, b ^ jnp.int32(0x7FFFFFFF), b)
    return jnp.where(key == -1, 0, key)


def _kth_largest_key(key, k):
    rows = key.shape[0]

    def count_ge(cand):
        return jnp.sum(jnp.where(key >= cand, 1.0, 0.0), axis=-1, keepdims=True)

    def body(i, t):
        step = jnp.int32(1) << (30 - 2 * i)
        c1 = t + step
        c2 = c1 + step
        c3 = c2 + step
        kf = float(k)
        t = jnp.where(count_ge(c1) >= kf, c1, t)
        t = jnp.where(count_ge(c2) >= kf, c2, t)
        return jnp.where(count_ge(c3) >= kf, c3, t)

    return lax.fori_loop(0, 16, body, jnp.full((rows, 1), -2 ** 31, I32))


def _blocked_prefix(eq, tri, offset):
    outs = []
    run = offset
    for j in range(eq.shape[1] // LANES):
        blk = eq[:, j * LANES:(j + 1) * LANES].astype(BF16)
        pj = _dot(blk, tri) + run
        outs.append(pj)
        run = pj[:, LANES - 1:LANES]
    return outs, run


def _attn_prompt_kernel(q_ref, iq_ref, tail_ref, k_ref, v_ref, ik_ref, gq_ref, e_ref, tri_ref, o_ref,
                        kb, vb, ikb, bias, *, TQ, L, dh, iw_off, n_sel, n_buckets):
    qb = pl.program_id(1)

    @pl.when(qb == 0)
    def _():
        kb[...] = k_ref[0].astype(BF16)
        vb[...] = v_ref[0].astype(BF16)
        ikb[...] = ik_ref[0].astype(BF16)

    scale = dh ** -0.5
    q = q_ref[...]
    q2 = q * q
    hi = q2.astype(BF16)
    lo = (q2 - hi.astype(F32)).astype(BF16)
    ss = _dot(hi, e_ref[...]) + _dot(lo, e_ref[...])
    qn = (q * lax.rsqrt(ss * (1.0 / dh) + EPS) * gq_ref[...] * scale).astype(BF16)
    iqs = (iq_ref[...] * scale).astype(BF16)
    iw = tail_ref[:, iw_off:iw_off + IDX_HEADS] * (IDX_HEADS ** -0.5)

    def attend(Lk):
        score = None
        for h in range(IDX_HEADS):
            lg = _dot_nt(iqs[:, h * dh:(h + 1) * dh], ikb[0:Lk, :])
            t = jnp.maximum(lg, 0.0) * iw[:, h:h + 1]
            score = t if score is None else score + t
        col = lax.broadcasted_iota(I32, (TQ, Lk), 1)
        qpos = qb * TQ + lax.broadcasted_iota(I32, (TQ, Lk), 0)
        causal = col <= qpos
        score = jnp.where(causal, score, NEG_INF)

        key = _sort_keys(score)
        thr = _kth_largest_key(key, n_sel)
        gt = key > thr
        eq = key == thr
        need = float(n_sel) - jnp.sum(jnp.where(gt, 1.0, 0.0), axis=-1, keepdims=True)
        pref, _ = _blocked_prefix(jnp.where(eq, 1.0, 0.0), tri_ref[...], jnp.zeros((TQ, 1), F32))
        pref = jnp.concatenate(pref, axis=1)
        sel = (gt | (eq & (pref <= need))) & causal
        bias[:, 0:Lk] = jnp.where(sel, 0.0, NEG_INF)

        for h in range(N_HEADS_C):
            lg = _dot_nt(qn[:, h * dh:(h + 1) * dh], kb[0:Lk, :]) + bias[:, 0:Lk]
            m = jnp.max(lg, axis=-1, keepdims=True)
            e = jnp.exp(lg - m)
            den = jnp.sum(e, axis=-1, keepdims=True)
            o = _dot(e.astype(BF16), vb[0:Lk, :]) / den
            o_ref[:, h * dh:(h + 1) * dh] = o.astype(o_ref.dtype)

    per = (L // TQ) // n_buckets
    for bk in range(n_buckets):
        pl.when(qb // per == bk)(functools.partial(attend, (bk + 1) * per * TQ))


def attn_prompt(z, q_block, iq_block, tail_block, tail_w, iw_off, kn, v, ik, gq_tiled, B, T, W, dh, TQ=256,
                n_buckets=4):
    nq = T // TQ
    n_sel = min(TOPK_MAX, T // 4)
    eye = (jnp.arange(W)[:, None] // dh == jnp.arange(W)[None, :] // dh).astype(BF16)
    tri = (jnp.arange(LANES)[:, None] <= jnp.arange(LANES)[None, :]).astype(BF16)
    kspec = pl.BlockSpec((1, T, dh), lambda b, i: (b, 0, 0))
    return pl.pallas_call(
        functools.partial(_attn_prompt_kernel, TQ=TQ, L=T, dh=dh, iw_off=iw_off, n_sel=n_sel, n_buckets=n_buckets),
        grid=(B, nq),
        in_specs=[pl.BlockSpec((TQ, W), lambda b, i: (b * nq + i, q_block)),
                  pl.BlockSpec((TQ, W), lambda b, i: (b * nq + i, iq_block)),
                  pl.BlockSpec((TQ, tail_w), lambda b, i: (b * nq + i, tail_block)),
                  kspec, kspec, kspec,
                  pl.BlockSpec((1, W), lambda b, i: (0, 0)),
                  pl.BlockSpec((W, W), lambda b, i: (0, 0)),
                  pl.BlockSpec((LANES, LANES), lambda b, i: (0, 0))],
        out_specs=pl.BlockSpec((TQ, W), lambda b, i: (b * nq + i, 0)),
        out_shape=jax.ShapeDtypeStruct((B * T, W), BF16),
        scratch_shapes=[pltpu.VMEM((T, dh), BF16), pltpu.VMEM((T, dh), BF16), pltpu.VMEM((T, dh), BF16),
                        pltpu.VMEM((TQ, T), F32)],
        compiler_params=_cparams(("arbitrary", "arbitrary")),
        name="attn_prompt",
    )(z, z, z, kn, v, ik, gq_tiled, eye, tri)


def _xattn_kernel(q_ref, mk_ref, mv_ref, g_ref, o_ref, *, G, T, dh, scale):
    g = g_ref[...]
    for s in range(G):
        rows = slice(s * T, (s + 1) * T)
        for h in range(X_HEADS):
            cols = slice(h * dh, (h + 1) * dh)
            qh = q_ref[rows, cols]
            ms = jnp.mean(qh * qh, axis=-1, keepdims=True)
            qn = (qh * lax.rsqrt(ms + EPS) * g).astype(BF16)
            lg = _dot_nt(qn, mk_ref[s, :, cols].astype(BF16)) * scale
            m = jnp.max(lg, axis=-1, keepdims=True)
            e = jnp.exp(lg - m)
            den = jnp.sum(e, axis=-1, keepdims=True)
            o = _dot(e.astype(BF16), mv_ref[s, :, cols].astype(BF16)) / den
            o_ref[rows, cols] = o.astype(o_ref.dtype)


def _xattn_rows_kernel(q_ref, mk_ref, mv_ref, g_ref, o_ref, *, G, T, dh, scale):
    g = g_ref[...]
    H = X_HEADS
    n = mk_ref.shape[1]
    own = (lax.broadcasted_iota(I32, (H * T, n), 1) % H) == (lax.broadcasted_iota(I32, (H * T, n), 0) // T)
    for s in range(G):
        rows = slice(s * T, (s + 1) * T)
        parts = []
        for h in range(H):
            qh = q_ref[rows, h * dh:(h + 1) * dh]
            ms = jnp.mean(qh * qh, axis=-1, keepdims=True)
            parts.append(qh * lax.rsqrt(ms + EPS) * g)
        qs = jnp.concatenate(parts, axis=0).astype(BF16)
        lg = jnp.where(own, _dot_nt(qs, mk_ref[s].astype(BF16)) * scale, NEG_INF)
        m = jnp.max(lg, axis=-1, keepdims=True)
        e = jnp.exp(lg - m)
        den = jnp.sum(e, axis=-1, keepdims=True)
        o = _dot(e.astype(BF16), mv_ref[s].astype(BF16)) / den
        for h in range(H):
            o_ref[rows, h * dh:(h + 1) * dh] = o[h * T:(h + 1) * T, :].astype(o_ref.dtype)


def cross_attention(qx, row0, Bt, T, G, TQ, mem_k, mem_v, layer, gq):
    Wx = qx.shape[1]
    dh = Wx // X_HEADS
    n_rows, wm = mem_k.shape[2:]
    assert G == 1 or TQ == T
    R = G * TQ
    nt = T // TQ
    rb0 = row0 // R
    mspec = pl.BlockSpec((None, G, n_rows, wm), lambda i, t: (layer, i, 0, 0))
    body = _xattn_kernel if wm == Wx else _xattn_rows_kernel
    return pl.pallas_call(
        functools.partial(body, G=G, T=TQ, dh=dh, scale=dh ** -0.5),
        grid=(Bt // G, nt),
        in_specs=[pl.BlockSpec((R, Wx), lambda i, t: (rb0 + i * nt + t, 0)), mspec, mspec,
                  pl.BlockSpec((1, dh), lambda i, t: (0, 0))],
        out_specs=pl.BlockSpec((R, Wx), lambda i, t: (i * nt + t, 0)),
        out_shape=jax.ShapeDtypeStruct((Bt * T, Wx), BF16),
        compiler_params=_cparams(("arbitrary", "arbitrary")),
        name="cross_attention",
    )(qx, mem_k, mem_v, gq.reshape(1, dh))


def _attn_sample_kernel(pt_ref, q_ref, iq_ref, tail_ref, kn_ref, vn_ref, ikn_ref, ck_hbm, cv_hbm, cik_hbm,
                        gq_ref, e_ref, tri_ref, o_ref,
                        kp, vp, ikp, knp, vnp, iknp, qs_scr, iqs_scr, iw_scr, score, bias, sems,
                        *, G, T, dh, layer, n_pages, hp, iw_off, n_sel):
    step = pl.program_id(0)
    R = G * T
    H = N_HEADS_C
    LP = n_pages * hp
    LC = LP + LANES

    def page_copies(g, p):
        page = pt_ref[step * G + g, p]
        dst_cols = pl.ds(pl.multiple_of(p * hp, hp), hp)
        return [pltpu.make_async_copy(src.at[layer, page], dst.at[g, :, dst_cols], sems.at[s, g])
                for s, (src, dst) in enumerate(((ck_hbm, kp), (cv_hbm, vp), (cik_hbm, ikp)))]

    def start_all(i, c):
        g = i // n_pages
        for cp in page_copies(g, i - g * n_pages):
            cp.start()
        return c

    def wait_seq(g):
        def wait_page(p, c):
            for cp in page_copies(g, p):
                cp.wait()
            return c
        lax.fori_loop(0, n_pages, wait_page, 0)

    lax.fori_loop(0, G * n_pages, start_all, 0)

    @pl.when(step == 0)
    def _():
        knp[...] = jnp.zeros(knp.shape, F32)
        vnp[...] = jnp.zeros(vnp.shape, F32)
        iknp[...] = jnp.zeros(iknp.shape, F32)

    scale = dh ** -0.5
    q = q_ref[...]
    q2 = q * q
    hi = q2.astype(BF16)
    lo = (q2 - hi.astype(F32)).astype(BF16)
    ss = _dot(hi, e_ref[...]) + _dot(lo, e_ref[...])
    qs_scr[...] = q * lax.rsqrt(ss * (1.0 / dh) + EPS) * gq_ref[...] * scale
    iqs_scr[...] = iq_ref[...] * scale
    iw_blk = (iw_off // LANES) * LANES
    iw_lane = iw_off - iw_blk
    iw_scr[...] = tail_ref[:, iw_blk:iw_blk + LANES]

    new_ok = lax.broadcasted_iota(I32, (R, LANES), 1) <= lax.broadcasted_iota(I32, (R, LANES), 0) % T

    def stack_heads(ref, rows):
        return jnp.concatenate([ref[rows, h * dh:(h + 1) * dh] for h in range(H)], axis=0).astype(BF16)

    def score_body(g, c):
        wait_seq(g)
        rows = pl.ds(pl.multiple_of(g * T, T), T)
        iknp[0:T, :] = ikn_ref[rows, :]
        qs = stack_heads(iqs_scr, rows)
        lg_p = _dot(qs, ikp[g].astype(BF16))
        lg_n = _dot_nt(qs, iknp[...].astype(BF16))
        iw = iw_scr[rows, :] * (IDX_HEADS ** -0.5)
        sp = sn = None
        for h in range(IDX_HEADS):
            w = iw[:, iw_lane + h:iw_lane + h + 1]
            hs = slice(h * T, (h + 1) * T)
            tp = jnp.maximum(lg_p[hs], 0.0) * w
            tn = jnp.maximum(lg_n[hs], 0.0) * w
            sp, sn = (tp, tn) if sp is None else (sp + tp, sn + tn)
        score[rows, 0:LP] = sp
        score[rows, LP:LC] = sn
        return c

    lax.fori_loop(0, G, score_body, 0)
    score[:, LP:LC] = jnp.where(new_ok, score[:, LP:LC], NEG_INF)

    key = _sort_keys(score[...])
    thr = _kth_largest_key(key, n_sel)
    gt = key > thr
    eq = key == thr
    need = float(n_sel) - jnp.sum(jnp.where(gt, 1.0, 0.0), axis=-1, keepdims=True)
    pref, _ = _blocked_prefix(jnp.where(eq, 1.0, 0.0), tri_ref[...], jnp.zeros((R, 1), F32))
    pref = jnp.concatenate(pref, axis=1)
    bias[...] = jnp.where(gt | (eq & (pref <= need)), 0.0, NEG_INF)
    bias[:, LP:LC] = jnp.where(new_ok, bias[:, LP:LC], NEG_INF)

    def attn_body(g, c):
        rows = pl.ds(pl.multiple_of(g * T, T), T)
        knp[0:T, :] = kn_ref[rows, :]
        vnp[0:T, :] = vn_ref[rows, :]
        qs = stack_heads(qs_scr, rows)
        bh = jnp.concatenate([bias[rows, :]] * H, axis=0)
        lg_p = _dot(qs, kp[g].astype(BF16)) + bh[:, 0:LP]
        lg_n = _dot_nt(qs, knp[...].astype(BF16)) + bh[:, LP:LC]
        m = jnp.maximum(jnp.max(lg_p, axis=-1, keepdims=True), jnp.max(lg_n, axis=-1, keepdims=True))
        ep = jnp.exp(lg_p - m)
        en = jnp.exp(lg_n - m)
        den = jnp.sum(ep, axis=-1, keepdims=True) + jnp.sum(en, axis=-1, keepdims=True)
        o = (_dot_nt(ep.astype(BF16), vp[g].astype(BF16)) + _dot(en.astype(BF16), vnp[...].astype(BF16))) / den
        for h in range(H):
            o_ref[rows, h * dh:(h + 1) * dh] = o[h * T:(h + 1) * T, :].astype(o_ref.dtype)
        return c

    lax.fori_loop(0, G, attn_body, 0)


def attn_sample(z, row0, q_block, iq_block, tail_block, tail_w, iw_off, kn, v, ik, caches, page_table, layer,
                gq_tiled, Bs, T, W, dh, G=16):
    n_pages = page_table.shape[1]
    page = caches[0].shape[2]
    caches_t = [c.transpose(0, 1, 3, 2) for c in caches]
    R = G * T
    rb0 = row0 // R
    LP = n_pages * page
    LC = LP + LANES
    n_sel = min(TOPK_MAX, (LP + T) // 4)
    eye = (jnp.arange(W)[:, None] // dh == jnp.arange(W)[None, :] // dh).astype(BF16)
    tri = (jnp.arange(LANES)[:, None] <= jnp.arange(LANES)[None, :]).astype(BF16)
    any_spec = pl.BlockSpec(memory_space=pl.ANY)
    nspec = pl.BlockSpec((R, dh), lambda i, pt: (rb0 + i, 0))
    grid_spec = pltpu.PrefetchScalarGridSpec(
        num_scalar_prefetch=1,
        grid=(Bs // G,),
        in_specs=[pl.BlockSpec((R, W), lambda i, pt: (rb0 + i, q_block)),
                  pl.BlockSpec((R, W), lambda i, pt: (rb0 + i, iq_block)),
                  pl.BlockSpec((R, tail_w), lambda i, pt: (rb0 + i, tail_block)),
                  nspec, nspec, nspec, any_spec, any_spec, any_spec,
                  pl.BlockSpec((1, W), lambda i, pt: (0, 0)),
                  pl.BlockSpec((W, W), lambda i, pt: (0, 0)),
                  pl.BlockSpec((LANES, LANES), lambda i, pt: (0, 0))],
        out_specs=pl.BlockSpec((R, W), lambda i, pt: (i, 0)),
        scratch_shapes=[pltpu.VMEM((G, dh, LP), F32), pltpu.VMEM((G, dh, LP), F32), pltpu.VMEM((G, dh, LP), F32),
                        pltpu.VMEM((LANES, dh), F32), pltpu.VMEM((LANES, dh), F32), pltpu.VMEM((LANES, dh), F32),
                        pltpu.VMEM((R, W), F32), pltpu.VMEM((R, W), F32), pltpu.VMEM((R, LANES), F32),
                        pltpu.VMEM((R, LC), F32), pltpu.VMEM((R, LC), F32),
                        pltpu.SemaphoreType.DMA((3, G))],
    )
    return pl.pallas_call(
        functools.partial(_attn_sample_kernel, G=G, T=T, dh=dh, layer=layer, n_pages=n_pages, hp=page,
                          iw_off=iw_off, n_sel=n_sel),
        grid_spec=grid_spec,
        out_shape=jax.ShapeDtypeStruct((Bs * T, W), BF16),
        compiler_params=_cparams(("arbitrary",)),
        name="attn_sample",
    )(page_table, z, z, z, kn, v, ik, *caches_t, gq_tiled, eye, tri)


def _block_diag(blocks):
    G, a, b = blocks.shape
    eye = jnp.eye(G, dtype=blocks.dtype)
    return (eye[:, None, :, None] * blocks[:, :, None, :]).reshape(G * a, G * b)


def _lane_blocks(a):
    rows, n = a.shape
    return a.reshape(rows, n // LANES, LANES).transpose(1, 0, 2)


def kernel(x_prompt, x_sample, cache_attn_k, cache_attn_v, cache_idx_k, cache_mem_k, cache_mem_v, state_pool,
           state_conv, state_ssm_re, state_ssm_im, page_table, mem_prompt, norm_g, ffn_in, ffn_out, w_in,
           q_norm_g, k_norm_g, pool_mix, pool_scale, conv_w, ssm_a_re, ssm_a_im, ssm_log_step, ssm_b_re,
           ssm_b_im, ssm_c_re, ssm_c_im, ssm_d, ssm_glu_w, ssm_glu_b, w_branch, w_gate, b_gate, w_o,
           mem_norm_g, w_xq, w_xk, w_xv, xq_norm_g, xk_norm_g, w_xo):
    B, T, D = x_prompt.shape
    Bs, Ts, _ = x_sample.shape
    depth = norm_g.shape[0]
    Mp, Ms = B * T, Bs * Ts
    W = pool_scale.shape[1]
    dh = k_norm_g.shape[1]
    d_ff = ffn_out.shape[2]
    n_mem = mem_prompt.shape[1]
    Wx = w_xq.shape[2]
    SG, SN = ssm_a_re.shape[1:]
    NB = SG * SN // LANES
    past_len = page_table.shape[1] * cache_attn_k.shape[2]
    assert cache_idx_k.shape[-1] == dh and W == N_HEADS_C * dh == IDX_HEADS * dh
    TM = 1024

    o_k = 5 * W
    o_iq = o_k + 2 * dh
    o_ik = o_iq + W
    o_xs = o_ik + dh + IDX_HEADS
    tail_w = 2 * LANES
    Q_BLK, IQ_BLK, XS_BLK = 4, 5, 6
    TAIL_BLK = 7 * W // tail_w
    IW_OFF = 3 * dh
    n_z = 7 * W + tail_w

    h = jnp.concatenate([x_prompt.reshape(Mp, D), x_sample.reshape(Ms, D)], axis=0)
    caches = (cache_attn_k, cache_attn_v, cache_idx_k)
    mem_rows = mem_prompt.reshape(B * n_mem, D)
    outs = [[] for _ in range(16)]

    def ffn(h, l, i, g):
        act = matmul(h, [(ffn_in, (l, i), 0), (ffn_in, (l, i), d_ff // 512)], n_out=d_ff, tn=512, tm=TM,
                     mode="swiglu", out_dtype=BF16, norm_g=g, name="ffn_in")
        return matmul(act, [(ffn_out, (l, i), 0)], n_out=D, tn=512, tm=TM, mode="residual", aux=h, scale=0.5,
                      name="ffn_out")

    for l in range(depth):
        h = ffn(h, l, 0, norm_g[l, 0])

        wl = w_in[l]
        w_in2 = jnp.concatenate([wl[:, 0:o_k], wl[:, o_iq:o_iq + W], wl[:, o_xs:o_xs + W], wl[:, o_k:o_k + 2 * dh],
                                 wl[:, o_ik:o_ik + dh + IDX_HEADS],
                                 jnp.zeros((D, tail_w - 3 * dh - IDX_HEADS), F32)], axis=1)
        z = matmul(h, [(w_in2, (), 0)], n_out=n_z, tn=768, tm=TM, norm_g=norm_g[l, 1], name="in_proj")

        wmix = pool_mix[l].astype(BF16)
        ya_p, yb_p, npool_p, nconv_p = pool_conv(z, 0, B, T, 1, 512, 0, None, None, wmix, pool_scale[l], conv_w[l], W)
        ya_s, yb_s, npool_s, nconv_s = pool_conv(z, Mp, Bs, Ts, 16, Ts, past_len, state_pool[l], state_conv[l],
                                                 wmix, pool_scale[l], conv_w[l], W)

        kn, vv, ik = kv_prep(z, TAIL_BLK, tail_w, k_norm_g[l], dh)
        gq_tiled = jnp.tile(q_norm_g[l], N_HEADS_C).reshape(1, W)
        yc_p = attn_prompt(z, Q_BLK, IQ_BLK, TAIL_BLK, tail_w, IW_OFF, kn[:Mp].reshape(B, T, dh),
                           vv[:Mp].reshape(B, T, dh), ik[:Mp].reshape(B, T, dh), gq_tiled, B, T, W, dh)
        yc_s = attn_sample(z, Mp, Q_BLK, IQ_BLK, TAIL_BLK, tail_w, IW_OFF, kn, vv, ik, caches, page_table, l,
                           gq_tiled, Bs, Ts, W, dh)

        ab_re, ab_im, co_re, co_im = ssm_params(ssm_a_re[l], ssm_a_im[l], ssm_log_step[l])
        flat = lambda a: a.reshape(1, SG * SN)
        lev_re, lev_im, pw_re, pw_im = ssm_tables(flat(ab_re), flat(ab_im), SUBLANES)
        rep = lambda a: _lane_blocks(jnp.broadcast_to(flat(a), (SUBLANES, SG * SN)))
        consts = (
            _lane_blocks(_block_diag(ssm_b_re[l].transpose(0, 2, 1))).astype(BF16),
            _lane_blocks(_block_diag(ssm_b_im[l].transpose(0, 2, 1))).astype(BF16),
            _block_diag(ssm_c_re[l].transpose(0, 2, 1)).reshape(NB, LANES, W).astype(BF16),
            _block_diag(ssm_c_im[l].transpose(0, 2, 1)).reshape(NB, LANES, W).astype(BF16),
            rep(co_re), rep(co_im), _lane_blocks(lev_re), _lane_blocks(lev_im),
            _lane_blocks(pw_re), _lane_blocks(pw_im),
            ssm_d[l].reshape(1, W), ssm_glu_w[l].astype(BF16), ssm_glu_b[l].reshape(1, W))
        yd_p, sre_p, sim_p = ssm_mixer(z, XS_BLK, 0, B, T, consts, None)
        h0 = (state_ssm_re[l].reshape(Bs, NB, LANES), state_ssm_im[l].reshape(Bs, NB, LANES))
        yd_s, sre_s, sim_s = ssm_mixer(z, XS_BLK, Mp, Bs, Ts, consts, h0)

        merged = gated_merge(h, norm_g[l, 1], [(ya_p, ya_s), (yb_p, yb_s), (yc_p, yc_s), (yd_p, yd_s)],
                             w_gate, b_gate, w_branch, l)
        h = matmul(merged, [(w_o, (l,), 0)], n_out=D, tn=512, tm=TM, mode="residual", aux=h, scale=1.0, name="w_o")

        mn = rmsnorm_rows(mem_rows, mem_norm_g[l])
        mk = matmul(mn, [(w_xk, (l,), 0)], n_out=Wx, tn=Wx, tm=B * n_mem, mode="headnorm",
                    aux=jnp.tile(xk_norm_g[l], X_HEADS).reshape(1, Wx), group=Wx // X_HEADS, name="mem_k")
        mv = matmul(mn, [(w_xv, (l,), 0)], n_out=Wx, tn=Wx, tm=B * n_mem, name="mem_v")
        qx = matmul(h, [(w_xq, (l,), 0)], n_out=Wx, tn=Wx, tm=TM, norm_g=norm_g[l, 2], name="w_xq")
        xa_p = cross_attention(qx, 0, B, T, 1, 512, mk.reshape(1, B, n_mem, Wx), mv.reshape(1, B, n_mem, Wx), 0,
                               xq_norm_g[l])
        xa_s = cross_attention(qx, Mp, Bs, Ts, 8, Ts, cache_mem_k.reshape(depth, Bs, n_mem * X_HEADS, Wx // X_HEADS),
                               cache_mem_v.reshape(depth, Bs, n_mem * X_HEADS, Wx // X_HEADS), l, xq_norm_g[l])
        h = matmul((xa_p, xa_s), [(w_xo, (l,), 0)], n_out=D, tn=512, tm=TM, mode="residual", aux=h, scale=1.0,
                   name="w_xo")

        h = ffn(h, l, 1, norm_g[l, 3])

        xh = Wx // X_HEADS
        layer_out = (kn[:Mp].reshape(B, T, dh), vv[:Mp].reshape(B, T, dh), ik[:Mp].reshape(B, T, dh),
                     mk.reshape(B, n_mem, X_HEADS, xh), mv.reshape(B, n_mem, X_HEADS, xh), npool_p, nconv_p,
                     sre_p.reshape(B, SG, SN), sim_p.reshape(B, SG, SN),
                     kn[Mp:].reshape(Bs, Ts, dh), vv[Mp:].reshape(Bs, Ts, dh), ik[Mp:].reshape(Bs, Ts, dh),
                     npool_s, nconv_s, sre_s.reshape(Bs, SG, SN), sim_s.reshape(Bs, SG, SN))
        for acc, val in zip(outs, layer_out):
            acc.append(val)

    return (h[:Mp].reshape(B, T, D), h[Mp:].reshape(Bs, Ts, D)) + tuple(jnp.stack(o) for o in outs)
```

```python
import functools
import math

import jax
import jax.numpy as jnp
from jax import lax
from jax.experimental import pallas as pl
from jax.experimental.pallas import tpu as pltpu

F32, BF16, I32 = jnp.float32, jnp.bfloat16, jnp.int32
EPS = 1e-6
NEG_INF = float("-inf")

V7X_VMEM_BYTES = 64 * 1024 * 1024
VMEM_LIMIT = V7X_VMEM_BYTES - 8 * 1024 * 1024
LANES = 128
SUBLANES = 8

POOL_WINDOWS = (2, 4, 8, 16)
POOL_HIST = 16
CONV_WIDTH = 3
CONV_HIST = 8
N_HEADS_C = 8
IDX_HEADS = 8
TOPK_MAX = 256
X_HEADS = 4
SSM_CHUNK = 256
SSM_SEQ_ROWS = 128


def _cparams(sem, vmem=VMEM_LIMIT):
    return pltpu.CompilerParams(dimension_semantics=sem, vmem_limit_bytes=vmem)


def _dot(a, b):
    return jnp.dot(a, b, preferred_element_type=F32)


def _dot_nt(a, b):
    return lax.dot_general(a, b, (((1,), (1,)), ((), ())), preferred_element_type=F32)


def _rmsnorm_kernel(x_ref, g_ref, o_ref):
    x = x_ref[...]
    ms = jnp.mean(x * x, axis=-1, keepdims=True)
    o_ref[...] = (x * lax.rsqrt(ms + EPS) * g_ref[...]).astype(o_ref.dtype)


def rmsnorm_rows(x, g, tm=512):
    M, D = x.shape
    return pl.pallas_call(
        _rmsnorm_kernel,
        grid=(M // tm,),
        in_specs=[pl.BlockSpec((tm, D), lambda i: (i, 0)), pl.BlockSpec((1, D), lambda i: (0, 0))],
        out_specs=pl.BlockSpec((tm, D), lambda i: (i, 0)),
        out_shape=jax.ShapeDtypeStruct((M, D), BF16),
        compiler_params=_cparams(("arbitrary",)),
        name="rmsnorm",
    )(x, g.reshape(1, D))


def _rms_rows_bf16(x, g):
    ms = jnp.mean(x * x, axis=-1, keepdims=True)
    return (x * lax.rsqrt(ms + EPS) * g).astype(BF16)


def _pick_rows(ref_a, ref_b, rows, n_a_tiles):
    if ref_b is None:
        return ref_a[rows, :]
    return jnp.where(pl.program_id(1) >= n_a_tiles, ref_b[rows, :], ref_a[rows, :])


def _split_row_specs(xa, xb, tm, tn=None):
    na = xa.shape[0] // tm
    width = xa.shape[1] if tn is None else tn
    col = (lambda j: 0) if tn is None else (lambda j: j)
    spec_a = pl.BlockSpec((tm, width), lambda j, i: (jnp.minimum(i, na - 1), col(j)))
    spec_b = pl.BlockSpec((tm, width), lambda j, i: (jnp.maximum(i - na, 0), col(j)),
                          pipeline_mode=pl.Buffered(1) if (xb.shape[0] == tm and tn is None) else None)
    return spec_a, spec_b, na


def _mm_kernel(*refs, nw, mode, scale, sub, group, norm, n_a_tiles, aux_a_tiles, out_a_tiles):
    x_ref = refs[0]
    xb_ref = None
    if n_a_tiles is not None:
        xb_ref = refs[1]
        refs = refs[1:]
    w_refs = refs[1:1 + nw]
    p = 1 + nw
    if norm:
        g_ref = refs[p]
        p += 1
    aux_ref = auxb_ref = None
    if mode in ("bias_sigmoid", "residual", "headnorm"):
        aux_ref = refs[p]
        p += 1
        if aux_a_tiles is not None:
            auxb_ref = refs[p]
            p += 1
    o_ref = refs[p]
    ob_ref = None
    if out_a_tiles is not None:
        ob_ref = refs[p + 1]
        p += 1
    scr = refs[p + 1:p + 1 + nw]

    @pl.when(pl.program_id(1) == 0)
    def _():
        for w_ref, s in zip(w_refs, scr):
            s[...] = w_ref[...].astype(BF16)

    tm = x_ref.shape[0]

    def body(r, carry):
        rows = pl.ds(pl.multiple_of(r * sub, sub), sub)
        x = _pick_rows(x_ref, xb_ref, rows, n_a_tiles)
        if norm:
            x = _rms_rows_bf16(x, g_ref[...])
        acc = [_dot(x, s[...]) for s in scr]
        if mode == "swiglu":
            a, b = acc
            y = (a * jax.nn.sigmoid(a)) * b
        elif mode == "bias_sigmoid":
            y = jax.nn.sigmoid(acc[0] + aux_ref[...])
        elif mode == "residual":
            y = _pick_rows(aux_ref, auxb_ref, rows, aux_a_tiles) + scale * acc[0]
        elif mode == "headnorm":
            a = acc[0]
            parts = []
            for h in range(a.shape[1] // group):
                ah = a[:, h * group:(h + 1) * group]
                ms = jnp.mean(ah * ah, axis=-1, keepdims=True)
                parts.append(ah * lax.rsqrt(ms + EPS))
            y = jnp.concatenate(parts, axis=1) * aux_ref[...]
        else:
            y = acc[0]
        if ob_ref is None:
            o_ref[rows, :] = y.astype(o_ref.dtype)
        else:
            @pl.when(pl.program_id(1) < out_a_tiles)
            def _():
                o_ref[rows, :] = y.astype(o_ref.dtype)

            @pl.when(pl.program_id(1) >= out_a_tiles)
            def _():
                ob_ref[rows, :] = y.astype(ob_ref.dtype)
        return carry

    lax.fori_loop(0, tm // sub, body, 0, unroll=True)


def matmul(x, weights, *, n_out, tn, tm, mode="plain", aux=None, scale=1.0, out_dtype=F32, group=LANES,
           norm_g=None, split_out_rows=None, name="matmul"):
    nw = len(weights)
    sub = min(tm, 256)
    n_a_tiles = None
    if isinstance(x, tuple):
        xa, xb = x
        spec_a, spec_b, n_a_tiles = _split_row_specs(xa, xb, tm)
        M, K = xa.shape[0] + xb.shape[0], xa.shape[1]
        in_specs = [spec_a, spec_b]
        args = [xa, xb]
    else:
        M, K = x.shape
        in_specs = [pl.BlockSpec((tm, K), lambda j, i: (i, 0))]
        args = [x]
    for arr, lead, coff in weights:
        nl = len(lead)
        in_specs.append(pl.BlockSpec((None,) * nl + (K, tn),
                                     functools.partial(lambda j, i, lead, coff: (*lead, 0, coff + j), lead=lead, coff=coff)))
        args.append(arr)
    if norm_g is not None:
        in_specs.append(pl.BlockSpec((1, K), lambda j, i: (0, 0)))
        args.append(norm_g.reshape(1, K))
    if mode in ("bias_sigmoid", "headnorm"):
        in_specs.append(pl.BlockSpec((1, tn), lambda j, i: (0, j)))
        args.append(aux)
    aux_a_tiles = None
    if mode == "residual" and isinstance(aux, tuple):
        spec_a, spec_b, aux_a_tiles = _split_row_specs(aux[0], aux[1], tm, tn)
        in_specs += [spec_a, spec_b]
        args += list(aux)
    elif mode == "residual":
        in_specs.append(pl.BlockSpec((tm, tn), lambda j, i: (i, j)))
        args.append(aux)
    out_a_tiles = None
    out_specs = pl.BlockSpec((tm, tn), lambda j, i: (i, j))
    out_shape = jax.ShapeDtypeStruct((M, n_out), out_dtype)
    if split_out_rows is not None:
        out_a_tiles = split_out_rows // tm
        out_specs = [pl.BlockSpec((tm, tn), lambda j, i: (jnp.minimum(i, out_a_tiles - 1), j)),
                     pl.BlockSpec((tm, tn), lambda j, i: (jnp.maximum(i - out_a_tiles, 0), j))]
        out_shape = [jax.ShapeDtypeStruct((split_out_rows, n_out), out_dtype),
                     jax.ShapeDtypeStruct((M - split_out_rows, n_out), out_dtype)]
    return pl.pallas_call(
        functools.partial(_mm_kernel, nw=nw, mode=mode, scale=scale, sub=sub, group=group, norm=norm_g is not None,
                          n_a_tiles=n_a_tiles, aux_a_tiles=aux_a_tiles, out_a_tiles=out_a_tiles),
        grid=(n_out // tn, M // tm),
        in_specs=in_specs,
        out_specs=out_specs,
        out_shape=out_shape,
        scratch_shapes=[pltpu.VMEM((K, tn), BF16) for _ in range(nw)],
        compiler_params=_cparams(("arbitrary", "arbitrary")),
        name=name,
    )(*args)


def _merge_kernel(*refs, nb, n_a_tiles):
    h_ref, g_ref = refs[0:2]
    br = refs[2:2 + nb]
    brb = refs[2 + nb:2 + 2 * nb]
    refs = refs[nb:]
    wg = refs[2 + nb:2 + 2 * nb]
    bg = refs[2 + 2 * nb:2 + 3 * nb]
    wb = refs[2 + 3 * nb:2 + 4 * nb]
    o_ref = refs[2 + 4 * nb]
    sg = refs[3 + 4 * nb:3 + 5 * nb]
    sb = refs[3 + 5 * nb:3 + 6 * nb]

    @pl.when(pl.program_id(1) == 0)
    def _():
        for w_ref, s in zip(wg + wb, sg + sb):
            s[...] = w_ref[...].astype(BF16)

    tm = o_ref.shape[0]
    sub = min(tm, 256)

    def body(r, carry):
        rows = pl.ds(pl.multiple_of(r * sub, sub), sub)
        u = _rms_rows_bf16(h_ref[rows, :], g_ref[...])
        acc = None
        for k in range(nb):
            gate = jax.nn.sigmoid(_dot(u, sg[k][...]) + bg[k][...])
            t = gate * _dot(_pick_rows(br[k], brb[k], rows, n_a_tiles), sb[k][...])
            acc = t if acc is None else acc + t
        o_ref[rows, :] = acc.astype(o_ref.dtype)
        return carry

    lax.fori_loop(0, tm // sub, body, 0, unroll=True)


def gated_merge(h, norm_g, branches, w_gate, b_gate, w_branch, layer, *, tn=256, tm=1024):
    nb = len(branches)
    M = h.shape[0]
    W = branches[0][0].shape[1]
    D = w_branch.shape[-1]
    nj = D // tn
    bias = b_gate.reshape(b_gate.shape[0], 1, nb * D)

    def per_branch(shape):
        return [pl.BlockSpec(shape, functools.partial(lambda j, i, k: (layer, 0, k * nj + j), k=k)) for k in range(nb)]

    row_specs = [_split_row_specs(a, b, tm) for a, b in branches]
    n_a_tiles = row_specs[0][2]
    in_specs = [pl.BlockSpec((tm, D), lambda j, i: (i, 0)), pl.BlockSpec((1, D), lambda j, i: (0, 0))]
    in_specs += [s[0] for s in row_specs] + [s[1] for s in row_specs]
    in_specs += per_branch((None, D, tn)) + per_branch((None, 1, tn))
    in_specs += [pl.BlockSpec((None, None, W, tn), functools.partial(lambda j, i, k: (layer, k, 0, j), k=k))
                 for k in range(nb)]
    return pl.pallas_call(
        functools.partial(_merge_kernel, nb=nb, n_a_tiles=n_a_tiles),
        grid=(nj, M // tm),
        in_specs=in_specs,
        out_specs=pl.BlockSpec((tm, tn), lambda j, i: (i, j)),
        out_shape=jax.ShapeDtypeStruct((M, D), BF16),
        scratch_shapes=[pltpu.VMEM((D, tn), BF16) for _ in range(nb)] + [pltpu.VMEM((W, tn), BF16) for _ in range(nb)],
        compiler_params=_cparams(("arbitrary", "arbitrary")),
        name="gated_merge",
    )(h, norm_g.reshape(1, D), *[a for a, _ in branches], *[b for _, b in branches],
      *([w_gate] * nb), *([bias] * nb), *([w_branch] * nb))


def _kvprep_kernel(t_ref, g_ref, k_ref, v_ref, ik_ref, *, dh):
    t = t_ref[...]
    k = t[:, 0:dh]
    ms = jnp.mean(k * k, axis=-1, keepdims=True)
    k_ref[...] = k * lax.rsqrt(ms + EPS) * g_ref[...]
    v_ref[...] = t[:, dh:2 * dh]
    ik_ref[...] = t[:, 2 * dh:3 * dh]


def kv_prep(z, tail_block, tail_w, gk, dh, tm=1024):
    M = z.shape[0]
    out = jax.ShapeDtypeStruct((M, dh), F32)
    return pl.pallas_call(
        functools.partial(_kvprep_kernel, dh=dh),
        grid=(M // tm,),
        in_specs=[pl.BlockSpec((tm, tail_w), lambda i: (i, tail_block)), pl.BlockSpec((1, dh), lambda i: (0, 0))],
        out_specs=[pl.BlockSpec((tm, dh), lambda i: (i, 0))] * 3,
        out_shape=[out, out, out],
        compiler_params=_cparams(("arbitrary",)),
        name="kv_prep",
    )(z, gk.reshape(1, dh))


def _poolconv_kernel(*refs, G, T, W, pos0, has_state):
    xp_ref, xc_ref, bg_ref, cg_ref = refs[0:4]
    p = 4
    if has_state:
        pbuf_ref, cbuf_ref = refs[4:6]
        p = 6
    wmix_ref, pscale_ref, convw_ref = refs[p:p + 3]
    ya_ref, yb_ref, npool_ref, nconv_ref = refs[p + 3:p + 7]
    fullp, fullc = refs[p + 7:p + 9]
    PH, CH = POOL_HIST, CONV_HIST
    nh = CONV_WIDTH - 1
    gw = W // len(POOL_WINDOWS)
    c = pl.program_id(1)

    @pl.when(c == 0)
    def _():
        if has_state:
            fullp[:, 1:PH, :] = pbuf_ref[...]
            fullc[:, CH - nh:CH, :] = cbuf_ref[...]
        else:
            fullp[:, 0:PH, :] = jnp.zeros((G, PH, W), F32)
            fullc[:, 0:CH, :] = jnp.zeros((G, CH, W), F32)

    @pl.when(c > 0)
    def _():
        fullp[:, 0:PH, :] = fullp[:, T:T + PH, :]
        fullc[:, 0:CH, :] = fullc[:, T:T + CH, :]

    fullp[:, PH:PH + T, :] = xp_ref[...].reshape(G, T, W)
    t_idx = lax.broadcasted_iota(I32, (1, T, 1), 1) + (c * T + (pos0 + 1))
    for gi, w in enumerate(POOL_WINDOWS):
        cols = slice(gi * gw, (gi + 1) * gw)
        acc = fullp[:, PH:PH + T, cols]
        for j in range(1, w):
            acc = acc + fullp[:, PH - j:PH - j + T, cols]
        cnt = jnp.minimum(t_idx, w).astype(F32)
        d = acc / cnt - fullp[:, PH:PH + T, cols]
        y = _dot(d.reshape(G * T, gw).astype(BF16), wmix_ref[gi]) * pscale_ref[:, cols]
        ya_ref[:, cols] = y.astype(ya_ref.dtype)
    npool_ref[...] = fullp[:, T + 1:T + PH, :]

    fullc[:, CH:CH + T, :] = (cg_ref[...] * xc_ref[...]).reshape(G, T, W)
    y = None
    for j in range(CONV_WIDTH):
        wj = convw_ref[j:j + 1, :].reshape(1, 1, W)
        term = wj * fullc[:, CH - nh + j:CH - nh + j + T, :]
        y = term if y is None else y + term
    yb_ref[...] = (bg_ref[...] * y.reshape(G * T, W)).astype(yb_ref.dtype)
    nconv_ref[...] = fullc[:, CH + T - nh:CH + T, :]


def pool_conv(z, row0, Bt, T, G, TC, pos0, pool_buf, conv_buf, wmix_bf16, pool_scale, conv_w, W):
    has_state = pool_buf is not None
    assert G == 1 or TC == T
    R = G * TC
    nc = T // TC
    rb0 = row0 // R
    nh = CONV_WIDTH - 1

    def zspec(cb):
        return pl.BlockSpec((R, W), functools.partial(lambda i, c, cb: (rb0 + i * nc + c, cb), cb=cb))

    in_specs = [zspec(0), zspec(1), zspec(2), zspec(3)]
    args = [z, z, z, z]
    if has_state:
        in_specs += [pl.BlockSpec((G, POOL_HIST - 1, W), lambda i, c: (i, 0, 0)),
                     pl.BlockSpec((G, nh, W), lambda i, c: (i, 0, 0))]
        args += [pool_buf, conv_buf]
    nwin = len(POOL_WINDOWS)
    in_specs += [pl.BlockSpec((nwin, W // nwin, W // nwin), lambda i, c: (0, 0, 0)),
                 pl.BlockSpec((1, W), lambda i, c: (0, 0)),
                 pl.BlockSpec((CONV_WIDTH, W), lambda i, c: (0, 0))]
    args += [wmix_bf16, pool_scale.reshape(1, W), conv_w]
    return pl.pallas_call(
        functools.partial(_poolconv_kernel, G=G, T=TC, W=W, pos0=pos0, has_state=has_state),
        grid=(Bt // G, nc),
        in_specs=in_specs,
        out_specs=[pl.BlockSpec((R, W), lambda i, c: (i * nc + c, 0)), pl.BlockSpec((R, W), lambda i, c: (i * nc + c, 0)),
                   pl.BlockSpec((G, POOL_HIST - 1, W), lambda i, c: (i, 0, 0)),
                   pl.BlockSpec((G, nh, W), lambda i, c: (i, 0, 0))],
        out_shape=[jax.ShapeDtypeStruct((Bt * T, W), BF16), jax.ShapeDtypeStruct((Bt * T, W), BF16),
                   jax.ShapeDtypeStruct((Bt, POOL_HIST - 1, W), F32), jax.ShapeDtypeStruct((Bt, nh, W), F32)],
        scratch_shapes=[pltpu.VMEM((G, POOL_HIST + TC, W), F32), pltpu.VMEM((G, CONV_HIST + TC, W), F32)],
        compiler_params=_cparams(("arbitrary", "arbitrary")),
        name="pool_conv",
    )(*args)


def _ssm_params_kernel(are_ref, aim_ref, ls_ref, abre_ref, abim_ref, core_ref, coim_ref):
    a_re = are_ref[...]
    a_im = aim_ref[...]
    step = jnp.exp(ls_ref[...])
    decay = jnp.exp(step * a_re)
    ab_re = decay * jnp.cos(step * a_im)
    ab_im = decay * jnp.sin(step * a_im)
    den = a_re * a_re + a_im * a_im
    nr = ab_re - 1.0
    abre_ref[...] = ab_re
    abim_ref[...] = ab_im
    core_ref[...] = (nr * a_re + ab_im * a_im) / den
    coim_ref[...] = (ab_im * a_re - nr * a_im) / den


def ssm_params(a_re, a_im, log_step):
    G, N = a_re.shape
    out = jax.ShapeDtypeStruct((G, N), F32)
    return pl.pallas_call(_ssm_params_kernel, out_shape=[out, out, out, out], name="ssm_params")(
        a_re, a_im, log_step.reshape(G, 1))


def _scan_levels(C):
    return [1 << k for k in range(int(math.log2(C)))]


def _ssm_tables_kernel(abre_ref, abim_ref, lre_ref, lim_ref, pre_ref, pim_ref, *, C):
    ar = abre_ref[...]
    ai = abim_ref[...]
    N = ar.shape[1]
    row = lax.broadcasted_iota(I32, (C, N), 0)
    hr = jnp.where(row == 0, ar, 0.0)
    hi = jnp.where(row == 0, ai, 0.0)
    lre_ref[...] = jnp.zeros(lre_ref.shape, F32)
    lim_ref[...] = jnp.zeros(lim_ref.shape, F32)
    for k, s in enumerate(_scan_levels(C)):
        lre_ref[k:k + 1, :] = ar
        lim_ref[k:k + 1, :] = ai
        sr = jnp.where(row >= s, pltpu.roll(hr, s, 0), 0.0)
        si = jnp.where(row >= s, pltpu.roll(hi, s, 0), 0.0)
        hr, hi = hr + ar * sr - ai * si, hi + ar * si + ai * sr
        ar, ai = ar * ar - ai * ai, 2.0 * ar * ai
    pre_ref[...] = hr
    pim_ref[...] = hi


def ssm_tables(ab_re, ab_im, C):
    N = ab_re.shape[1]
    nlev = len(_scan_levels(C))
    lev = jax.ShapeDtypeStruct((SUBLANES * ((nlev + SUBLANES - 1) // SUBLANES), N), F32)
    pw = jax.ShapeDtypeStruct((C, N), F32)
    return pl.pallas_call(functools.partial(_ssm_tables_kernel, C=C), out_shape=[lev, lev, pw, pw],
                          name="ssm_tables")(ab_re, ab_im)


def _ssm_fold_kernel(bre_ref, bim_ref, core_ref, coim_ref, ore_ref, oim_ref):
    bre = bre_ref[...]
    bim = bim_ref[...]
    cor = core_ref[...]
    coi = coim_ref[...]
    ore_ref[...] = (cor * bre - coi * bim).astype(ore_ref.dtype)
    oim_ref[...] = (cor * bim + coi * bre).astype(oim_ref.dtype)


def ssm_fold_input(b_re_bd, b_im_bd, co_re, co_im):
    out = jax.ShapeDtypeStruct(b_re_bd.shape, BF16)
    return pl.pallas_call(_ssm_fold_kernel, out_shape=[out, out], name="ssm_fold")(b_re_bd, b_im_bd, co_re, co_im)


def _gelu_tanh(x):
    return 0.5 * x * (1.0 + jnp.tanh(math.sqrt(2.0 / math.pi) * (x + 0.044715 * (x * x * x))))


def _ssm_kernel(*refs, R, T, NB, chained):
    (xs_ref, bre_ref, bim_ref, cre_ref, cim_ref, lre_ref, lim_ref) = refs[0:7]
    p = 7
    if chained:
        pre_ref, pim_ref = refs[p:p + 2]
        p += 2
    else:
        h0r_ref, h0i_ref = refs[p:p + 2]
        p += 2
    d_ref, gw_ref, gb_ref = refs[p:p + 3]
    yd_ref, sre_ref, sim_ref = refs[p + 3:p + 6]
    p += 6
    if chained:
        car_ref, cai_ref = refs[p:p + 2]
    G = R // T
    xs = xs_ref[...]
    xb = xs.astype(BF16)
    tpos = lax.broadcasted_iota(I32, (R, LANES), 0) % SUBLANES
    sub_pos = lax.broadcasted_iota(I32, (SUBLANES, LANES), 0)
    levels = _scan_levels(SUBLANES)

    if chained:
        c = pl.program_id(1)

        @pl.when(c == 0)
        def _():
            car_ref[...] = jnp.zeros(car_ref.shape, F32)
            cai_ref[...] = jnp.zeros(cai_ref.shape, F32)

    y = d_ref[...] * xs
    for cb in range(NB):
        hr = _dot(xb, bre_ref[cb])
        hi = _dot(xb, bim_ref[cb])
        if not chained:
            ar = lre_ref[cb][0:1, :]
            ai = lim_ref[cb][0:1, :]
            h0r = jnp.broadcast_to(h0r_ref[:, cb:cb + 1, :], (G, T, LANES)).reshape(R, LANES)
            h0i = jnp.broadcast_to(h0i_ref[:, cb:cb + 1, :], (G, T, LANES)).reshape(R, LANES)
            first = tpos == 0
            hr = hr + jnp.where(first, ar * h0r - ai * h0i, 0.0)
            hi = hi + jnp.where(first, ar * h0i + ai * h0r, 0.0)
        for k, s in enumerate(levels):
            ar = jnp.where(sub_pos >= s, lre_ref[cb][k:k + 1, :], 0.0)
            ai = jnp.where(sub_pos >= s, lim_ref[cb][k:k + 1, :], 0.0)
            ar = jnp.concatenate([ar] * (R // SUBLANES), axis=0)
            ai = jnp.concatenate([ai] * (R // SUBLANES), axis=0)
            sr = pltpu.roll(hr, s, 0)
            si = pltpu.roll(hi, s, 0)
            hr, hi = hr + ar * sr - ai * si, hi + ar * si + ai * sr
        if chained:
            cr = car_ref[cb:cb + 1, :]
            ci = cai_ref[cb:cb + 1, :]
            pwr = pre_ref[cb]
            pwi = pim_ref[cb]
            grs, gis = [], []
            for v in range(R // SUBLANES):
                rows = slice(v * SUBLANES, (v + 1) * SUBLANES)
                cbr = jnp.broadcast_to(cr, (SUBLANES, LANES))
                cbi = jnp.broadcast_to(ci, (SUBLANES, LANES))
                gr = hr[rows] + pwr * cbr - pwi * cbi
                gi = hi[rows] + pwr * cbi + pwi * cbr
                cr = gr[SUBLANES - 1:SUBLANES, :]
                ci = gi[SUBLANES - 1:SUBLANES, :]
                grs.append(gr)
                gis.append(gi)
            hr = jnp.concatenate(grs, axis=0)
            hi = jnp.concatenate(gis, axis=0)
            car_ref[cb:cb + 1, :] = cr
            cai_ref[cb:cb + 1, :] = ci
        else:
            sre_ref[:, cb:cb + 1, :] = hr.reshape(G, T, LANES)[:, T - 1:T, :]
            sim_ref[:, cb:cb + 1, :] = hi.reshape(G, T, LANES)[:, T - 1:T, :]
        y = y + _dot(hr.astype(BF16), cre_ref[cb]) - _dot(hi.astype(BF16), cim_ref[cb])

    z = _gelu_tanh(y)
    out = z * jax.nn.sigmoid(_dot(z.astype(BF16), gw_ref[...]) + gb_ref[...])
    yd_ref[...] = out.astype(yd_ref.dtype)

    if chained:
        @pl.when(c == pl.num_programs(1) - 1)
        def _():
            sre_ref[0] = car_ref[...]
            sim_ref[0] = cai_ref[...]


def ssm_mixer(z, xs_block, row0, Bt, T, consts, h0):
    (bre3, bim3, cre3, cim3, lre3, lim3, pre3, pim3, dvec, gw, gb) = consts
    NB = bre3.shape[0]
    W = dvec.shape[1]
    chained = h0 is None
    assert chained or T == SUBLANES
    R = SSM_CHUNK if chained else SSM_SEQ_ROWS
    rb0 = row0 // R

    def full(a):
        nd = a.ndim
        return pl.BlockSpec(a.shape, lambda *_: (0,) * nd)

    if chained:
        nchunk = T // R
        grid = (Bt, nchunk)
        xs_spec = pl.BlockSpec((R, W), lambda b, c: (rb0 + b * nchunk + c, xs_block))
        st_args, st_specs = [pre3, pim3], [full(pre3), full(pim3)]
        yd_spec = pl.BlockSpec((R, W), lambda b, c: (b * nchunk + c, 0))
        s_spec = pl.BlockSpec((1, NB, LANES), lambda b, c: (b, 0, 0))
        scratch = [pltpu.VMEM((NB, LANES), F32), pltpu.VMEM((NB, LANES), F32)]
        sem = ("arbitrary", "arbitrary")
        Tk = R
    else:
        G = R // T
        grid = (Bt // G,)
        xs_spec = pl.BlockSpec((R, W), lambda i: (rb0 + i, xs_block))
        st_args = list(h0)
        st_specs = [pl.BlockSpec((G, NB, LANES), lambda i: (i, 0, 0))] * 2
        yd_spec = pl.BlockSpec((R, W), lambda i: (i, 0))
        s_spec = pl.BlockSpec((G, NB, LANES), lambda i: (i, 0, 0))
        scratch = []
        sem = ("arbitrary",)
        Tk = T
    shared = [bre3, bim3, cre3, cim3, lre3, lim3]
    tailc = [dvec, gw, gb]
    s_shape = jax.ShapeDtypeStruct((Bt, NB, LANES), F32)
    return pl.pallas_call(
        functools.partial(_ssm_kernel, R=R, T=Tk, NB=NB, chained=chained),
        grid=grid,
        in_specs=[xs_spec] + [full(a) for a in shared] + st_specs + [full(a) for a in tailc],
        out_specs=[yd_spec, s_spec, s_spec],
        out_shape=[jax.ShapeDtypeStruct((Bt * T, W), BF16), s_shape, s_shape],
        scratch_shapes=scratch,
        compiler_params=_cparams(sem),
        name="ssm_chained" if chained else "ssm_stateful",
    )(z, *shared, *st_args, *tailc)


def _sort_keys(score):
    b = pltpu.bitcast(score, I32)
    key = jnp.where(b < 0, b ^ jnp.int32(0x7FFFFFFF), b)
    return jnp.where(key == -1, 0, key)


def _kth_largest_key(key, k):
    rows = key.shape[0]

    def body(i, t):
        cand = t + (jnp.int32(1) << (31 - i))
        cnt = jnp.sum(jnp.where(key >= cand, 1.0, 0.0), axis=-1, keepdims=True)
        return jnp.where(cnt >= float(k), cand, t)

    return lax.fori_loop(0, 32, body, jnp.full((rows, 1), -2 ** 31, I32))


def _blocked_prefix(eq, tri, offset):
    outs = []
    run = offset
    for j in range(eq.shape[1] // LANES):
        blk = eq[:, j * LANES:(j + 1) * LANES].astype(BF16)
        pj = _dot(blk, tri) + run
        outs.append(pj)
        run = pj[:, LANES - 1:LANES]
    return outs, run


def _attn_prompt_kernel(q_ref, iq_ref, tail_ref, k_ref, v_ref, ik_ref, gq_ref, e_ref, tri_ref, o_ref,
                        kb, vb, ikb, bias, *, TQ, L, dh, iw_off, n_sel, n_buckets):
    qb = pl.program_id(1)

    @pl.when(qb == 0)
    def _():
        kb[...] = k_ref[0].astype(BF16)
        vb[...] = v_ref[0].astype(BF16)
        ikb[...] = ik_ref[0].astype(BF16)

    scale = dh ** -0.5
    q = q_ref[...]
    q2 = q * q
    hi = q2.astype(BF16)
    lo = (q2 - hi.astype(F32)).astype(BF16)
    ss = _dot(hi, e_ref[...]) + _dot(lo, e_ref[...])
    qn = (q * lax.rsqrt(ss * (1.0 / dh) + EPS) * gq_ref[...] * scale).astype(BF16)
    iqs = (iq_ref[...] * scale).astype(BF16)
    iw = tail_ref[:, iw_off:iw_off + IDX_HEADS] * (IDX_HEADS ** -0.5)

    def attend(Lk):
        score = None
        for h in range(IDX_HEADS):
            lg = _dot_nt(iqs[:, h * dh:(h + 1) * dh], ikb[0:Lk, :])
            t = jnp.maximum(lg, 0.0) * iw[:, h:h + 1]
            score = t if score is None else score + t
        col = lax.broadcasted_iota(I32, (TQ, Lk), 1)
        qpos = qb * TQ + lax.broadcasted_iota(I32, (TQ, Lk), 0)
        causal = col <= qpos
        score = jnp.where(causal, score, NEG_INF)

        key = _sort_keys(score)
        thr = _kth_largest_key(key, n_sel)
        gt = key > thr
        eq = key == thr
        need = float(n_sel) - jnp.sum(jnp.where(gt, 1.0, 0.0), axis=-1, keepdims=True)
        pref, _ = _blocked_prefix(jnp.where(eq, 1.0, 0.0), tri_ref[...], jnp.zeros((TQ, 1), F32))
        pref = jnp.concatenate(pref, axis=1)
        sel = (gt | (eq & (pref <= need))) & causal
        bias[:, 0:Lk] = jnp.where(sel, 0.0, NEG_INF)

        for h in range(N_HEADS_C):
            lg = _dot_nt(qn[:, h * dh:(h + 1) * dh], kb[0:Lk, :]) + bias[:, 0:Lk]
            m = jnp.max(lg, axis=-1, keepdims=True)
            e = jnp.exp(lg - m)
            den = jnp.sum(e, axis=-1, keepdims=True)
            o = _dot(e.astype(BF16), vb[0:Lk, :]) / den
            o_ref[:, h * dh:(h + 1) * dh] = o.astype(o_ref.dtype)

    per = (L // TQ) // n_buckets
    for bk in range(n_buckets):
        pl.when(qb // per == bk)(functools.partial(attend, (bk + 1) * per * TQ))


def attn_prompt(z, q_block, iq_block, tail_block, tail_w, iw_off, kn, v, ik, gq_tiled, B, T, W, dh, TQ=256,
                n_buckets=4):
    nq = T // TQ
    n_sel = min(TOPK_MAX, T // 4)
    eye = (jnp.arange(W)[:, None] // dh == jnp.arange(W)[None, :] // dh).astype(BF16)
    tri = (jnp.arange(LANES)[:, None] <= jnp.arange(LANES)[None, :]).astype(BF16)
    kspec = pl.BlockSpec((1, T, dh), lambda b, i: (b, 0, 0))
    return pl.pallas_call(
        functools.partial(_attn_prompt_kernel, TQ=TQ, L=T, dh=dh, iw_off=iw_off, n_sel=n_sel, n_buckets=n_buckets),
        grid=(B, nq),
        in_specs=[pl.BlockSpec((TQ, W), lambda b, i: (b * nq + i, q_block)),
                  pl.BlockSpec((TQ, W), lambda b, i: (b * nq + i, iq_block)),
                  pl.BlockSpec((TQ, tail_w), lambda b, i: (b * nq + i, tail_block)),
                  kspec, kspec, kspec,
                  pl.BlockSpec((1, W), lambda b, i: (0, 0)),
                  pl.BlockSpec((W, W), lambda b, i: (0, 0)),
                  pl.BlockSpec((LANES, LANES), lambda b, i: (0, 0))],
        out_specs=pl.BlockSpec((TQ, W), lambda b, i: (b * nq + i, 0)),
        out_shape=jax.ShapeDtypeStruct((B * T, W), BF16),
        scratch_shapes=[pltpu.VMEM((T, dh), BF16), pltpu.VMEM((T, dh), BF16), pltpu.VMEM((T, dh), BF16),
                        pltpu.VMEM((TQ, T), F32)],
        compiler_params=_cparams(("arbitrary", "arbitrary")),
        name="attn_prompt",
    )(z, z, z, kn, v, ik, gq_tiled, eye, tri)


def _xattn_kernel(q_ref, mk_ref, mv_ref, g_ref, o_ref, *, G, T, dh, scale):
    g = g_ref[...]
    for s in range(G):
        rows = slice(s * T, (s + 1) * T)
        for h in range(X_HEADS):
            cols = slice(h * dh, (h + 1) * dh)
            qh = q_ref[rows, cols]
            ms = jnp.mean(qh * qh, axis=-1, keepdims=True)
            qn = (qh * lax.rsqrt(ms + EPS) * g).astype(BF16)
            lg = _dot_nt(qn, mk_ref[s, :, cols].astype(BF16)) * scale
            m = jnp.max(lg, axis=-1, keepdims=True)
            e = jnp.exp(lg - m)
            den = jnp.sum(e, axis=-1, keepdims=True)
            o = _dot(e.astype(BF16), mv_ref[s, :, cols].astype(BF16)) / den
            o_ref[rows, cols] = o.astype(o_ref.dtype)


def _xattn_rows_kernel(q_ref, mk_ref, mv_ref, g_ref, o_ref, *, G, T, dh, scale):
    g = g_ref[...]
    H = X_HEADS
    n = mk_ref.shape[1]
    own = (lax.broadcasted_iota(I32, (H * T, n), 1) % H) == (lax.broadcasted_iota(I32, (H * T, n), 0) // T)
    for s in range(G):
        rows = slice(s * T, (s + 1) * T)
        parts = []
        for h in range(H):
            qh = q_ref[rows, h * dh:(h + 1) * dh]
            ms = jnp.mean(qh * qh, axis=-1, keepdims=True)
            parts.append(qh * lax.rsqrt(ms + EPS) * g)
        qs = jnp.concatenate(parts, axis=0).astype(BF16)
        lg = jnp.where(own, _dot_nt(qs, mk_ref[s].astype(BF16)) * scale, NEG_INF)
        m = jnp.max(lg, axis=-1, keepdims=True)
        e = jnp.exp(lg - m)
        den = jnp.sum(e, axis=-1, keepdims=True)
        o = _dot(e.astype(BF16), mv_ref[s].astype(BF16)) / den
        for h in range(H):
            o_ref[rows, h * dh:(h + 1) * dh] = o[h * T:(h + 1) * T, :].astype(o_ref.dtype)


def cross_attention(qx, row0, Bt, T, G, TQ, mem_k, mem_v, layer, gq):
    Wx = qx.shape[1]
    dh = Wx // X_HEADS
    n_rows, wm = mem_k.shape[2:]
    assert G == 1 or TQ == T
    R = G * TQ
    nt = T // TQ
    rb0 = row0 // R
    mspec = pl.BlockSpec((None, G, n_rows, wm), lambda i, t: (layer, i, 0, 0))
    body = _xattn_kernel if wm == Wx else _xattn_rows_kernel
    return pl.pallas_call(
        functools.partial(body, G=G, T=TQ, dh=dh, scale=dh ** -0.5),
        grid=(Bt // G, nt),
        in_specs=[pl.BlockSpec((R, Wx), lambda i, t: (rb0 + i * nt + t, 0)), mspec, mspec,
                  pl.BlockSpec((1, dh), lambda i, t: (0, 0))],
        out_specs=pl.BlockSpec((R, Wx), lambda i, t: (i * nt + t, 0)),
        out_shape=jax.ShapeDtypeStruct((Bt * T, Wx), BF16),
        compiler_params=_cparams(("arbitrary", "arbitrary")),
        name="cross_attention",
    )(qx, mem_k, mem_v, gq.reshape(1, dh))


def _attn_sample_kernel(pt_ref, q_ref, iq_ref, tail_ref, kn_ref, vn_ref, ikn_ref, ck_hbm, cv_hbm, cik_hbm,
                        gq_ref, e_ref, tri_ref, o_ref,
                        kp, vp, ikp, qs_scr, iqs_scr, iw_scr, score, bias, sems,
                        *, G, T, dh, layer, n_pages, hp, iw_off, n_sel):
    step = pl.program_id(0)
    R = G * T
    H = N_HEADS_C
    LP = n_pages * hp
    LC = LP + LANES

    def page_copies(g, p):
        page = pt_ref[step * G + g, p]
        dst_cols = pl.ds(pl.multiple_of(p * hp, hp), hp)
        return [pltpu.make_async_copy(src.at[layer, page], dst.at[g, :, dst_cols], sems.at[s, g])
                for s, (src, dst) in enumerate(((ck_hbm, kp), (cv_hbm, vp), (cik_hbm, ikp)))]

    def start_all(i, c):
        g = i // n_pages
        for cp in page_copies(g, i - g * n_pages):
            cp.start()
        return c

    def wait_seq(g):
        def wait_page(p, c):
            for cp in page_copies(g, p):
                cp.wait()
            return c
        lax.fori_loop(0, n_pages, wait_page, 0)

    lax.fori_loop(0, G * n_pages, start_all, 0)

    def padded_new(ref, rows):
        return jnp.concatenate([ref[rows, :], jnp.zeros((LANES - T, dh), F32)], axis=0).astype(BF16)

    scale = dh ** -0.5
    q = q_ref[...]
    q2 = q * q
    hi = q2.astype(BF16)
    lo = (q2 - hi.astype(F32)).astype(BF16)
    ss = _dot(hi, e_ref[...]) + _dot(lo, e_ref[...])
    qs_scr[...] = q * lax.rsqrt(ss * (1.0 / dh) + EPS) * gq_ref[...] * scale
    iqs_scr[...] = iq_ref[...] * scale
    iw_blk = (iw_off // LANES) * LANES
    iw_lane = iw_off - iw_blk
    iw_scr[...] = tail_ref[:, iw_blk:iw_blk + LANES]

    new_ok = lax.broadcasted_iota(I32, (R, LANES), 1) <= lax.broadcasted_iota(I32, (R, LANES), 0) % T

    def stack_heads(ref, rows):
        return jnp.concatenate([ref[rows, h * dh:(h + 1) * dh] for h in range(H)], axis=0).astype(BF16)

    def score_body(g, c):
        wait_seq(g)
        rows = pl.ds(pl.multiple_of(g * T, T), T)
        qs = stack_heads(iqs_scr, rows)
        lg_p = _dot(qs, ikp[g].astype(BF16))
        lg_n = _dot_nt(qs, padded_new(ikn_ref, rows))
        iw = iw_scr[rows, :] * (IDX_HEADS ** -0.5)
        sp = sn = None
        for h in range(IDX_HEADS):
            w = iw[:, iw_lane + h:iw_lane + h + 1]
            hs = slice(h * T, (h + 1) * T)
            tp = jnp.maximum(lg_p[hs], 0.0) * w
            tn = jnp.maximum(lg_n[hs], 0.0) * w
            sp, sn = (tp, tn) if sp is None else (sp + tp, sn + tn)
        score[rows, 0:LP] = sp
        score[rows, LP:LC] = sn
        return c

    lax.fori_loop(0, G, score_body, 0, unroll=2)
    score[:, LP:LC] = jnp.where(new_ok, score[:, LP:LC], NEG_INF)

    key = _sort_keys(score[...])
    thr = _kth_largest_key(key, n_sel)
    gt = key > thr
    eq = key == thr
    need = float(n_sel) - jnp.sum(jnp.where(gt, 1.0, 0.0), axis=-1, keepdims=True)
    pref, _ = _blocked_prefix(jnp.where(eq, 1.0, 0.0), tri_ref[...], jnp.zeros((R, 1), F32))
    pref = jnp.concatenate(pref, axis=1)
    bias[...] = jnp.where(gt | (eq & (pref <= need)), 0.0, NEG_INF)
    bias[:, LP:LC] = jnp.where(new_ok, bias[:, LP:LC], NEG_INF)

    def attn_body(g, c):
        rows = pl.ds(pl.multiple_of(g * T, T), T)
        qs = stack_heads(qs_scr, rows)
        bh = jnp.concatenate([bias[rows, :]] * H, axis=0)
        lg_p = _dot(qs, kp[g].astype(BF16)) + bh[:, 0:LP]
        lg_n = _dot_nt(qs, padded_new(kn_ref, rows)) + bh[:, LP:LC]
        m = jnp.maximum(jnp.max(lg_p, axis=-1, keepdims=True), jnp.max(lg_n, axis=-1, keepdims=True))
        ep = jnp.exp(lg_p - m)
        en = jnp.exp(lg_n - m)
        den = jnp.sum(ep, axis=-1, keepdims=True) + jnp.sum(en, axis=-1, keepdims=True)
        o = (_dot_nt(ep.astype(BF16), vp[g].astype(BF16)) + _dot(en.astype(BF16), padded_new(vn_ref, rows))) / den
        for h in range(H):
            o_ref[rows, h * dh:(h + 1) * dh] = o[h * T:(h + 1) * T, :].astype(o_ref.dtype)
        return c

    lax.fori_loop(0, G, attn_body, 0, unroll=2)


def attn_sample(z, row0, q_block, iq_block, tail_block, tail_w, iw_off, kn, v, ik, caches, page_table, layer,
                gq_tiled, Bs, T, W, dh, G=16):
    n_pages = page_table.shape[1]
    page = caches[0].shape[2]
    caches_t = [c.transpose(0, 1, 3, 2) for c in caches]
    R = G * T
    rb0 = row0 // R
    LP = n_pages * page
    LC = LP + LANES
    n_sel = min(TOPK_MAX, (LP + T) // 4)
    eye = (jnp.arange(W)[:, None] // dh == jnp.arange(W)[None, :] // dh).astype(BF16)
    tri = (jnp.arange(LANES)[:, None] <= jnp.arange(LANES)[None, :]).astype(BF16)
    any_spec = pl.BlockSpec(memory_space=pl.ANY)
    nspec = pl.BlockSpec((R, dh), lambda i, pt: (rb0 + i, 0))
    grid_spec = pltpu.PrefetchScalarGridSpec(
        num_scalar_prefetch=1,
        grid=(Bs // G,),
        in_specs=[pl.BlockSpec((R, W), lambda i, pt: (rb0 + i, q_block)),
                  pl.BlockSpec((R, W), lambda i, pt: (rb0 + i, iq_block)),
                  pl.BlockSpec((R, tail_w), lambda i, pt: (rb0 + i, tail_block)),
                  nspec, nspec, nspec, any_spec, any_spec, any_spec,
                  pl.BlockSpec((1, W), lambda i, pt: (0, 0)),
                  pl.BlockSpec((W, W), lambda i, pt: (0, 0)),
                  pl.BlockSpec((LANES, LANES), lambda i, pt: (0, 0))],
        out_specs=pl.BlockSpec((R, W), lambda i, pt: (i, 0)),
        scratch_shapes=[pltpu.VMEM((G, dh, LP), F32), pltpu.VMEM((G, dh, LP), F32), pltpu.VMEM((G, dh, LP), F32),
                        pltpu.VMEM((R, W), F32), pltpu.VMEM((R, W), F32), pltpu.VMEM((R, LANES), F32),
                        pltpu.VMEM((R, LC), F32), pltpu.VMEM((R, LC), F32),
                        pltpu.SemaphoreType.DMA((3, G))],
    )
    return pl.pallas_call(
        functools.partial(_attn_sample_kernel, G=G, T=T, dh=dh, layer=layer, n_pages=n_pages, hp=page,
                          iw_off=iw_off, n_sel=n_sel),
        grid_spec=grid_spec,
        out_shape=jax.ShapeDtypeStruct((Bs * T, W), BF16),
        compiler_params=_cparams(("arbitrary",)),
        name="attn_sample",
    )(page_table, z, z, z, kn, v, ik, *caches_t, gq_tiled, eye, tri)


def _block_diag(blocks):
    G, a, b = blocks.shape
    eye = jnp.eye(G, dtype=blocks.dtype)
    return (eye[:, None, :, None] * blocks[:, :, None, :]).reshape(G * a, G * b)


def _lane_blocks(a):
    rows, n = a.shape
    return a.reshape(rows, n // LANES, LANES).transpose(1, 0, 2)


def kernel(x_prompt, x_sample, cache_attn_k, cache_attn_v, cache_idx_k, cache_mem_k, cache_mem_v, state_pool,
           state_conv, state_ssm_re, state_ssm_im, page_table, mem_prompt, norm_g, ffn_in, ffn_out, w_in,
           q_norm_g, k_norm_g, pool_mix, pool_scale, conv_w, ssm_a_re, ssm_a_im, ssm_log_step, ssm_b_re,
           ssm_b_im, ssm_c_re, ssm_c_im, ssm_d, ssm_glu_w, ssm_glu_b, w_branch, w_gate, b_gate, w_o,
           mem_norm_g, w_xq, w_xk, w_xv, xq_norm_g, xk_norm_g, w_xo):
    B, T, D = x_prompt.shape
    Bs, Ts, _ = x_sample.shape
    depth = norm_g.shape[0]
    Mp, Ms = B * T, Bs * Ts
    W = pool_scale.shape[1]
    dh = k_norm_g.shape[1]
    d_ff = ffn_out.shape[2]
    n_mem = mem_prompt.shape[1]
    Wx = w_xq.shape[2]
    SG, SN = ssm_a_re.shape[1:]
    NB = SG * SN // LANES
    past_len = page_table.shape[1] * cache_attn_k.shape[2]
    assert cache_idx_k.shape[-1] == dh and W == N_HEADS_C * dh == IDX_HEADS * dh
    TM = 1024

    o_k = 5 * W
    o_iq = o_k + 2 * dh
    o_ik = o_iq + W
    o_xs = o_ik + dh + IDX_HEADS
    tail_w = 2 * LANES
    Q_BLK, IQ_BLK, XS_BLK = 4, 5, 6
    TAIL_BLK = 7 * W // tail_w
    IW_OFF = 3 * dh
    n_z = 7 * W + tail_w

    h = (x_prompt.reshape(Mp, D), x_sample.reshape(Ms, D))
    caches = (cache_attn_k, cache_attn_v, cache_idx_k)
    mem_rows = mem_prompt.reshape(B * n_mem, D)
    outs = [[] for _ in range(16)]

    def ffn(h, l, i, g, split_out_rows=None):
        act = matmul(h, [(ffn_in, (l, i), 0), (ffn_in, (l, i), d_ff // 512)], n_out=d_ff, tn=512, tm=TM,
                     mode="swiglu", out_dtype=BF16, norm_g=g, name="ffn_in")
        return matmul(act, [(ffn_out, (l, i), 0)], n_out=D, tn=512, tm=TM, mode="residual", aux=h, scale=0.5,
                      split_out_rows=split_out_rows, name="ffn_out")

    for l in range(depth):
        h = ffn(h, l, 0, norm_g[l, 0])

        wl = w_in[l]
        w_in2 = jnp.concatenate([wl[:, 0:o_k], wl[:, o_iq:o_iq + W], wl[:, o_xs:o_xs + W], wl[:, o_k:o_k + 2 * dh],
                                 wl[:, o_ik:o_ik + dh + IDX_HEADS],
                                 jnp.zeros((D, tail_w - 3 * dh - IDX_HEADS), F32)], axis=1)
        z = matmul(h, [(w_in2, (), 0)], n_out=n_z, tn=768, tm=TM, norm_g=norm_g[l, 1], name="in_proj")

        wmix = pool_mix[l].astype(BF16)
        ya_p, yb_p, npool_p, nconv_p = pool_conv(z, 0, B, T, 1, 512, 0, None, None, wmix, pool_scale[l], conv_w[l], W)
        ya_s, yb_s, npool_s, nconv_s = pool_conv(z, Mp, Bs, Ts, 16, Ts, past_len, state_pool[l], state_conv[l],
                                                 wmix, pool_scale[l], conv_w[l], W)

        kn, vv, ik = kv_prep(z, TAIL_BLK, tail_w, k_norm_g[l], dh)
        gq_tiled = jnp.tile(q_norm_g[l], N_HEADS_C).reshape(1, W)
        yc_p = attn_prompt(z, Q_BLK, IQ_BLK, TAIL_BLK, tail_w, IW_OFF, kn[:Mp].reshape(B, T, dh),
                           vv[:Mp].reshape(B, T, dh), ik[:Mp].reshape(B, T, dh), gq_tiled, B, T, W, dh)
        yc_s = attn_sample(z, Mp, Q_BLK, IQ_BLK, TAIL_BLK, tail_w, IW_OFF, kn, vv, ik, caches, page_table, l,
                           gq_tiled, Bs, Ts, W, dh)

        ab_re, ab_im, co_re, co_im = ssm_params(ssm_a_re[l], ssm_a_im[l], ssm_log_step[l])
        flat = lambda a: a.reshape(1, SG * SN)
        lev_re, lev_im, pw_re, pw_im = ssm_tables(flat(ab_re), flat(ab_im), SUBLANES)
        bb_re, bb_im = ssm_fold_input(_block_diag(ssm_b_re[l].transpose(0, 2, 1)),
                                      _block_diag(ssm_b_im[l].transpose(0, 2, 1)), flat(co_re), flat(co_im))
        consts = (
            _lane_blocks(bb_re), _lane_blocks(bb_im),
            _block_diag(ssm_c_re[l].transpose(0, 2, 1)).reshape(NB, LANES, W).astype(BF16),
            _block_diag(ssm_c_im[l].transpose(0, 2, 1)).reshape(NB, LANES, W).astype(BF16),
            _lane_blocks(lev_re), _lane_blocks(lev_im),
            _lane_blocks(pw_re), _lane_blocks(pw_im),
            ssm_d[l].reshape(1, W), ssm_glu_w[l].astype(BF16), ssm_glu_b[l].reshape(1, W))
        yd_p, sre_p, sim_p = ssm_mixer(z, XS_BLK, 0, B, T, consts, None)
        h0 = (state_ssm_re[l].reshape(Bs, NB, LANES), state_ssm_im[l].reshape(Bs, NB, LANES))
        yd_s, sre_s, sim_s = ssm_mixer(z, XS_BLK, Mp, Bs, Ts, consts, h0)

        merged = gated_merge(h, norm_g[l, 1], [(ya_p, ya_s), (yb_p, yb_s), (yc_p, yc_s), (yd_p, yd_s)],
                             w_gate, b_gate, w_branch, l)
        h = matmul(merged, [(w_o, (l,), 0)], n_out=D, tn=512, tm=TM, mode="residual", aux=h, scale=1.0, name="w_o")

        mn = rmsnorm_rows(mem_rows, mem_norm_g[l])
        mk = matmul(mn, [(w_xk, (l,), 0)], n_out=Wx, tn=Wx, tm=B * n_mem, mode="headnorm",
                    aux=jnp.tile(xk_norm_g[l], X_HEADS).reshape(1, Wx), group=Wx // X_HEADS, name="mem_k")
        mv = matmul(mn, [(w_xv, (l,), 0)], n_out=Wx, tn=Wx, tm=B * n_mem, name="mem_v")
        qx = matmul(h, [(w_xq, (l,), 0)], n_out=Wx, tn=Wx, tm=TM, norm_g=norm_g[l, 2], name="w_xq")
        xa_p = cross_attention(qx, 0, B, T, 1, 512, mk.reshape(1, B, n_mem, Wx), mv.reshape(1, B, n_mem, Wx), 0,
                               xq_norm_g[l])
        xa_s = cross_attention(qx, Mp, Bs, Ts, 8, Ts, cache_mem_k.reshape(depth, Bs, n_mem * X_HEADS, Wx // X_HEADS),
                               cache_mem_v.reshape(depth, Bs, n_mem * X_HEADS, Wx // X_HEADS), l, xq_norm_g[l])
        h = matmul((xa_p, xa_s), [(w_xo, (l,), 0)], n_out=D, tn=512, tm=TM, mode="residual", aux=h, scale=1.0,
                   name="w_xo")

        h = ffn(h, l, 1, norm_g[l, 3], split_out_rows=Mp if l == depth - 1 else None)

        xh = Wx // X_HEADS
        layer_out = (kn[:Mp].reshape(B, T, dh), vv[:Mp].reshape(B, T, dh), ik[:Mp].reshape(B, T, dh),
                     mk.reshape(B, n_mem, X_HEADS, xh), mv.reshape(B, n_mem, X_HEADS, xh), npool_p, nconv_p,
                     sre_p.reshape(B, SG, SN), sim_p.reshape(B, SG, SN),
                     kn[Mp:].reshape(Bs, Ts, dh), vv[Mp:].reshape(Bs, Ts, dh), ik[Mp:].reshape(Bs, Ts, dh),
                     npool_s, nconv_s, sre_s.reshape(Bs, SG, SN), sim_s.reshape(Bs, SG, SN))
        for acc, val in zip(outs, layer_out):
            acc.append(val)

    return (h[0].reshape(B, T, D), h[1].reshape(Bs, Ts, D)) + tuple(jnp.stack(o) for o in outs)
```

```python
import functools
import math

import jax
import jax.numpy as jnp
from jax import lax
from jax.experimental import pallas as pl
from jax.experimental.pallas import tpu as pltpu

F32, BF16, I32 = jnp.float32, jnp.bfloat16, jnp.int32
EPS = 1e-6
NEG_INF = float("-inf")

V7X_VMEM_BYTES = 64 * 1024 * 1024
VMEM_LIMIT = V7X_VMEM_BYTES - 8 * 1024 * 1024
LANES = 128
SUBLANES = 8

POOL_WINDOWS = (2, 4, 8, 16)
POOL_HIST = 16
CONV_WIDTH = 3
CONV_HIST = 8
N_HEADS_C = 8
IDX_HEADS = 8
TOPK_MAX = 256
X_HEADS = 4
SSM_CHUNK = 256
SSM_SEQ_ROWS = 128


def _cparams(sem, vmem=VMEM_LIMIT):
    return pltpu.CompilerParams(dimension_semantics=sem, vmem_limit_bytes=vmem)


def _dot(a, b):
    return jnp.dot(a, b, preferred_element_type=F32)


def _dot_nt(a, b):
    return lax.dot_general(a, b, (((1,), (1,)), ((), ())), preferred_element_type=F32)


def _rmsnorm_kernel(x_ref, g_ref, o_ref):
    x = x_ref[...]
    ms = jnp.mean(x * x, axis=-1, keepdims=True)
    o_ref[...] = (x * lax.rsqrt(ms + EPS) * g_ref[...]).astype(o_ref.dtype)


def rmsnorm_rows(x, g, tm=512):
    M, D = x.shape
    return pl.pallas_call(
        _rmsnorm_kernel,
        grid=(M // tm,),
        in_specs=[pl.BlockSpec((tm, D), lambda i: (i, 0)), pl.BlockSpec((1, D), lambda i: (0, 0))],
        out_specs=pl.BlockSpec((tm, D), lambda i: (i, 0)),
        out_shape=jax.ShapeDtypeStruct((M, D), BF16),
        compiler_params=_cparams(("arbitrary",)),
        name="rmsnorm",
    )(x, g.reshape(1, D))


def _rms_rows_bf16(x, g):
    ms = jnp.mean(x * x, axis=-1, keepdims=True)
    return (x * lax.rsqrt(ms + EPS) * g).astype(BF16)


def _pick_rows(ref_a, ref_b, rows, n_a_tiles):
    if ref_b is None:
        return ref_a[rows, :]
    return jnp.where(pl.program_id(1) >= n_a_tiles, ref_b[rows, :], ref_a[rows, :])


def _split_row_specs(xa, xb, tm, tn=None):
    na = xa.shape[0] // tm
    width = xa.shape[1] if tn is None else tn
    col = (lambda j: 0) if tn is None else (lambda j: j)
    spec_a = pl.BlockSpec((tm, width), lambda j, i: (jnp.minimum(i, na - 1), col(j)))
    spec_b = pl.BlockSpec((tm, width), lambda j, i: (jnp.maximum(i - na, 0), col(j)),
                          pipeline_mode=pl.Buffered(1) if (xb.shape[0] == tm and tn is None) else None)
    return spec_a, spec_b, na


def _mm_kernel(*refs, nw, mode, scale, sub, group, norm, n_a_tiles, aux_a_tiles, out_a_tiles):
    x_ref = refs[0]
    xb_ref = None
    if n_a_tiles is not None:
        xb_ref = refs[1]
        refs = refs[1:]
    w_refs = refs[1:1 + nw]
    p = 1 + nw
    if norm:
        g_ref = refs[p]
        p += 1
    aux_ref = auxb_ref = None
    if mode in ("bias_sigmoid", "residual", "headnorm"):
        aux_ref = refs[p]
        p += 1
        if aux_a_tiles is not None:
            auxb_ref = refs[p]
            p += 1
    o_ref = refs[p]
    ob_ref = None
    if out_a_tiles is not None:
        ob_ref = refs[p + 1]
        p += 1
    scr = refs[p + 1:p + 1 + nw]

    @pl.when(pl.program_id(1) == 0)
    def _():
        for w_ref, s in zip(w_refs, scr):
            s[...] = w_ref[...].astype(BF16)

    tm = x_ref.shape[0]

    def body(r, carry):
        rows = pl.ds(pl.multiple_of(r * sub, sub), sub)
        x = _pick_rows(x_ref, xb_ref, rows, n_a_tiles)
        if norm:
            x = _rms_rows_bf16(x, g_ref[...])
        acc = [_dot(x, s[...]) for s in scr]
        if mode == "swiglu":
            a, b = acc
            y = (a * jax.nn.sigmoid(a)) * b
        elif mode == "bias_sigmoid":
            y = jax.nn.sigmoid(acc[0] + aux_ref[...])
        elif mode == "residual":
            y = _pick_rows(aux_ref, auxb_ref, rows, aux_a_tiles) + scale * acc[0]
        elif mode == "headnorm":
            a = acc[0]
            parts = []
            for h in range(a.shape[1] // group):
                ah = a[:, h * group:(h + 1) * group]
                ms = jnp.mean(ah * ah, axis=-1, keepdims=True)
                parts.append(ah * lax.rsqrt(ms + EPS))
            y = jnp.concatenate(parts, axis=1) * aux_ref[...]
        else:
            y = acc[0]
        if ob_ref is None:
            o_ref[rows, :] = y.astype(o_ref.dtype)
        else:
            @pl.when(pl.program_id(1) < out_a_tiles)
            def _():
                o_ref[rows, :] = y.astype(o_ref.dtype)

            @pl.when(pl.program_id(1) >= out_a_tiles)
            def _():
                ob_ref[rows, :] = y.astype(ob_ref.dtype)
        return carry

    lax.fori_loop(0, tm // sub, body, 0, unroll=True)


def matmul(x, weights, *, n_out, tn, tm, mode="plain", aux=None, scale=1.0, out_dtype=F32, group=LANES,
           norm_g=None, split_out_rows=None, name="matmul"):
    nw = len(weights)
    sub = min(tm, 256)
    n_a_tiles = None
    if isinstance(x, tuple):
        xa, xb = x
        spec_a, spec_b, n_a_tiles = _split_row_specs(xa, xb, tm)
        M, K = xa.shape[0] + xb.shape[0], xa.shape[1]
        in_specs = [spec_a, spec_b]
        args = [xa, xb]
    else:
        M, K = x.shape
        in_specs = [pl.BlockSpec((tm, K), lambda j, i: (i, 0))]
        args = [x]
    for arr, lead, coff in weights:
        nl = len(lead)
        in_specs.append(pl.BlockSpec((None,) * nl + (K, tn),
                                     functools.partial(lambda j, i, lead, coff: (*lead, 0, coff + j), lead=lead, coff=coff)))
        args.append(arr)
    if norm_g is not None:
        in_specs.append(pl.BlockSpec((1, K), lambda j, i: (0, 0)))
        args.append(norm_g.reshape(1, K))
    if mode in ("bias_sigmoid", "headnorm"):
        in_specs.append(pl.BlockSpec((1, tn), lambda j, i: (0, j)))
        args.append(aux)
    aux_a_tiles = None
    if mode == "residual" and isinstance(aux, tuple):
        spec_a, spec_b, aux_a_tiles = _split_row_specs(aux[0], aux[1], tm, tn)
        in_specs += [spec_a, spec_b]
        args += list(aux)
    elif mode == "residual":
        in_specs.append(pl.BlockSpec((tm, tn), lambda j, i: (i, j)))
        args.append(aux)
    out_a_tiles = None
    out_specs = pl.BlockSpec((tm, tn), lambda j, i: (i, j))
    out_shape = jax.ShapeDtypeStruct((M, n_out), out_dtype)
    if split_out_rows is not None:
        out_a_tiles = split_out_rows // tm
        out_specs = [pl.BlockSpec((tm, tn), lambda j, i: (jnp.minimum(i, out_a_tiles - 1), j)),
                     pl.BlockSpec((tm, tn), lambda j, i: (jnp.maximum(i - out_a_tiles, 0), j))]
        out_shape = [jax.ShapeDtypeStruct((split_out_rows, n_out), out_dtype),
                     jax.ShapeDtypeStruct((M - split_out_rows, n_out), out_dtype)]
    return pl.pallas_call(
        functools.partial(_mm_kernel, nw=nw, mode=mode, scale=scale, sub=sub, group=group, norm=norm_g is not None,
                          n_a_tiles=n_a_tiles, aux_a_tiles=aux_a_tiles, out_a_tiles=out_a_tiles),
        grid=(n_out // tn, M // tm),
        in_specs=in_specs,
        out_specs=out_specs,
        out_shape=out_shape,
        scratch_shapes=[pltpu.VMEM((K, tn), BF16) for _ in range(nw)],
        compiler_params=_cparams(("arbitrary", "arbitrary")),
        name=name,
    )(*args)


def _merge_kernel(*refs, nb, n_a_tiles):
    h_ref, g_ref = refs[0:2]
    br = refs[2:2 + nb]
    brb = refs[2 + nb:2 + 2 * nb]
    refs = refs[nb:]
    wg = refs[2 + nb:2 + 2 * nb]
    bg = refs[2 + 2 * nb:2 + 3 * nb]
    wb = refs[2 + 3 * nb:2 + 4 * nb]
    o_ref = refs[2 + 4 * nb]
    sg = refs[3 + 4 * nb:3 + 5 * nb]
    sb = refs[3 + 5 * nb:3 + 6 * nb]

    @pl.when(pl.program_id(1) == 0)
    def _():
        for w_ref, s in zip(wg + wb, sg + sb):
            s[...] = w_ref[...].astype(BF16)

    tm = o_ref.shape[0]
    sub = min(tm, 256)

    def body(r, carry):
        rows = pl.ds(pl.multiple_of(r * sub, sub), sub)
        u = _rms_rows_bf16(h_ref[rows, :], g_ref[...])
        acc = None
        for k in range(nb):
            gate = jax.nn.sigmoid(_dot(u, sg[k][...]) + bg[k][...])
            t = gate * _dot(_pick_rows(br[k], brb[k], rows, n_a_tiles), sb[k][...])
            acc = t if acc is None else acc + t
        o_ref[rows, :] = acc.astype(o_ref.dtype)
        return carry

    lax.fori_loop(0, tm // sub, body, 0, unroll=True)


def gated_merge(h, norm_g, branches, w_gate, b_gate, w_branch, layer, *, tn=256, tm=1024):
    nb = len(branches)
    M = h.shape[0]
    W = branches[0][0].shape[1]
    D = w_branch.shape[-1]
    nj = D // tn
    bias = b_gate.reshape(b_gate.shape[0], 1, nb * D)

    def per_branch(shape):
        return [pl.BlockSpec(shape, functools.partial(lambda j, i, k: (layer, 0, k * nj + j), k=k)) for k in range(nb)]

    row_specs = [_split_row_specs(a, b, tm) for a, b in branches]
    n_a_tiles = row_specs[0][2]
    in_specs = [pl.BlockSpec((tm, D), lambda j, i: (i, 0)), pl.BlockSpec((1, D), lambda j, i: (0, 0))]
    in_specs += [s[0] for s in row_specs] + [s[1] for s in row_specs]
    in_specs += per_branch((None, D, tn)) + per_branch((None, 1, tn))
    in_specs += [pl.BlockSpec((None, None, W, tn), functools.partial(lambda j, i, k: (layer, k, 0, j), k=k))
                 for k in range(nb)]
    return pl.pallas_call(
        functools.partial(_merge_kernel, nb=nb, n_a_tiles=n_a_tiles),
        grid=(nj, M // tm),
        in_specs=in_specs,
        out_specs=pl.BlockSpec((tm, tn), lambda j, i: (i, j)),
        out_shape=jax.ShapeDtypeStruct((M, D), BF16),
        scratch_shapes=[pltpu.VMEM((D, tn), BF16) for _ in range(nb)] + [pltpu.VMEM((W, tn), BF16) for _ in range(nb)],
        compiler_params=_cparams(("arbitrary", "arbitrary")),
        name="gated_merge",
    )(h, norm_g.reshape(1, D), *[a for a, _ in branches], *[b for _, b in branches],
      *([w_gate] * nb), *([bias] * nb), *([w_branch] * nb))


def _kvprep_kernel(t_ref, g_ref, k_ref, v_ref, ik_ref, *, dh):
    t = t_ref[...]
    k = t[:, 0:dh]
    ms = jnp.mean(k * k, axis=-1, keepdims=True)
    k_ref[...] = k * lax.rsqrt(ms + EPS) * g_ref[...]
    v_ref[...] = t[:, dh:2 * dh]
    ik_ref[...] = t[:, 2 * dh:3 * dh]


def kv_prep(z, tail_block, tail_w, gk, dh, tm=1024):
    M = z.shape[0]
    out = jax.ShapeDtypeStruct((M, dh), F32)
    return pl.pallas_call(
        functools.partial(_kvprep_kernel, dh=dh),
        grid=(M // tm,),
        in_specs=[pl.BlockSpec((tm, tail_w), lambda i: (i, tail_block)), pl.BlockSpec((1, dh), lambda i: (0, 0))],
        out_specs=[pl.BlockSpec((tm, dh), lambda i: (i, 0))] * 3,
        out_shape=[out, out, out],
        compiler_params=_cparams(("arbitrary",)),
        name="kv_prep",
    )(z, gk.reshape(1, dh))


def _poolconv_kernel(*refs, G, T, W, pos0, has_state):
    xp_ref, xc_ref, bg_ref, cg_ref = refs[0:4]
    p = 4
    if has_state:
        pbuf_ref, cbuf_ref = refs[4:6]
        p = 6
    wmix_ref, pscale_ref, convw_ref = refs[p:p + 3]
    ya_ref, yb_ref, npool_ref, nconv_ref = refs[p + 3:p + 7]
    fullp, fullc = refs[p + 7:p + 9]
    PH, CH = POOL_HIST, CONV_HIST
    nh = CONV_WIDTH - 1
    gw = W // len(POOL_WINDOWS)
    c = pl.program_id(1)

    @pl.when(c == 0)
    def _():
        if has_state:
            fullp[:, 1:PH, :] = pbuf_ref[...]
            fullc[:, CH - nh:CH, :] = cbuf_ref[...]
        else:
            fullp[:, 0:PH, :] = jnp.zeros((G, PH, W), F32)
            fullc[:, 0:CH, :] = jnp.zeros((G, CH, W), F32)

    @pl.when(c > 0)
    def _():
        fullp[:, 0:PH, :] = fullp[:, T:T + PH, :]
        fullc[:, 0:CH, :] = fullc[:, T:T + CH, :]

    fullp[:, PH:PH + T, :] = xp_ref[...].reshape(G, T, W)
    t_idx = lax.broadcasted_iota(I32, (1, T, 1), 1) + (c * T + (pos0 + 1))
    for gi, w in enumerate(POOL_WINDOWS):
        cols = slice(gi * gw, (gi + 1) * gw)
        acc = fullp[:, PH:PH + T, cols]
        for j in range(1, w):
            acc = acc + fullp[:, PH - j:PH - j + T, cols]
        cnt = jnp.minimum(t_idx, w).astype(F32)
        d = acc / cnt - fullp[:, PH:PH + T, cols]
        y = _dot(d.reshape(G * T, gw).astype(BF16), wmix_ref[gi]) * pscale_ref[:, cols]
        ya_ref[:, cols] = y.astype(ya_ref.dtype)
    npool_ref[...] = fullp[:, T + 1:T + PH, :]

    fullc[:, CH:CH + T, :] = (cg_ref[...] * xc_ref[...]).reshape(G, T, W)
    y = None
    for j in range(CONV_WIDTH):
        wj = convw_ref[j:j + 1, :].reshape(1, 1, W)
        term = wj * fullc[:, CH - nh + j:CH - nh + j + T, :]
        y = term if y is None else y + term
    yb_ref[...] = (bg_ref[...] * y.reshape(G * T, W)).astype(yb_ref.dtype)
    nconv_ref[...] = fullc[:, CH + T - nh:CH + T, :]


def pool_conv(z, row0, Bt, T, G, TC, pos0, pool_buf, conv_buf, wmix_bf16, pool_scale, conv_w, W):
    has_state = pool_buf is not None
    assert G == 1 or TC == T
    R = G * TC
    nc = T // TC
    rb0 = row0 // R
    nh = CONV_WIDTH - 1

    def zspec(cb):
        return pl.BlockSpec((R, W), functools.partial(lambda i, c, cb: (rb0 + i * nc + c, cb), cb=cb))

    in_specs = [zspec(0), zspec(1), zspec(2), zspec(3)]
    args = [z, z, z, z]
    if has_state:
        in_specs += [pl.BlockSpec((G, POOL_HIST - 1, W), lambda i, c: (i, 0, 0)),
                     pl.BlockSpec((G, nh, W), lambda i, c: (i, 0, 0))]
        args += [pool_buf, conv_buf]
    nwin = len(POOL_WINDOWS)
    in_specs += [pl.BlockSpec((nwin, W // nwin, W // nwin), lambda i, c: (0, 0, 0)),
                 pl.BlockSpec((1, W), lambda i, c: (0, 0)),
                 pl.BlockSpec((CONV_WIDTH, W), lambda i, c: (0, 0))]
    args += [wmix_bf16, pool_scale.reshape(1, W), conv_w]
    return pl.pallas_call(
        functools.partial(_poolconv_kernel, G=G, T=TC, W=W, pos0=pos0, has_state=has_state),
        grid=(Bt // G, nc),
        in_specs=in_specs,
        out_specs=[pl.BlockSpec((R, W), lambda i, c: (i * nc + c, 0)), pl.BlockSpec((R, W), lambda i, c: (i * nc + c, 0)),
                   pl.BlockSpec((G, POOL_HIST - 1, W), lambda i, c: (i, 0, 0)),
                   pl.BlockSpec((G, nh, W), lambda i, c: (i, 0, 0))],
        out_shape=[jax.ShapeDtypeStruct((Bt * T, W), BF16), jax.ShapeDtypeStruct((Bt * T, W), BF16),
                   jax.ShapeDtypeStruct((Bt, POOL_HIST - 1, W), F32), jax.ShapeDtypeStruct((Bt, nh, W), F32)],
        scratch_shapes=[pltpu.VMEM((G, POOL_HIST + TC, W), F32), pltpu.VMEM((G, CONV_HIST + TC, W), F32)],
        compiler_params=_cparams(("arbitrary", "arbitrary")),
        name="pool_conv",
    )(*args)


def _ssm_params_kernel(are_ref, aim_ref, ls_ref, abre_ref, abim_ref, core_ref, coim_ref):
    a_re = are_ref[...]
    a_im = aim_ref[...]
    step = jnp.exp(ls_ref[...])
    decay = jnp.exp(step * a_re)
    ab_re = decay * jnp.cos(step * a_im)
    ab_im = decay * jnp.sin(step * a_im)
    den = a_re * a_re + a_im * a_im
    nr = ab_re - 1.0
    abre_ref[...] = ab_re
    abim_ref[...] = ab_im
    core_ref[...] = (nr * a_re + ab_im * a_im) / den
    coim_ref[...] = (ab_im * a_re - nr * a_im) / den


def ssm_params(a_re, a_im, log_step):
    G, N = a_re.shape
    out = jax.ShapeDtypeStruct((G, N), F32)
    return pl.pallas_call(_ssm_params_kernel, out_shape=[out, out, out, out], name="ssm_params")(
        a_re, a_im, log_step.reshape(G, 1))


def _scan_levels(C):
    return [1 << k for k in range(int(math.log2(C)))]


def _ssm_tables_kernel(abre_ref, abim_ref, lre_ref, lim_ref, pre_ref, pim_ref, *, C):
    ar = abre_ref[...]
    ai = abim_ref[...]
    N = ar.shape[1]
    row = lax.broadcasted_iota(I32, (C, N), 0)
    hr = jnp.where(row == 0, ar, 0.0)
    hi = jnp.where(row == 0, ai, 0.0)
    lre_ref[...] = jnp.zeros(lre_ref.shape, F32)
    lim_ref[...] = jnp.zeros(lim_ref.shape, F32)
    for k, s in enumerate(_scan_levels(C)):
        lre_ref[k:k + 1, :] = ar
        lim_ref[k:k + 1, :] = ai
        sr = jnp.where(row >= s, pltpu.roll(hr, s, 0), 0.0)
        si = jnp.where(row >= s, pltpu.roll(hi, s, 0), 0.0)
        hr, hi = hr + ar * sr - ai * si, hi + ar * si + ai * sr
        ar, ai = ar * ar - ai * ai, 2.0 * ar * ai
    pre_ref[...] = hr
    pim_ref[...] = hi


def ssm_tables(ab_re, ab_im, C):
    N = ab_re.shape[1]
    nlev = len(_scan_levels(C))
    lev = jax.ShapeDtypeStruct((SUBLANES * ((nlev + SUBLANES - 1) // SUBLANES), N), F32)
    pw = jax.ShapeDtypeStruct((C, N), F32)
    return pl.pallas_call(functools.partial(_ssm_tables_kernel, C=C), out_shape=[lev, lev, pw, pw],
                          name="ssm_tables")(ab_re, ab_im)


def _ssm_fold_kernel(bre_ref, bim_ref, core_ref, coim_ref, ore_ref, oim_ref):
    bre = bre_ref[...]
    bim = bim_ref[...]
    cor = core_ref[...]
    coi = coim_ref[...]
    ore_ref[...] = (cor * bre - coi * bim).astype(ore_ref.dtype)
    oim_ref[...] = (cor * bim + coi * bre).astype(oim_ref.dtype)


def ssm_fold_input(b_re_bd, b_im_bd, co_re, co_im):
    out = jax.ShapeDtypeStruct(b_re_bd.shape, BF16)
    return pl.pallas_call(_ssm_fold_kernel, out_shape=[out, out], name="ssm_fold")(b_re_bd, b_im_bd, co_re, co_im)


def _gelu_tanh(x):
    return 0.5 * x * (1.0 + jnp.tanh(math.sqrt(2.0 / math.pi) * (x + 0.044715 * (x * x * x))))


def _ssm_kernel(*refs, R, T, NB, chained):
    (xs_ref, bre_ref, bim_ref, cre_ref, cim_ref, lre_ref, lim_ref) = refs[0:7]
    p = 7
    if chained:
        pre_ref, pim_ref = refs[p:p + 2]
        p += 2
    else:
        h0r_ref, h0i_ref = refs[p:p + 2]
        p += 2
    d_ref, gw_ref, gb_ref = refs[p:p + 3]
    yd_ref, sre_ref, sim_ref = refs[p + 3:p + 6]
    p += 6
    if chained:
        car_ref, cai_ref = refs[p:p + 2]
    G = R // T
    xs = xs_ref[...]
    xb = xs.astype(BF16)
    tpos = lax.broadcasted_iota(I32, (R, LANES), 0) % SUBLANES
    sub_pos = lax.broadcasted_iota(I32, (SUBLANES, LANES), 0)
    levels = _scan_levels(SUBLANES)

    if chained:
        c = pl.program_id(1)

        @pl.when(c == 0)
        def _():
            car_ref[...] = jnp.zeros(car_ref.shape, F32)
            cai_ref[...] = jnp.zeros(cai_ref.shape, F32)

    y = d_ref[...] * xs
    for cb in range(NB):
        hr = _dot(xb, bre_ref[cb])
        hi = _dot(xb, bim_ref[cb])
        if not chained:
            ar = lre_ref[cb][0:1, :]
            ai = lim_ref[cb][0:1, :]
            h0r = jnp.broadcast_to(h0r_ref[:, cb:cb + 1, :], (G, T, LANES)).reshape(R, LANES)
            h0i = jnp.broadcast_to(h0i_ref[:, cb:cb + 1, :], (G, T, LANES)).reshape(R, LANES)
            first = tpos == 0
            hr = hr + jnp.where(first, ar * h0r - ai * h0i, 0.0)
            hi = hi + jnp.where(first, ar * h0i + ai * h0r, 0.0)
        for k, s in enumerate(levels):
            ar = jnp.where(sub_pos >= s, lre_ref[cb][k:k + 1, :], 0.0)
            ai = jnp.where(sub_pos >= s, lim_ref[cb][k:k + 1, :], 0.0)
            ar = jnp.concatenate([ar] * (R // SUBLANES), axis=0)
            ai = jnp.concatenate([ai] * (R // SUBLANES), axis=0)
            sr = pltpu.roll(hr, s, 0)
            si = pltpu.roll(hi, s, 0)
            hr, hi = hr + ar * sr - ai * si, hi + ar * si + ai * sr
        if chained:
            cr = car_ref[cb:cb + 1, :]
            ci = cai_ref[cb:cb + 1, :]
            pwr = pre_ref[cb]
            pwi = pim_ref[cb]
            grs, gis = [], []
            for v in range(R // SUBLANES):
                rows = slice(v * SUBLANES, (v + 1) * SUBLANES)
                cbr = jnp.broadcast_to(cr, (SUBLANES, LANES))
                cbi = jnp.broadcast_to(ci, (SUBLANES, LANES))
                gr = hr[rows] + pwr * cbr - pwi * cbi
                gi = hi[rows] + pwr * cbi + pwi * cbr
                cr = gr[SUBLANES - 1:SUBLANES, :]
                ci = gi[SUBLANES - 1:SUBLANES, :]
                grs.append(gr)
                gis.append(gi)
            hr = jnp.concatenate(grs, axis=0)
            hi = jnp.concatenate(gis, axis=0)
            car_ref[cb:cb + 1, :] = cr
            cai_ref[cb:cb + 1, :] = ci
        else:
            sre_ref[:, cb:cb + 1, :] = hr.reshape(G, T, LANES)[:, T - 1:T, :]
            sim_ref[:, cb:cb + 1, :] = hi.reshape(G, T, LANES)[:, T - 1:T, :]
        y = y + _dot(hr.astype(BF16), cre_ref[cb]) - _dot(hi.astype(BF16), cim_ref[cb])

    z = _gelu_tanh(y)
    out = z * jax.nn.sigmoid(_dot(z.astype(BF16), gw_ref[...]) + gb_ref[...])
    yd_ref[...] = out.astype(yd_ref.dtype)

    if chained:
        @pl.when(c == pl.num_programs(1) - 1)
        def _():
            sre_ref[0] = car_ref[...]
            sim_ref[0] = cai_ref[...]


def ssm_mixer(z, xs_block, row0, Bt, T, consts, h0):
    (bre3, bim3, cre3, cim3, lre3, lim3, pre3, pim3, dvec, gw, gb) = consts
    NB = bre3.shape[0]
    W = dvec.shape[1]
    chained = h0 is None
    assert chained or T == SUBLANES
    R = SSM_CHUNK if chained else SSM_SEQ_ROWS
    rb0 = row0 // R

    def full(a):
        nd = a.ndim
        return pl.BlockSpec(a.shape, lambda *_: (0,) * nd)

    if chained:
        nchunk = T // R
        grid = (Bt, nchunk)
        xs_spec = pl.BlockSpec((R, W), lambda b, c: (rb0 + b * nchunk + c, xs_block))
        st_args, st_specs = [pre3, pim3], [full(pre3), full(pim3)]
        yd_spec = pl.BlockSpec((R, W), lambda b, c: (b * nchunk + c, 0))
        s_spec = pl.BlockSpec((1, NB, LANES), lambda b, c: (b, 0, 0))
        scratch = [pltpu.VMEM((NB, LANES), F32), pltpu.VMEM((NB, LANES), F32)]
        sem = ("arbitrary", "arbitrary")
        Tk = R
    else:
        G = R // T
        grid = (Bt // G,)
        xs_spec = pl.BlockSpec((R, W), lambda i: (rb0 + i, xs_block))
        st_args = list(h0)
        st_specs = [pl.BlockSpec((G, NB, LANES), lambda i: (i, 0, 0))] * 2
        yd_spec = pl.BlockSpec((R, W), lambda i: (i, 0))
        s_spec = pl.BlockSpec((G, NB, LANES), lambda i: (i, 0, 0))
        scratch = []
        sem = ("arbitrary",)
        Tk = T
    shared = [bre3, bim3, cre3, cim3, lre3, lim3]
    tailc = [dvec, gw, gb]
    s_shape = jax.ShapeDtypeStruct((Bt, NB, LANES), F32)
    return pl.pallas_call(
        functools.partial(_ssm_kernel, R=R, T=Tk, NB=NB, chained=chained),
        grid=grid,
        in_specs=[xs_spec] + [full(a) for a in shared] + st_specs + [full(a) for a in tailc],
        out_specs=[yd_spec, s_spec, s_spec],
        out_shape=[jax.ShapeDtypeStruct((Bt * T, W), BF16), s_shape, s_shape],
        scratch_shapes=scratch,
        compiler_params=_cparams(sem),
        name="ssm_chained" if chained else "ssm_stateful",
    )(z, *shared, *st_args, *tailc)


def _sort_keys(score):
    b = pltpu.bitcast(score, I32)
    key = jnp.where(b < 0, b ^ jnp.int32(0x7FFFFFFF), b)
    return jnp.where(key == -1, 0, key)


def _kth_largest_key(key, k):
    rows = key.shape[0]

    def body(i, t):
        cand = t + (jnp.int32(1) << (31 - i))
        cnt = jnp.sum(jnp.where(key >= cand, 1.0, 0.0), axis=-1, keepdims=True)
        return jnp.where(cnt >= float(k), cand, t)

    return lax.fori_loop(0, 32, body, jnp.full((rows, 1), -2 ** 31, I32))


def _blocked_prefix(eq, tri, offset):
    outs = []
    run = offset
    for j in range(eq.shape[1] // LANES):
        blk = eq[:, j * LANES:(j + 1) * LANES].astype(BF16)
        pj = _dot(blk, tri) + run
        outs.append(pj)
        run = pj[:, LANES - 1:LANES]
    return outs, run


def _attn_prompt_kernel(q_ref, iq_ref, tail_ref, k_ref, v_ref, ik_ref, gq_ref, e_ref, tri_ref, o_ref,
                        kb, vb, ikb, bias, *, TQ, L, dh, iw_off, n_sel, n_buckets):
    qb = pl.program_id(1)

    @pl.when(qb == 0)
    def _():
        kb[...] = k_ref[0].astype(BF16)
        vb[...] = v_ref[0].astype(BF16)
        ikb[...] = ik_ref[0].astype(BF16)

    scale = dh ** -0.5
    q = q_ref[...]
    q2 = q * q
    hi = q2.astype(BF16)
    lo = (q2 - hi.astype(F32)).astype(BF16)
    ss = _dot(hi, e_ref[...]) + _dot(lo, e_ref[...])
    qn = (q * lax.rsqrt(ss * (1.0 / dh) + EPS) * gq_ref[...] * scale).astype(BF16)
    iqs = (iq_ref[...] * scale).astype(BF16)
    iw = tail_ref[:, iw_off:iw_off + IDX_HEADS] * (IDX_HEADS ** -0.5)

    def attend(Lk):
        score = None
        for h in range(IDX_HEADS):
            lg = _dot_nt(iqs[:, h * dh:(h + 1) * dh], ikb[0:Lk, :])
            t = jnp.maximum(lg, 0.0) * iw[:, h:h + 1]
            score = t if score is None else score + t
        col = lax.broadcasted_iota(I32, (TQ, Lk), 1)
        qpos = qb * TQ + lax.broadcasted_iota(I32, (TQ, Lk), 0)
        causal = col <= qpos
        score = jnp.where(causal, score, NEG_INF)

        key = _sort_keys(score)
        thr = _kth_largest_key(key, n_sel)
        gt = key > thr
        eq = key == thr
        need = float(n_sel) - jnp.sum(jnp.where(gt, 1.0, 0.0), axis=-1, keepdims=True)
        pref, _ = _blocked_prefix(jnp.where(eq, 1.0, 0.0), tri_ref[...], jnp.zeros((TQ, 1), F32))
        pref = jnp.concatenate(pref, axis=1)
        sel = (gt | (eq & (pref <= need))) & causal
        bias[:, 0:Lk] = jnp.where(sel, 0.0, NEG_INF)

        for h in range(N_HEADS_C):
            lg = _dot_nt(qn[:, h * dh:(h + 1) * dh], kb[0:Lk, :]) + bias[:, 0:Lk]
            m = jnp.max(lg, axis=-1, keepdims=True)
            e = jnp.exp(lg - m)
            den = jnp.sum(e, axis=-1, keepdims=True)
            o = _dot(e.astype(BF16), vb[0:Lk, :]) / den
            o_ref[:, h * dh:(h + 1) * dh] = o.astype(o_ref.dtype)

    per = (L // TQ) // n_buckets
    for bk in range(n_buckets):
        pl.when(qb // per == bk)(functools.partial(attend, (bk + 1) * per * TQ))


def attn_prompt(z, q_block, iq_block, tail_block, tail_w, iw_off, kn, v, ik, gq_tiled, B, T, W, dh, TQ=256,
                n_buckets=4):
    nq = T // TQ
    n_sel = min(TOPK_MAX, T // 4)
    eye = (jnp.arange(W)[:, None] // dh == jnp.arange(W)[None, :] // dh).astype(BF16)
    tri = (jnp.arange(LANES)[:, None] <= jnp.arange(LANES)[None, :]).astype(BF16)
    kspec = pl.BlockSpec((1, T, dh), lambda b, i: (b, 0, 0))
    return pl.pallas_call(
        functools.partial(_attn_prompt_kernel, TQ=TQ, L=T, dh=dh, iw_off=iw_off, n_sel=n_sel, n_buckets=n_buckets),
        grid=(B, nq),
        in_specs=[pl.BlockSpec((TQ, W), lambda b, i: (b * nq + i, q_block)),
                  pl.BlockSpec((TQ, W), lambda b, i: (b * nq + i, iq_block)),
                  pl.BlockSpec((TQ, tail_w), lambda b, i: (b * nq + i, tail_block)),
                  kspec, kspec, kspec,
                  pl.BlockSpec((1, W), lambda b, i: (0, 0)),
                  pl.BlockSpec((W, W), lambda b, i: (0, 0)),
                  pl.BlockSpec((LANES, LANES), lambda b, i: (0, 0))],
        out_specs=pl.BlockSpec((TQ, W), lambda b, i: (b * nq + i, 0)),
        out_shape=jax.ShapeDtypeStruct((B * T, W), BF16),
        scratch_shapes=[pltpu.VMEM((T, dh), BF16), pltpu.VMEM((T, dh), BF16), pltpu.VMEM((T, dh), BF16),
                        pltpu.VMEM((TQ, T), F32)],
        compiler_params=_cparams(("arbitrary", "arbitrary")),
        name="attn_prompt",
    )(z, z, z, kn, v, ik, gq_tiled, eye, tri)


def _xattn_kernel(q_ref, mk_ref, mv_ref, g_ref, o_ref, *, G, T, dh, scale):
    g = g_ref[...]
    for s in range(G):
        rows = slice(s * T, (s + 1) * T)
        for h in range(X_HEADS):
            cols = slice(h * dh, (h + 1) * dh)
            qh = q_ref[rows, cols]
            ms = jnp.mean(qh * qh, axis=-1, keepdims=True)
            qn = (qh * lax.rsqrt(ms + EPS) * g).astype(BF16)
            lg = _dot_nt(qn, mk_ref[s, :, cols].astype(BF16)) * scale
            m = jnp.max(lg, axis=-1, keepdims=True)
            e = jnp.exp(lg - m)
            den = jnp.sum(e, axis=-1, keepdims=True)
            o = _dot(e.astype(BF16), mv_ref[s, :, cols].astype(BF16)) / den
            o_ref[rows, cols] = o.astype(o_ref.dtype)


def _xattn_rows_kernel(q_ref, mk_ref, mv_ref, g_ref, o_ref, *, G, T, dh, scale):
    g = g_ref[...]
    H = X_HEADS
    n = mk_ref.shape[1]
    own = (lax.broadcasted_iota(I32, (H * T, n), 1) % H) == (lax.broadcasted_iota(I32, (H * T, n), 0) // T)
    for s in range(G):
        rows = slice(s * T, (s + 1) * T)
        parts = []
        for h in range(H):
            qh = q_ref[rows, h * dh:(h + 1) * dh]
            ms = jnp.mean(qh * qh, axis=-1, keepdims=True)
            parts.append(qh * lax.rsqrt(ms + EPS) * g)
        qs = jnp.concatenate(parts, axis=0).astype(BF16)
        lg = jnp.where(own, _dot_nt(qs, mk_ref[s].astype(BF16)) * scale, NEG_INF)
        m = jnp.max(lg, axis=-1, keepdims=True)
        e = jnp.exp(lg - m)
        den = jnp.sum(e, axis=-1, keepdims=True)
        o = _dot(e.astype(BF16), mv_ref[s].astype(BF16)) / den
        for h in range(H):
            o_ref[rows, h * dh:(h + 1) * dh] = o[h * T:(h + 1) * T, :].astype(o_ref.dtype)


def cross_attention(qx, row0, Bt, T, G, TQ, mem_k, mem_v, layer, gq):
    Wx = qx.shape[1]
    dh = Wx // X_HEADS
    n_rows, wm = mem_k.shape[2:]
    assert G == 1 or TQ == T
    R = G * TQ
    nt = T // TQ
    rb0 = row0 // R
    mspec = pl.BlockSpec((None, G, n_rows, wm), lambda i, t: (layer, i, 0, 0))
    body = _xattn_kernel if wm == Wx else _xattn_rows_kernel
    return pl.pallas_call(
        functools.partial(body, G=G, T=TQ, dh=dh, scale=dh ** -0.5),
        grid=(Bt // G, nt),
        in_specs=[pl.BlockSpec((R, Wx), lambda i, t: (rb0 + i * nt + t, 0)), mspec, mspec,
                  pl.BlockSpec((1, dh), lambda i, t: (0, 0))],
        out_specs=pl.BlockSpec((R, Wx), lambda i, t: (i * nt + t, 0)),
        out_shape=jax.ShapeDtypeStruct((Bt * T, Wx), BF16),
        compiler_params=_cparams(("arbitrary", "arbitrary")),
        name="cross_attention",
    )(qx, mem_k, mem_v, gq.reshape(1, dh))


def _attn_sample_kernel(pt_ref, q_ref, iq_ref, tail_ref, kn_ref, vn_ref, ikn_ref, ck_hbm, cv_hbm, cik_hbm,
                        gq_ref, e_ref, tri_ref, o_ref,
                        kp, vp, ikp, qs_scr, iqs_scr, iw_scr, score, bias, sems,
                        *, G, T, dh, layer, n_pages, hp, iw_off, n_sel):
    step = pl.program_id(0)
    R = G * T
    H = N_HEADS_C
    LP = n_pages * hp
    LC = LP + LANES

    def page_copies(g, p):
        page = pt_ref[step * G + g, p]
        return [pltpu.make_async_copy(src.at[layer, page], dst.at[g, :, p * hp:(p + 1) * hp], sems.at[s, g])
                for s, (src, dst) in enumerate(((ck_hbm, kp), (cv_hbm, vp), (cik_hbm, ikp)))]

    def start_seq(g, c):
        for p in range(n_pages):
            for cp in page_copies(g, p):
                cp.start()
        return c

    def wait_seq(g):
        for p in range(n_pages):
            for cp in page_copies(g, p):
                cp.wait()

    lax.fori_loop(0, G, start_seq, 0)

    def padded_new(ref, rows):
        return jnp.concatenate([ref[rows, :], jnp.zeros((LANES - T, dh), F32)], axis=0).astype(BF16)

    scale = dh ** -0.5
    q = q_ref[...]
    q2 = q * q
    hi = q2.astype(BF16)
    lo = (q2 - hi.astype(F32)).astype(BF16)
    ss = _dot(hi, e_ref[...]) + _dot(lo, e_ref[...])
    qs_scr[...] = q * lax.rsqrt(ss * (1.0 / dh) + EPS) * gq_ref[...] * scale
    iqs_scr[...] = iq_ref[...] * scale
    iw_blk = (iw_off // LANES) * LANES
    iw_lane = iw_off - iw_blk
    iw_scr[...] = tail_ref[:, iw_blk:iw_blk + LANES]

    new_ok = lax.broadcasted_iota(I32, (R, LANES), 1) <= lax.broadcasted_iota(I32, (R, LANES), 0) % T

    def stack_heads(ref, rows):
        return jnp.concatenate([ref[rows, h * dh:(h + 1) * dh] for h in range(H)], axis=0).astype(BF16)

    def score_body(g, c):
        wait_seq(g)
        rows = pl.ds(pl.multiple_of(g * T, T), T)
        qs = stack_heads(iqs_scr, rows)
        lg_p = _dot(qs, ikp[g].astype(BF16))
        lg_n = _dot_nt(qs, padded_new(ikn_ref, rows))
        iw = iw_scr[rows, :] * (IDX_HEADS ** -0.5)
        sp = sn = None
        for h in range(IDX_HEADS):
            w = iw[:, iw_lane + h:iw_lane + h + 1]
            hs = slice(h * T, (h + 1) * T)
            tp = jnp.maximum(lg_p[hs], 0.0) * w
            tn = jnp.maximum(lg_n[hs], 0.0) * w
            sp, sn = (tp, tn) if sp is None else (sp + tp, sn + tn)
        score[rows, 0:LP] = sp
        score[rows, LP:LC] = sn
        return c

    lax.fori_loop(0, G, score_body, 0, unroll=2)
    score[:, LP:LC] = jnp.where(new_ok, score[:, LP:LC], NEG_INF)

    key = _sort_keys(score[...])
    thr = _kth_largest_key(key, n_sel)
    gt = key > thr
    eq = key == thr
    need = float(n_sel) - jnp.sum(jnp.where(gt, 1.0, 0.0), axis=-1, keepdims=True)
    pref, _ = _blocked_prefix(jnp.where(eq, 1.0, 0.0), tri_ref[...], jnp.zeros((R, 1), F32))
    pref = jnp.concatenate(pref, axis=1)
    bias[...] = jnp.where(gt | (eq & (pref <= need)), 0.0, NEG_INF)
    bias[:, LP:LC] = jnp.where(new_ok, bias[:, LP:LC], NEG_INF)

    def attn_body(g, c):
        rows = pl.ds(pl.multiple_of(g * T, T), T)
        qs = stack_heads(qs_scr, rows)
        bh = jnp.concatenate([bias[rows, :]] * H, axis=0)
        lg_p = _dot(qs, kp[g].astype(BF16)) + bh[:, 0:LP]
        lg_n = _dot_nt(qs, padded_new(kn_ref, rows)) + bh[:, LP:LC]
        m = jnp.maximum(jnp.max(lg_p, axis=-1, keepdims=True), jnp.max(lg_n, axis=-1, keepdims=True))
        ep = jnp.exp(lg_p - m)
        en = jnp.exp(lg_n - m)
        den = jnp.sum(ep, axis=-1, keepdims=True) + jnp.sum(en, axis=-1, keepdims=True)
        o = (_dot_nt(ep.astype(BF16), vp[g].astype(BF16)) + _dot(en.astype(BF16), padded_new(vn_ref, rows))) / den
        for h in range(H):
            o_ref[rows, h * dh:(h + 1) * dh] = o[h * T:(h + 1) * T, :].astype(o_ref.dtype)
        return c

    lax.fori_loop(0, G, attn_body, 0, unroll=2)


def attn_sample(z, row0, q_block, iq_block, tail_block, tail_w, iw_off, kn, v, ik, caches, page_table, layer,
                gq_tiled, Bs, T, W, dh, G=16):
    n_pages = page_table.shape[1]
    page = caches[0].shape[2]
    caches_t = [c.transpose(0, 1, 3, 2) for c in caches]
    R = G * T
    rb0 = row0 // R
    LP = n_pages * page
    LC = LP + LANES
    n_sel = min(TOPK_MAX, (LP + T) // 4)
    eye = (jnp.arange(W)[:, None] // dh == jnp.arange(W)[None, :] // dh).astype(BF16)
    tri = (jnp.arange(LANES)[:, None] <= jnp.arange(LANES)[None, :]).astype(BF16)
    any_spec = pl.BlockSpec(memory_space=pl.ANY)
    nspec = pl.BlockSpec((R, dh), lambda i, pt: (rb0 + i, 0))
    grid_spec = pltpu.PrefetchScalarGridSpec(
        num_scalar_prefetch=1,
        grid=(Bs // G,),
        in_specs=[pl.BlockSpec((R, W), lambda i, pt: (rb0 + i, q_block)),
                  pl.BlockSpec((R, W), lambda i, pt: (rb0 + i, iq_block)),
                  pl.BlockSpec((R, tail_w), lambda i, pt: (rb0 + i, tail_block)),
                  nspec, nspec, nspec, any_spec, any_spec, any_spec,
                  pl.BlockSpec((1, W), lambda i, pt: (0, 0)),
                  pl.BlockSpec((W, W), lambda i, pt: (0, 0)),
                  pl.BlockSpec((LANES, LANES), lambda i, pt: (0, 0))],
        out_specs=pl.BlockSpec((R, W), lambda i, pt: (i, 0)),
        scratch_shapes=[pltpu.VMEM((G, dh, LP), F32), pltpu.VMEM((G, dh, LP), F32), pltpu.VMEM((G, dh, LP), F32),
                        pltpu.VMEM((R, W), F32), pltpu.VMEM((R, W), F32), pltpu.VMEM((R, LANES), F32),
                        pltpu.VMEM((R, LC), F32), pltpu.VMEM((R, LC), F32),
                        pltpu.SemaphoreType.DMA((3, G))],
    )
    return pl.pallas_call(
        functools.partial(_attn_sample_kernel, G=G, T=T, dh=dh, layer=layer, n_pages=n_pages, hp=page,
                          iw_off=iw_off, n_sel=n_sel),
        grid_spec=grid_spec,
        out_shape=jax.ShapeDtypeStruct((Bs * T, W), BF16),
        compiler_params=_cparams(("arbitrary",)),
        name="attn_sample",
    )(page_table, z, z, z, kn, v, ik, *caches_t, gq_tiled, eye, tri)


def _block_diag(blocks):
    G, a, b = blocks.shape
    eye = jnp.eye(G, dtype=blocks.dtype)
    return (eye[:, None, :, None] * blocks[:, :, None, :]).reshape(G * a, G * b)


def _lane_blocks(a):
    rows, n = a.shape
    return a.reshape(rows, n // LANES, LANES).transpose(1, 0, 2)


def kernel(x_prompt, x_sample, cache_attn_k, cache_attn_v, cache_idx_k, cache_mem_k, cache_mem_v, state_pool,
           state_conv, state_ssm_re, state_ssm_im, page_table, mem_prompt, norm_g, ffn_in, ffn_out, w_in,
           q_norm_g, k_norm_g, pool_mix, pool_scale, conv_w, ssm_a_re, ssm_a_im, ssm_log_step, ssm_b_re,
           ssm_b_im, ssm_c_re, ssm_c_im, ssm_d, ssm_glu_w, ssm_glu_b, w_branch, w_gate, b_gate, w_o,
           mem_norm_g, w_xq, w_xk, w_xv, xq_norm_g, xk_norm_g, w_xo):
    B, T, D = x_prompt.shape
    Bs, Ts, _ = x_sample.shape
    depth = norm_g.shape[0]
    Mp, Ms = B * T, Bs * Ts
    W = pool_scale.shape[1]
    dh = k_norm_g.shape[1]
    d_ff = ffn_out.shape[2]
    n_mem = mem_prompt.shape[1]
    Wx = w_xq.shape[2]
    SG, SN = ssm_a_re.shape[1:]
    NB = SG * SN // LANES
    past_len = page_table.shape[1] * cache_attn_k.shape[2]
    assert cache_idx_k.shape[-1] == dh and W == N_HEADS_C * dh == IDX_HEADS * dh
    TM = 1024

    o_k = 5 * W
    o_iq = o_k + 2 * dh
    o_ik = o_iq + W
    o_xs = o_ik + dh + IDX_HEADS
    tail_w = 2 * LANES
    Q_BLK, IQ_BLK, XS_BLK = 4, 5, 6
    TAIL_BLK = 7 * W // tail_w
    IW_OFF = 3 * dh
    n_z = 7 * W + tail_w

    h = (x_prompt.reshape(Mp, D), x_sample.reshape(Ms, D))
    caches = (cache_attn_k, cache_attn_v, cache_idx_k)
    mem_rows = mem_prompt.reshape(B * n_mem, D)
    outs = [[] for _ in range(16)]

    def ffn(h, l, i, g, split_out_rows=None):
        act = matmul(h, [(ffn_in, (l, i), 0), (ffn_in, (l, i), d_ff // 512)], n_out=d_ff, tn=512, tm=TM,
                     mode="swiglu", out_dtype=BF16, norm_g=g, name="ffn_in")
        return matmul(act, [(ffn_out, (l, i), 0)], n_out=D, tn=512, tm=TM, mode="residual", aux=h, scale=0.5,
                      split_out_rows=split_out_rows, name="ffn_out")

    for l in range(depth):
        h = ffn(h, l, 0, norm_g[l, 0])

        wl = w_in[l]
        w_in2 = jnp.concatenate([wl[:, 0:o_k], wl[:, o_iq:o_iq + W], wl[:, o_xs:o_xs + W], wl[:, o_k:o_k + 2 * dh],
                                 wl[:, o_ik:o_ik + dh + IDX_HEADS],
                                 jnp.zeros((D, tail_w - 3 * dh - IDX_HEADS), F32)], axis=1)
        z = matmul(h, [(w_in2, (), 0)], n_out=n_z, tn=1280, tm=TM, norm_g=norm_g[l, 1], name="in_proj")

        wmix = pool_mix[l].astype(BF16)
        ya_p, yb_p, npool_p, nconv_p = pool_conv(z, 0, B, T, 1, 512, 0, None, None, wmix, pool_scale[l], conv_w[l], W)
        ya_s, yb_s, npool_s, nconv_s = pool_conv(z, Mp, Bs, Ts, 16, Ts, past_len, state_pool[l], state_conv[l],
                                                 wmix, pool_scale[l], conv_w[l], W)

        kn, vv, ik = kv_prep(z, TAIL_BLK, tail_w, k_norm_g[l], dh)
        gq_tiled = jnp.tile(q_norm_g[l], N_HEADS_C).reshape(1, W)
        yc_p = attn_prompt(z, Q_BLK, IQ_BLK, TAIL_BLK, tail_w, IW_OFF, kn[:Mp].reshape(B, T, dh),
                           vv[:Mp].reshape(B, T, dh), ik[:Mp].reshape(B, T, dh), gq_tiled, B, T, W, dh)
        yc_s = attn_sample(z, Mp, Q_BLK, IQ_BLK, TAIL_BLK, tail_w, IW_OFF, kn, vv, ik, caches, page_table, l,
                           gq_tiled, Bs, Ts, W, dh)

        ab_re, ab_im, co_re, co_im = ssm_params(ssm_a_re[l], ssm_a_im[l], ssm_log_step[l])
        flat = lambda a: a.reshape(1, SG * SN)
        lev_re, lev_im, pw_re, pw_im = ssm_tables(flat(ab_re), flat(ab_im), SUBLANES)
        bb_re, bb_im = ssm_fold_input(_block_diag(ssm_b_re[l].transpose(0, 2, 1)),
                                      _block_diag(ssm_b_im[l].transpose(0, 2, 1)), flat(co_re), flat(co_im))
        consts = (
            _lane_blocks(bb_re), _lane_blocks(bb_im),
            _block_diag(ssm_c_re[l].transpose(0, 2, 1)).reshape(NB, LANES, W).astype(BF16),
            _block_diag(ssm_c_im[l].transpose(0, 2, 1)).reshape(NB, LANES, W).astype(BF16),
            _lane_blocks(lev_re), _lane_blocks(lev_im),
            _lane_blocks(pw_re), _lane_blocks(pw_im),
            ssm_d[l].reshape(1, W), ssm_glu_w[l].astype(BF16), ssm_glu_b[l].reshape(1, W))
        yd_p, sre_p, sim_p = ssm_mixer(z, XS_BLK, 0, B, T, consts, None)
        h0 = (state_ssm_re[l].reshape(Bs, NB, LANES), state_ssm_im[l].reshape(Bs, NB, LANES))
        yd_s, sre_s, sim_s = ssm_mixer(z, XS_BLK, Mp, Bs, Ts, consts, h0)

        merged = gated_merge(h, norm_g[l, 1], [(ya_p, ya_s), (yb_p, yb_s), (yc_p, yc_s), (yd_p, yd_s)],
                             w_gate, b_gate, w_branch, l)
        h = matmul(merged, [(w_o, (l,), 0)], n_out=D, tn=1024, tm=TM, mode="residual", aux=h, scale=1.0, name="w_o")

        mn = rmsnorm_rows(mem_rows, mem_norm_g[l])
        mk = matmul(mn, [(w_xk, (l,), 0)], n_out=Wx, tn=Wx, tm=B * n_mem, mode="headnorm",
                    aux=jnp.tile(xk_norm_g[l], X_HEADS).reshape(1, Wx), group=Wx // X_HEADS, name="mem_k")
        mv = matmul(mn, [(w_xv, (l,), 0)], n_out=Wx, tn=Wx, tm=B * n_mem, name="mem_v")
        qx = matmul(h, [(w_xq, (l,), 0)], n_out=Wx, tn=Wx, tm=TM, norm_g=norm_g[l, 2], name="w_xq")
        xa_p = cross_attention(qx, 0, B, T, 1, 512, mk.reshape(1, B, n_mem, Wx), mv.reshape(1, B, n_mem, Wx), 0,
                               xq_norm_g[l])
        xa_s = cross_attention(qx, Mp, Bs, Ts, 8, Ts, cache_mem_k.reshape(depth, Bs, n_mem * X_HEADS, Wx // X_HEADS),
                               cache_mem_v.reshape(depth, Bs, n_mem * X_HEADS, Wx // X_HEADS), l, xq_norm_g[l])
        h = matmul((xa_p, xa_s), [(w_xo, (l,), 0)], n_out=D, tn=1024, tm=TM, mode="residual", aux=h, scale=1.0,
                   name="w_xo")

        h = ffn(h, l, 1, norm_g[l, 3], split_out_rows=Mp if l == depth - 1 else None)

        xh = Wx // X_HEADS
        layer_out = (kn[:Mp].reshape(B, T, dh), vv[:Mp].reshape(B, T, dh), ik[:Mp].reshape(B, T, dh),
                     mk.reshape(B, n_mem, X_HEADS, xh), mv.reshape(B, n_mem, X_HEADS, xh), npool_p, nconv_p,
                     sre_p.reshape(B, SG, SN), sim_p.reshape(B, SG, SN),
                     kn[Mp:].reshape(Bs, Ts, dh), vv[Mp:].reshape(Bs, Ts, dh), ik[Mp:].reshape(Bs, Ts, dh),
                     npool_s, nconv_s, sre_s.reshape(Bs, SG, SN), sim_s.reshape(Bs, SG, SN))
        for acc, val in zip(outs, layer_out):
            acc.append(val)

    return (h[0].reshape(B, T, D), h[1].reshape(Bs, Ts, D)) + tuple(jnp.stack(o) for o in outs)
```

```python
import functools
import math

import jax
import jax.numpy as jnp
from jax import lax
from jax.experimental import pallas as pl
from jax.experimental.pallas import tpu as pltpu

F32, BF16, I32 = jnp.float32, jnp.bfloat16, jnp.int32
EPS = 1e-6
NEG_INF = float("-inf")

V7X_VMEM_BYTES = 64 * 1024 * 1024
VMEM_LIMIT = V7X_VMEM_BYTES - 8 * 1024 * 1024
LANES = 128
SUBLANES = 8

POOL_WINDOWS = (2, 4, 8, 16)
POOL_HIST = 16
CONV_WIDTH = 3
CONV_HIST = 8
N_HEADS_C = 8
IDX_HEADS = 8
TOPK_MAX = 256
X_HEADS = 4
SSM_CHUNK = 256
SSM_SEQ_ROWS = 128


def _cparams(sem, vmem=VMEM_LIMIT):
    return pltpu.CompilerParams(dimension_semantics=sem, vmem_limit_bytes=vmem)


def _dot(a, b):
    return jnp.dot(a, b, preferred_element_type=F32)


def _dot_nt(a, b):
    return lax.dot_general(a, b, (((1,), (1,)), ((), ())), preferred_element_type=F32)


def _rmsnorm_kernel(x_ref, g_ref, o_ref):
    x = x_ref[...]
    ms = jnp.mean(x * x, axis=-1, keepdims=True)
    o_ref[...] = (x * lax.rsqrt(ms + EPS) * g_ref[...]).astype(o_ref.dtype)


def rmsnorm_rows(x, g, tm=512):
    M, D = x.shape
    return pl.pallas_call(
        _rmsnorm_kernel,
        grid=(M // tm,),
        in_specs=[pl.BlockSpec((tm, D), lambda i: (i, 0)), pl.BlockSpec((1, D), lambda i: (0, 0))],
        out_specs=pl.BlockSpec((tm, D), lambda i: (i, 0)),
        out_shape=jax.ShapeDtypeStruct((M, D), BF16),
        compiler_params=_cparams(("arbitrary",)),
        name="rmsnorm",
    )(x, g.reshape(1, D))


def _rms_rows_bf16(x, g):
    ms = jnp.mean(x * x, axis=-1, keepdims=True)
    return (x * lax.rsqrt(ms + EPS) * g).astype(BF16)


def _pick_rows(ref_a, ref_b, rows, n_a_tiles):
    if ref_b is None:
        return ref_a[rows, :]
    return jnp.where(pl.program_id(1) >= n_a_tiles, ref_b[rows, :], ref_a[rows, :])


def _split_row_specs(xa, xb, tm, tn=None):
    na = xa.shape[0] // tm
    width = xa.shape[1] if tn is None else tn
    col = (lambda j: 0) if tn is None else (lambda j: j)
    spec_a = pl.BlockSpec((tm, width), lambda j, i: (jnp.minimum(i, na - 1), col(j)))
    spec_b = pl.BlockSpec((tm, width), lambda j, i: (jnp.maximum(i - na, 0), col(j)),
                          pipeline_mode=pl.Buffered(1) if (xb.shape[0] == tm and tn is None) else None)
    return spec_a, spec_b, na


def _mm_kernel(*refs, nw, mode, scale, sub, group, norm, n_a_tiles, aux_a_tiles, out_a_tiles):
    x_ref = refs[0]
    xb_ref = None
    if n_a_tiles is not None:
        xb_ref = refs[1]
        refs = refs[1:]
    w_refs = refs[1:1 + nw]
    p = 1 + nw
    if norm:
        g_ref = refs[p]
        p += 1
    aux_ref = auxb_ref = None
    if mode in ("bias_sigmoid", "residual", "headnorm"):
        aux_ref = refs[p]
        p += 1
        if aux_a_tiles is not None:
            auxb_ref = refs[p]
            p += 1
    o_ref = refs[p]
    ob_ref = None
    if out_a_tiles is not None:
        ob_ref = refs[p + 1]
        p += 1
    scr = refs[p + 1:p + 1 + nw]

    @pl.when(pl.program_id(1) == 0)
    def _():
        for w_ref, s in zip(w_refs, scr):
            s[...] = w_ref[...].astype(BF16)

    tm = x_ref.shape[0]

    def body(xr, r, carry):
        rows = pl.ds(pl.multiple_of(r * sub, sub), sub)
        x = xr[rows, :]
        if norm:
            x = _rms_rows_bf16(x, g_ref[...])
        acc = [_dot(x, s[...]) for s in scr]
        if mode == "swiglu":
            a, b = acc
            y = (a * jax.nn.sigmoid(a)) * b
        elif mode == "bias_sigmoid":
            y = jax.nn.sigmoid(acc[0] + aux_ref[...])
        elif mode == "residual":
            y = _pick_rows(aux_ref, auxb_ref, rows, aux_a_tiles) + scale * acc[0]
        elif mode == "headnorm":
            a = acc[0]
            parts = []
            for h in range(a.shape[1] // group):
                ah = a[:, h * group:(h + 1) * group]
                ms = jnp.mean(ah * ah, axis=-1, keepdims=True)
                parts.append(ah * lax.rsqrt(ms + EPS))
            y = jnp.concatenate(parts, axis=1) * aux_ref[...]
        else:
            y = acc[0]
        if ob_ref is None:
            o_ref[rows, :] = y.astype(o_ref.dtype)
        else:
            @pl.when(pl.program_id(1) < out_a_tiles)
            def _():
                o_ref[rows, :] = y.astype(o_ref.dtype)

            @pl.when(pl.program_id(1) >= out_a_tiles)
            def _():
                ob_ref[rows, :] = y.astype(ob_ref.dtype)
        return carry

    def run(xr):
        lax.fori_loop(0, tm // sub, functools.partial(body, xr), 0, unroll=True)

    if xb_ref is None:
        run(x_ref)
    else:
        pl.when(pl.program_id(1) < n_a_tiles)(functools.partial(run, x_ref))
        pl.when(pl.program_id(1) >= n_a_tiles)(functools.partial(run, xb_ref))


def matmul(x, weights, *, n_out, tn, tm, mode="plain", aux=None, scale=1.0, out_dtype=F32, group=LANES,
           norm_g=None, split_out_rows=None, name="matmul"):
    nw = len(weights)
    sub = min(tm, 256)
    n_a_tiles = None
    if isinstance(x, tuple):
        xa, xb = x
        spec_a, spec_b, n_a_tiles = _split_row_specs(xa, xb, tm)
        M, K = xa.shape[0] + xb.shape[0], xa.shape[1]
        in_specs = [spec_a, spec_b]
        args = [xa, xb]
    else:
        M, K = x.shape
        in_specs = [pl.BlockSpec((tm, K), lambda j, i: (i, 0))]
        args = [x]
    for arr, lead, coff in weights:
        nl = len(lead)
        in_specs.append(pl.BlockSpec((None,) * nl + (K, tn),
                                     functools.partial(lambda j, i, lead, coff: (*lead, 0, coff + j), lead=lead, coff=coff)))
        args.append(arr)
    if norm_g is not None:
        in_specs.append(pl.BlockSpec((1, K), lambda j, i: (0, 0)))
        args.append(norm_g.reshape(1, K))
    if mode in ("bias_sigmoid", "headnorm"):
        in_specs.append(pl.BlockSpec((1, tn), lambda j, i: (0, j)))
        args.append(aux)
    aux_a_tiles = None
    if mode == "residual" and isinstance(aux, tuple):
        spec_a, spec_b, aux_a_tiles = _split_row_specs(aux[0], aux[1], tm, tn)
        in_specs += [spec_a, spec_b]
        args += list(aux)
    elif mode == "residual":
        in_specs.append(pl.BlockSpec((tm, tn), lambda j, i: (i, j)))
        args.append(aux)
    out_a_tiles = None
    out_specs = pl.BlockSpec((tm, tn), lambda j, i: (i, j))
    out_shape = jax.ShapeDtypeStruct((M, n_out), out_dtype)
    if split_out_rows is not None:
        out_a_tiles = split_out_rows // tm
        out_specs = [pl.BlockSpec((tm, tn), lambda j, i: (jnp.minimum(i, out_a_tiles - 1), j)),
                     pl.BlockSpec((tm, tn), lambda j, i: (jnp.maximum(i - out_a_tiles, 0), j))]
        out_shape = [jax.ShapeDtypeStruct((split_out_rows, n_out), out_dtype),
                     jax.ShapeDtypeStruct((M - split_out_rows, n_out), out_dtype)]
    return pl.pallas_call(
        functools.partial(_mm_kernel, nw=nw, mode=mode, scale=scale, sub=sub, group=group, norm=norm_g is not None,
                          n_a_tiles=n_a_tiles, aux_a_tiles=aux_a_tiles, out_a_tiles=out_a_tiles),
        grid=(n_out // tn, M // tm),
        in_specs=in_specs,
        out_specs=out_specs,
        out_shape=out_shape,
        scratch_shapes=[pltpu.VMEM((K, tn), BF16) for _ in range(nw)],
        compiler_params=_cparams(("arbitrary", "arbitrary")),
        name=name,
    )(*args)


def _merge_kernel(*refs, nb, n_a_tiles):
    h_ref, g_ref = refs[0:2]
    br = refs[2:2 + nb]
    brb = refs[2 + nb:2 + 2 * nb]
    refs = refs[nb:]
    wg = refs[2 + nb:2 + 2 * nb]
    bg = refs[2 + 2 * nb:2 + 3 * nb]
    wb = refs[2 + 3 * nb:2 + 4 * nb]
    o_ref = refs[2 + 4 * nb]
    sg = refs[3 + 4 * nb:3 + 5 * nb]
    sb = refs[3 + 5 * nb:3 + 6 * nb]

    @pl.when(pl.program_id(1) == 0)
    def _():
        for w_ref, s in zip(wg + wb, sg + sb):
            s[...] = w_ref[...].astype(BF16)

    tm = o_ref.shape[0]
    sub = min(tm, 256)

    def body(r, carry):
        rows = pl.ds(pl.multiple_of(r * sub, sub), sub)
        u = _rms_rows_bf16(h_ref[rows, :], g_ref[...])
        acc = None
        for k in range(nb):
            gate = jax.nn.sigmoid(_dot(u, sg[k][...]) + bg[k][...])
            t = gate * _dot(_pick_rows(br[k], brb[k], rows, n_a_tiles), sb[k][...])
            acc = t if acc is None else acc + t
        o_ref[rows, :] = acc.astype(o_ref.dtype)
        return carry

    lax.fori_loop(0, tm // sub, body, 0, unroll=True)


def gated_merge(h, norm_g, branches, w_gate, b_gate, w_branch, layer, *, tn=256, tm=1024):
    nb = len(branches)
    M = h.shape[0]
    W = branches[0][0].shape[1]
    D = w_branch.shape[-1]
    nj = D // tn
    bias = b_gate.reshape(b_gate.shape[0], 1, nb * D)

    def per_branch(shape):
        return [pl.BlockSpec(shape, functools.partial(lambda j, i, k: (layer, 0, k * nj + j), k=k)) for k in range(nb)]

    row_specs = [_split_row_specs(a, b, tm) for a, b in branches]
    n_a_tiles = row_specs[0][2]
    in_specs = [pl.BlockSpec((tm, D), lambda j, i: (i, 0)), pl.BlockSpec((1, D), lambda j, i: (0, 0))]
    in_specs += [s[0] for s in row_specs] + [s[1] for s in row_specs]
    in_specs += per_branch((None, D, tn)) + per_branch((None, 1, tn))
    in_specs += [pl.BlockSpec((None, None, W, tn), functools.partial(lambda j, i, k: (layer, k, 0, j), k=k))
                 for k in range(nb)]
    return pl.pallas_call(
        functools.partial(_merge_kernel, nb=nb, n_a_tiles=n_a_tiles),
        grid=(nj, M // tm),
        in_specs=in_specs,
        out_specs=pl.BlockSpec((tm, tn), lambda j, i: (i, j)),
        out_shape=jax.ShapeDtypeStruct((M, D), BF16),
        scratch_shapes=[pltpu.VMEM((D, tn), BF16) for _ in range(nb)] + [pltpu.VMEM((W, tn), BF16) for _ in range(nb)],
        compiler_params=_cparams(("arbitrary", "arbitrary")),
        name="gated_merge",
    )(h, norm_g.reshape(1, D), *[a for a, _ in branches], *[b for _, b in branches],
      *([w_gate] * nb), *([bias] * nb), *([w_branch] * nb))


def _kvprep_kernel(t_ref, g_ref, k_ref, v_ref, ik_ref, *, dh):
    t = t_ref[...]
    k = t[:, 0:dh]
    ms = jnp.mean(k * k, axis=-1, keepdims=True)
    k_ref[...] = k * lax.rsqrt(ms + EPS) * g_ref[...]
    v_ref[...] = t[:, dh:2 * dh]
    ik_ref[...] = t[:, 2 * dh:3 * dh]


def kv_prep(z, tail_block, tail_w, gk, dh, tm=1024):
    M = z.shape[0]
    out = jax.ShapeDtypeStruct((M, dh), F32)
    return pl.pallas_call(
        functools.partial(_kvprep_kernel, dh=dh),
        grid=(M // tm,),
        in_specs=[pl.BlockSpec((tm, tail_w), lambda i: (i, tail_block)), pl.BlockSpec((1, dh), lambda i: (0, 0))],
        out_specs=[pl.BlockSpec((tm, dh), lambda i: (i, 0))] * 3,
        out_shape=[out, out, out],
        compiler_params=_cparams(("arbitrary",)),
        name="kv_prep",
    )(z, gk.reshape(1, dh))


def _poolconv_kernel(*refs, G, T, W, pos0, has_state):
    xp_ref, xc_ref, bg_ref, cg_ref = refs[0:4]
    p = 4
    if has_state:
        pbuf_ref, cbuf_ref = refs[4:6]
        p = 6
    wmix_ref, pscale_ref, convw_ref = refs[p:p + 3]
    ya_ref, yb_ref, npool_ref, nconv_ref = refs[p + 3:p + 7]
    fullp, fullc = refs[p + 7:p + 9]
    PH, CH = POOL_HIST, CONV_HIST
    nh = CONV_WIDTH - 1
    gw = W // len(POOL_WINDOWS)
    c = pl.program_id(1)

    @pl.when(c == 0)
    def _():
        if has_state:
            fullp[:, 1:PH, :] = pbuf_ref[...]
            fullc[:, CH - nh:CH, :] = cbuf_ref[...]
        else:
            fullp[:, 0:PH, :] = jnp.zeros((G, PH, W), F32)
            fullc[:, 0:CH, :] = jnp.zeros((G, CH, W), F32)

    @pl.when(c > 0)
    def _():
        fullp[:, 0:PH, :] = fullp[:, T:T + PH, :]
        fullc[:, 0:CH, :] = fullc[:, T:T + CH, :]

    fullp[:, PH:PH + T, :] = xp_ref[...].reshape(G, T, W)
    t_idx = lax.broadcasted_iota(I32, (1, T, 1), 1) + (c * T + (pos0 + 1))
    for gi, w in enumerate(POOL_WINDOWS):
        cols = slice(gi * gw, (gi + 1) * gw)
        acc = fullp[:, PH:PH + T, cols]
        for j in range(1, w):
            acc = acc + fullp[:, PH - j:PH - j + T, cols]
        cnt = jnp.minimum(t_idx, w).astype(F32)
        d = acc / cnt - fullp[:, PH:PH + T, cols]
        y = _dot(d.reshape(G * T, gw).astype(BF16), wmix_ref[gi]) * pscale_ref[:, cols]
        ya_ref[:, cols] = y.astype(ya_ref.dtype)
    npool_ref[...] = fullp[:, T + 1:T + PH, :]

    fullc[:, CH:CH + T, :] = (cg_ref[...] * xc_ref[...]).reshape(G, T, W)
    y = None
    for j in range(CONV_WIDTH):
        wj = convw_ref[j:j + 1, :].reshape(1, 1, W)
        term = wj * fullc[:, CH - nh + j:CH - nh + j + T, :]
        y = term if y is None else y + term
    yb_ref[...] = (bg_ref[...] * y.reshape(G * T, W)).astype(yb_ref.dtype)
    nconv_ref[...] = fullc[:, CH + T - nh:CH + T, :]


def pool_conv(z, row0, Bt, T, G, TC, pos0, pool_buf, conv_buf, wmix_bf16, pool_scale, conv_w, W):
    has_state = pool_buf is not None
    assert G == 1 or TC == T
    R = G * TC
    nc = T // TC
    rb0 = row0 // R
    nh = CONV_WIDTH - 1

    def zspec(cb):
        return pl.BlockSpec((R, W), functools.partial(lambda i, c, cb: (rb0 + i * nc + c, cb), cb=cb))

    in_specs = [zspec(0), zspec(1), zspec(2), zspec(3)]
    args = [z, z, z, z]
    if has_state:
        in_specs += [pl.BlockSpec((G, POOL_HIST - 1, W), lambda i, c: (i, 0, 0)),
                     pl.BlockSpec((G, nh, W), lambda i, c: (i, 0, 0))]
        args += [pool_buf, conv_buf]
    nwin = len(POOL_WINDOWS)
    in_specs += [pl.BlockSpec((nwin, W // nwin, W // nwin), lambda i, c: (0, 0, 0)),
                 pl.BlockSpec((1, W), lambda i, c: (0, 0)),
                 pl.BlockSpec((CONV_WIDTH, W), lambda i, c: (0, 0))]
    args += [wmix_bf16, pool_scale.reshape(1, W), conv_w]
    return pl.pallas_call(
        functools.partial(_poolconv_kernel, G=G, T=TC, W=W, pos0=pos0, has_state=has_state),
        grid=(Bt // G, nc),
        in_specs=in_specs,
        out_specs=[pl.BlockSpec((R, W), lambda i, c: (i * nc + c, 0)), pl.BlockSpec((R, W), lambda i, c: (i * nc + c, 0)),
                   pl.BlockSpec((G, POOL_HIST - 1, W), lambda i, c: (i, 0, 0)),
                   pl.BlockSpec((G, nh, W), lambda i, c: (i, 0, 0))],
        out_shape=[jax.ShapeDtypeStruct((Bt * T, W), BF16), jax.ShapeDtypeStruct((Bt * T, W), BF16),
                   jax.ShapeDtypeStruct((Bt, POOL_HIST - 1, W), F32), jax.ShapeDtypeStruct((Bt, nh, W), F32)],
        scratch_shapes=[pltpu.VMEM((G, POOL_HIST + TC, W), F32), pltpu.VMEM((G, CONV_HIST + TC, W), F32)],
        compiler_params=_cparams(("arbitrary", "arbitrary")),
        name="pool_conv",
    )(*args)


def _ssm_params_kernel(are_ref, aim_ref, ls_ref, abre_ref, abim_ref, core_ref, coim_ref):
    a_re = are_ref[...]
    a_im = aim_ref[...]
    step = jnp.exp(ls_ref[...])
    decay = jnp.exp(step * a_re)
    ab_re = decay * jnp.cos(step * a_im)
    ab_im = decay * jnp.sin(step * a_im)
    den = a_re * a_re + a_im * a_im
    nr = ab_re - 1.0
    abre_ref[...] = ab_re
    abim_ref[...] = ab_im
    core_ref[...] = (nr * a_re + ab_im * a_im) / den
    coim_ref[...] = (ab_im * a_re - nr * a_im) / den


def ssm_params(a_re, a_im, log_step):
    G, N = a_re.shape
    out = jax.ShapeDtypeStruct((G, N), F32)
    return pl.pallas_call(_ssm_params_kernel, out_shape=[out, out, out, out], name="ssm_params")(
        a_re, a_im, log_step.reshape(G, 1))


def _scan_levels(C):
    return [1 << k for k in range(int(math.log2(C)))]


def _ssm_tables_kernel(abre_ref, abim_ref, lre_ref, lim_ref, pre_ref, pim_ref, *, C):
    ar = abre_ref[...]
    ai = abim_ref[...]
    N = ar.shape[1]
    row = lax.broadcasted_iota(I32, (C, N), 0)
    hr = jnp.where(row == 0, ar, 0.0)
    hi = jnp.where(row == 0, ai, 0.0)
    lre_ref[...] = jnp.zeros(lre_ref.shape, F32)
    lim_ref[...] = jnp.zeros(lim_ref.shape, F32)
    for k, s in enumerate(_scan_levels(C)):
        lre_ref[k:k + 1, :] = ar
        lim_ref[k:k + 1, :] = ai
        sr = jnp.where(row >= s, pltpu.roll(hr, s, 0), 0.0)
        si = jnp.where(row >= s, pltpu.roll(hi, s, 0), 0.0)
        hr, hi = hr + ar * sr - ai * si, hi + ar * si + ai * sr
        ar, ai = ar * ar - ai * ai, 2.0 * ar * ai
    pre_ref[...] = hr
    pim_ref[...] = hi


def ssm_tables(ab_re, ab_im, C):
    N = ab_re.shape[1]
    nlev = len(_scan_levels(C))
    lev = jax.ShapeDtypeStruct((SUBLANES * ((nlev + SUBLANES - 1) // SUBLANES), N), F32)
    pw = jax.ShapeDtypeStruct((C, N), F32)
    return pl.pallas_call(functools.partial(_ssm_tables_kernel, C=C), out_shape=[lev, lev, pw, pw],
                          name="ssm_tables")(ab_re, ab_im)


def _ssm_fold_kernel(bre_ref, bim_ref, core_ref, coim_ref, ore_ref, oim_ref):
    bre = bre_ref[...]
    bim = bim_ref[...]
    cor = core_ref[...]
    coi = coim_ref[...]
    ore_ref[...] = (cor * bre - coi * bim).astype(ore_ref.dtype)
    oim_ref[...] = (cor * bim + coi * bre).astype(oim_ref.dtype)


def ssm_fold_input(b_re_bd, b_im_bd, co_re, co_im):
    out = jax.ShapeDtypeStruct(b_re_bd.shape, BF16)
    return pl.pallas_call(_ssm_fold_kernel, out_shape=[out, out], name="ssm_fold")(b_re_bd, b_im_bd, co_re, co_im)


def _gelu_tanh(x):
    return 0.5 * x * (1.0 + jnp.tanh(math.sqrt(2.0 / math.pi) * (x + 0.044715 * (x * x * x))))


def _ssm_kernel(*refs, R, T, NB, chained):
    (xs_ref, bre_ref, bim_ref, cre_ref, cim_ref, lre_ref, lim_ref) = refs[0:7]
    p = 7
    if chained:
        pre_ref, pim_ref = refs[p:p + 2]
        p += 2
    else:
        h0r_ref, h0i_ref = refs[p:p + 2]
        p += 2
    d_ref, gw_ref, gb_ref = refs[p:p + 3]
    yd_ref, sre_ref, sim_ref = refs[p + 3:p + 6]
    p += 6
    if chained:
        car_ref, cai_ref = refs[p:p + 2]
    G = R // T
    xs = xs_ref[...]
    xb = xs.astype(BF16)
    tpos = lax.broadcasted_iota(I32, (R, LANES), 0) % SUBLANES
    sub_pos = lax.broadcasted_iota(I32, (SUBLANES, LANES), 0)
    levels = _scan_levels(SUBLANES)

    if chained:
        c = pl.program_id(1)

        @pl.when(c == 0)
        def _():
            car_ref[...] = jnp.zeros(car_ref.shape, F32)
            cai_ref[...] = jnp.zeros(cai_ref.shape, F32)

    y = d_ref[...] * xs
    for cb in range(NB):
        hr = _dot(xb, bre_ref[cb])
        hi = _dot(xb, bim_ref[cb])
        if not chained:
            ar = lre_ref[cb][0:1, :]
            ai = lim_ref[cb][0:1, :]
            h0r = jnp.broadcast_to(h0r_ref[:, cb:cb + 1, :], (G, T, LANES)).reshape(R, LANES)
            h0i = jnp.broadcast_to(h0i_ref[:, cb:cb + 1, :], (G, T, LANES)).reshape(R, LANES)
            first = tpos == 0
            hr = hr + jnp.where(first, ar * h0r - ai * h0i, 0.0)
            hi = hi + jnp.where(first, ar * h0i + ai * h0r, 0.0)
        for k, s in enumerate(levels):
            ar = jnp.where(sub_pos >= s, lre_ref[cb][k:k + 1, :], 0.0)
            ai = jnp.where(sub_pos >= s, lim_ref[cb][k:k + 1, :], 0.0)
            ar = jnp.concatenate([ar] * (R // SUBLANES), axis=0)
            ai = jnp.concatenate([ai] * (R // SUBLANES), axis=0)
            sr = pltpu.roll(hr, s, 0)
            si = pltpu.roll(hi, s, 0)
            hr, hi = hr + ar * sr - ai * si, hi + ar * si + ai * sr
        if chained:
            cr = car_ref[cb:cb + 1, :]
            ci = cai_ref[cb:cb + 1, :]
            pwr = pre_ref[cb]
            pwi = pim_ref[cb]
            grs, gis = [], []
            for v in range(R // SUBLANES):
                rows = slice(v * SUBLANES, (v + 1) * SUBLANES)
                cbr = jnp.broadcast_to(cr, (SUBLANES, LANES))
                cbi = jnp.broadcast_to(ci, (SUBLANES, LANES))
                gr = hr[rows] + pwr * cbr - pwi * cbi
                gi = hi[rows] + pwr * cbi + pwi * cbr
                cr = gr[SUBLANES - 1:SUBLANES, :]
                ci = gi[SUBLANES - 1:SUBLANES, :]
                grs.append(gr)
                gis.append(gi)
            hr = jnp.concatenate(grs, axis=0)
            hi = jnp.concatenate(gis, axis=0)
            car_ref[cb:cb + 1, :] = cr
            cai_ref[cb:cb + 1, :] = ci
        else:
            sre_ref[:, cb:cb + 1, :] = hr.reshape(G, T, LANES)[:, T - 1:T, :]
            sim_ref[:, cb:cb + 1, :] = hi.reshape(G, T, LANES)[:, T - 1:T, :]
        y = y + _dot(hr.astype(BF16), cre_ref[cb]) - _dot(hi.astype(BF16), cim_ref[cb])

    z = _gelu_tanh(y)
    out = z * jax.nn.sigmoid(_dot(z.astype(BF16), gw_ref[...]) + gb_ref[...])
    yd_ref[...] = out.astype(yd_ref.dtype)

    if chained:
        @pl.when(c == pl.num_programs(1) - 1)
        def _():
            sre_ref[0] = car_ref[...]
            sim_ref[0] = cai_ref[...]


def ssm_mixer(z, xs_block, row0, Bt, T, consts, h0):
    (bre3, bim3, cre3, cim3, lre3, lim3, pre3, pim3, dvec, gw, gb) = consts
    NB = bre3.shape[0]
    W = dvec.shape[1]
    chained = h0 is None
    assert chained or T == SUBLANES
    R = SSM_CHUNK if chained else SSM_SEQ_ROWS
    rb0 = row0 // R

    def full(a):
        nd = a.ndim
        return pl.BlockSpec(a.shape, lambda *_: (0,) * nd)

    if chained:
        nchunk = T // R
        grid = (Bt, nchunk)
        xs_spec = pl.BlockSpec((R, W), lambda b, c: (rb0 + b * nchunk + c, xs_block))
        st_args, st_specs = [pre3, pim3], [full(pre3), full(pim3)]
        yd_spec = pl.BlockSpec((R, W), lambda b, c: (b * nchunk + c, 0))
        s_spec = pl.BlockSpec((1, NB, LANES), lambda b, c: (b, 0, 0))
        scratch = [pltpu.VMEM((NB, LANES), F32), pltpu.VMEM((NB, LANES), F32)]
        sem = ("arbitrary", "arbitrary")
        Tk = R
    else:
        G = R // T
        grid = (Bt // G,)
        xs_spec = pl.BlockSpec((R, W), lambda i: (rb0 + i, xs_block))
        st_args = list(h0)
        st_specs = [pl.BlockSpec((G, NB, LANES), lambda i: (i, 0, 0))] * 2
        yd_spec = pl.BlockSpec((R, W), lambda i: (i, 0))
        s_spec = pl.BlockSpec((G, NB, LANES), lambda i: (i, 0, 0))
        scratch = []
        sem = ("arbitrary",)
        Tk = T
    shared = [bre3, bim3, cre3, cim3, lre3, lim3]
    tailc = [dvec, gw, gb]
    s_shape = jax.ShapeDtypeStruct((Bt, NB, LANES), F32)
    return pl.pallas_call(
        functools.partial(_ssm_kernel, R=R, T=Tk, NB=NB, chained=chained),
        grid=grid,
        in_specs=[xs_spec] + [full(a) for a in shared] + st_specs + [full(a) for a in tailc],
        out_specs=[yd_spec, s_spec, s_spec],
        out_shape=[jax.ShapeDtypeStruct((Bt * T, W), BF16), s_shape, s_shape],
        scratch_shapes=scratch,
        compiler_params=_cparams(sem),
        name="ssm_chained" if chained else "ssm_stateful",
    )(z, *shared, *st_args, *tailc)


def _sort_keys(score):
    b = pltpu.bitcast(score, I32)
    key = jnp.where(b < 0, b ^ jnp.int32(0x7FFFFFFF), b)
    return jnp.where(key == -1, 0, key)


def _kth_largest_key(key, k):
    rows = key.shape[0]

    def body(i, t):
        cand = t + (jnp.int32(1) << (31 - i))
        cnt = jnp.sum(jnp.where(key >= cand, 1.0, 0.0), axis=-1, keepdims=True)
        return jnp.where(cnt >= float(k), cand, t)

    return lax.fori_loop(0, 32, body, jnp.full((rows, 1), -2 ** 31, I32))


def _blocked_prefix(eq, tri, offset):
    outs = []
    run = offset
    for j in range(eq.shape[1] // LANES):
        blk = eq[:, j * LANES:(j + 1) * LANES].astype(BF16)
        pj = _dot(blk, tri) + run
        outs.append(pj)
        run = pj[:, LANES - 1:LANES]
    return outs, run


def _attn_prompt_kernel(q_ref, iq_ref, tail_ref, k_ref, v_ref, ik_ref, gq_ref, e_ref, tri_ref, o_ref,
                        kb, vb, ikb, bias, *, TQ, L, dh, iw_off, n_sel, n_buckets):
    qb = pl.program_id(1)

    @pl.when(qb == 0)
    def _():
        kb[...] = k_ref[0].astype(BF16)
        vb[...] = v_ref[0].astype(BF16)
        ikb[...] = ik_ref[0].astype(BF16)

    scale = dh ** -0.5
    q = q_ref[...]
    q2 = q * q
    hi = q2.astype(BF16)
    lo = (q2 - hi.astype(F32)).astype(BF16)
    ss = _dot(hi, e_ref[...]) + _dot(lo, e_ref[...])
    qn = (q * lax.rsqrt(ss * (1.0 / dh) + EPS) * gq_ref[...] * scale).astype(BF16)
    iqs = (iq_ref[...] * scale).astype(BF16)
    iw = tail_ref[:, iw_off:iw_off + IDX_HEADS] * (IDX_HEADS ** -0.5)

    def attend(Lk):
        score = None
        for h in range(IDX_HEADS):
            lg = _dot_nt(iqs[:, h * dh:(h + 1) * dh], ikb[0:Lk, :])
            t = jnp.maximum(lg, 0.0) * iw[:, h:h + 1]
            score = t if score is None else score + t
        col = lax.broadcasted_iota(I32, (TQ, Lk), 1)
        qpos = qb * TQ + lax.broadcasted_iota(I32, (TQ, Lk), 0)
        causal = col <= qpos
        score = jnp.where(causal, score, NEG_INF)

        key = _sort_keys(score)
        thr = _kth_largest_key(key, n_sel)
        gt = key > thr
        eq = key == thr
        need = float(n_sel) - jnp.sum(jnp.where(gt, 1.0, 0.0), axis=-1, keepdims=True)
        pref, _ = _blocked_prefix(jnp.where(eq, 1.0, 0.0), tri_ref[...], jnp.zeros((TQ, 1), F32))
        pref = jnp.concatenate(pref, axis=1)
        sel = (gt | (eq & (pref <= need))) & causal
        bias[:, 0:Lk] = jnp.where(sel, 0.0, NEG_INF)

        for h in range(N_HEADS_C):
            lg = _dot_nt(qn[:, h * dh:(h + 1) * dh], kb[0:Lk, :]) + bias[:, 0:Lk]
            m = jnp.max(lg, axis=-1, keepdims=True)
            e = jnp.exp(lg - m)
            den = jnp.sum(e, axis=-1, keepdims=True)
            o = _dot(e.astype(BF16), vb[0:Lk, :]) / den
            o_ref[:, h * dh:(h + 1) * dh] = o.astype(o_ref.dtype)

    per = (L // TQ) // n_buckets
    for bk in range(n_buckets):
        pl.when(qb // per == bk)(functools.partial(attend, (bk + 1) * per * TQ))


def attn_prompt(z, q_block, iq_block, tail_block, tail_w, iw_off, kn, v, ik, gq_tiled, B, T, W, dh, TQ=256,
                n_buckets=4):
    nq = T // TQ
    n_sel = min(TOPK_MAX, T // 4)
    eye = (jnp.arange(W)[:, None] // dh == jnp.arange(W)[None, :] // dh).astype(BF16)
    tri = (jnp.arange(LANES)[:, None] <= jnp.arange(LANES)[None, :]).astype(BF16)
    kspec = pl.BlockSpec((1, T, dh), lambda b, i: (b, 0, 0))
    return pl.pallas_call(
        functools.partial(_attn_prompt_kernel, TQ=TQ, L=T, dh=dh, iw_off=iw_off, n_sel=n_sel, n_buckets=n_buckets),
        grid=(B, nq),
        in_specs=[pl.BlockSpec((TQ, W), lambda b, i: (b * nq + i, q_block)),
                  pl.BlockSpec((TQ, W), lambda b, i: (b * nq + i, iq_block)),
                  pl.BlockSpec((TQ, tail_w), lambda b, i: (b * nq + i, tail_block)),
                  kspec, kspec, kspec,
                  pl.BlockSpec((1, W), lambda b, i: (0, 0)),
                  pl.BlockSpec((W, W), lambda b, i: (0, 0)),
                  pl.BlockSpec((LANES, LANES), lambda b, i: (0, 0))],
        out_specs=pl.BlockSpec((TQ, W), lambda b, i: (b * nq + i, 0)),
        out_shape=jax.ShapeDtypeStruct((B * T, W), BF16),
        scratch_shapes=[pltpu.VMEM((T, dh), BF16), pltpu.VMEM((T, dh), BF16), pltpu.VMEM((T, dh), BF16),
                        pltpu.VMEM((TQ, T), F32)],
        compiler_params=_cparams(("arbitrary", "arbitrary")),
        name="attn_prompt",
    )(z, z, z, kn, v, ik, gq_tiled, eye, tri)


def _xattn_kernel(q_ref, mk_ref, mv_ref, g_ref, o_ref, *, G, T, dh, scale):
    g = g_ref[...]
    for s in range(G):
        rows = slice(s * T, (s + 1) * T)
        for h in range(X_HEADS):
            cols = slice(h * dh, (h + 1) * dh)
            qh = q_ref[rows, cols]
            ms = jnp.mean(qh * qh, axis=-1, keepdims=True)
            qn = (qh * lax.rsqrt(ms + EPS) * g).astype(BF16)
            lg = _dot_nt(qn, mk_ref[s, :, cols].astype(BF16)) * scale
            m = jnp.max(lg, axis=-1, keepdims=True)
            e = jnp.exp(lg - m)
            den = jnp.sum(e, axis=-1, keepdims=True)
            o = _dot(e.astype(BF16), mv_ref[s, :, cols].astype(BF16)) / den
            o_ref[rows, cols] = o.astype(o_ref.dtype)


def _xattn_rows_kernel(q_ref, mk_ref, mv_ref, g_ref, o_ref, *, G, T, dh, scale):
    g = g_ref[...]
    H = X_HEADS
    n = mk_ref.shape[1]
    own = (lax.broadcasted_iota(I32, (H * T, n), 1) % H) == (lax.broadcasted_iota(I32, (H * T, n), 0) // T)
    for s in range(G):
        rows = slice(s * T, (s + 1) * T)
        parts = []
        for h in range(H):
            qh = q_ref[rows, h * dh:(h + 1) * dh]
            ms = jnp.mean(qh * qh, axis=-1, keepdims=True)
            parts.append(qh * lax.rsqrt(ms + EPS) * g)
        qs = jnp.concatenate(parts, axis=0).astype(BF16)
        lg = jnp.where(own, _dot_nt(qs, mk_ref[s].astype(BF16)) * scale, NEG_INF)
        m = jnp.max(lg, axis=-1, keepdims=True)
        e = jnp.exp(lg - m)
        den = jnp.sum(e, axis=-1, keepdims=True)
        o = _dot(e.astype(BF16), mv_ref[s].astype(BF16)) / den
        for h in range(H):
            o_ref[rows, h * dh:(h + 1) * dh] = o[h * T:(h + 1) * T, :].astype(o_ref.dtype)


def cross_attention(qx, row0, Bt, T, G, TQ, mem_k, mem_v, layer, gq):
    Wx = qx.shape[1]
    dh = Wx // X_HEADS
    n_rows, wm = mem_k.shape[2:]
    assert G == 1 or TQ == T
    R = G * TQ
    nt = T // TQ
    rb0 = row0 // R
    mspec = pl.BlockSpec((None, G, n_rows, wm), lambda i, t: (layer, i, 0, 0))
    body = _xattn_kernel if wm == Wx else _xattn_rows_kernel
    return pl.pallas_call(
        functools.partial(body, G=G, T=TQ, dh=dh, scale=dh ** -0.5),
        grid=(Bt // G, nt),
        in_specs=[pl.BlockSpec((R, Wx), lambda i, t: (rb0 + i * nt + t, 0)), mspec, mspec,
                  pl.BlockSpec((1, dh), lambda i, t: (0, 0))],
        out_specs=pl.BlockSpec((R, Wx), lambda i, t: (i * nt + t, 0)),
        out_shape=jax.ShapeDtypeStruct((Bt * T, Wx), BF16),
        compiler_params=_cparams(("arbitrary", "arbitrary")),
        name="cross_attention",
    )(qx, mem_k, mem_v, gq.reshape(1, dh))


def _attn_sample_kernel(pt_ref, q_ref, iq_ref, tail_ref, kn_ref, vn_ref, ikn_ref, ck_hbm, cv_hbm, cik_hbm,
                        gq_ref, e_ref, tri_ref, o_ref,
                        kp, vp, ikp, qs_scr, iqs_scr, iw_scr, score, bias, sems,
                        *, G, T, dh, layer, n_pages, hp, iw_off, n_sel):
    step = pl.program_id(0)
    R = G * T
    H = N_HEADS_C
    LP = n_pages * hp
    LC = LP + LANES

    def page_copies(g, p):
        page = pt_ref[step * G + g, p]
        return [pltpu.make_async_copy(src.at[layer, page], dst.at[g, :, p * hp:(p + 1) * hp], sems.at[s, g])
                for s, (src, dst) in enumerate(((ck_hbm, kp), (cv_hbm, vp), (cik_hbm, ikp)))]

    def start_seq(g, c):
        for p in range(n_pages):
            for cp in page_copies(g, p):
                cp.start()
        return c

    def wait_seq(g):
        for p in range(n_pages):
            for cp in page_copies(g, p):
                cp.wait()

    lax.fori_loop(0, G, start_seq, 0)

    def padded_new(ref, rows):
        return jnp.concatenate([ref[rows, :], jnp.zeros((LANES - T, dh), F32)], axis=0).astype(BF16)

    scale = dh ** -0.5
    q = q_ref[...]
    q2 = q * q
    hi = q2.astype(BF16)
    lo = (q2 - hi.astype(F32)).astype(BF16)
    ss = _dot(hi, e_ref[...]) + _dot(lo, e_ref[...])
    qs_scr[...] = q * lax.rsqrt(ss * (1.0 / dh) + EPS) * gq_ref[...] * scale
    iqs_scr[...] = iq_ref[...] * scale
    iw_blk = (iw_off // LANES) * LANES
    iw_lane = iw_off - iw_blk
    iw_scr[...] = tail_ref[:, iw_blk:iw_blk + LANES]

    new_ok = lax.broadcasted_iota(I32, (R, LANES), 1) <= lax.broadcasted_iota(I32, (R, LANES), 0) % T

    def stack_heads(ref, rows):
        return jnp.concatenate([ref[rows, h * dh:(h + 1) * dh] for h in range(H)], axis=0).astype(BF16)

    def score_body(g, c):
        wait_seq(g)
        rows = pl.ds(pl.multiple_of(g * T, T), T)
        qs = stack_heads(iqs_scr, rows)
        lg_p = _dot(qs, ikp[g].astype(BF16))
        lg_n = _dot_nt(qs, padded_new(ikn_ref, rows))
        iw = iw_scr[rows, :] * (IDX_HEADS ** -0.5)
        sp = sn = None
        for h in range(IDX_HEADS):
            w = iw[:, iw_lane + h:iw_lane + h + 1]
            hs = slice(h * T, (h + 1) * T)
            tp = jnp.maximum(lg_p[hs], 0.0) * w
            tn = jnp.maximum(lg_n[hs], 0.0) * w
            sp, sn = (tp, tn) if sp is None else (sp + tp, sn + tn)
        score[rows, 0:LP] = sp
        score[rows, LP:LC] = sn
        return c

    lax.fori_loop(0, G, score_body, 0, unroll=2)
    score[:, LP:LC] = jnp.where(new_ok, score[:, LP:LC], NEG_INF)

    key = _sort_keys(score[...])
    thr = _kth_largest_key(key, n_sel)
    gt = key > thr
    eq = key == thr
    need = float(n_sel) - jnp.sum(jnp.where(gt, 1.0, 0.0), axis=-1, keepdims=True)
    pref, _ = _blocked_prefix(jnp.where(eq, 1.0, 0.0), tri_ref[...], jnp.zeros((R, 1), F32))
    pref = jnp.concatenate(pref, axis=1)
    bias[...] = jnp.where(gt | (eq & (pref <= need)), 0.0, NEG_INF)
    bias[:, LP:LC] = jnp.where(new_ok, bias[:, LP:LC], NEG_INF)

    def attn_body(g, c):
        rows = pl.ds(pl.multiple_of(g * T, T), T)
        qs = stack_heads(qs_scr, rows)
        bh = jnp.concatenate([bias[rows, :]] * H, axis=0)
        lg_p = _dot(qs, kp[g].astype(BF16)) + bh[:, 0:LP]
        lg_n = _dot_nt(qs, padded_new(kn_ref, rows)) + bh[:, LP:LC]
        m = jnp.maximum(jnp.max(lg_p, axis=-1, keepdims=True), jnp.max(lg_n, axis=-1, keepdims=True))
        ep = jnp.exp(lg_p - m)
        en = jnp.exp(lg_n - m)
        den = jnp.sum(ep, axis=-1, keepdims=True) + jnp.sum(en, axis=-1, keepdims=True)
        o = (_dot_nt(ep.astype(BF16), vp[g].astype(BF16)) + _dot(en.astype(BF16), padded_new(vn_ref, rows))) / den
        for h in range(H):
            o_ref[rows, h * dh:(h + 1) * dh] = o[h * T:(h + 1) * T, :].astype(o_ref.dtype)
        return c

    lax.fori_loop(0, G, attn_body, 0, unroll=2)


def attn_sample(z, row0, q_block, iq_block, tail_block, tail_w, iw_off, kn, v, ik, caches, page_table, layer,
                gq_tiled, Bs, T, W, dh, G=16):
    n_pages = page_table.shape[1]
    page = caches[0].shape[2]
    caches_t = [c.transpose(0, 1, 3, 2) for c in caches]
    R = G * T
    rb0 = row0 // R
    LP = n_pages * page
    LC = LP + LANES
    n_sel = min(TOPK_MAX, (LP + T) // 4)
    eye = (jnp.arange(W)[:, None] // dh == jnp.arange(W)[None, :] // dh).astype(BF16)
    tri = (jnp.arange(LANES)[:, None] <= jnp.arange(LANES)[None, :]).astype(BF16)
    any_spec = pl.BlockSpec(memory_space=pl.ANY)
    nspec = pl.BlockSpec((R, dh), lambda i, pt: (rb0 + i, 0))
    grid_spec = pltpu.PrefetchScalarGridSpec(
        num_scalar_prefetch=1,
        grid=(Bs // G,),
        in_specs=[pl.BlockSpec((R, W), lambda i, pt: (rb0 + i, q_block)),
                  pl.BlockSpec((R, W), lambda i, pt: (rb0 + i, iq_block)),
                  pl.BlockSpec((R, tail_w), lambda i, pt: (rb0 + i, tail_block)),
                  nspec, nspec, nspec, any_spec, any_spec, any_spec,
                  pl.BlockSpec((1, W), lambda i, pt: (0, 0)),
                  pl.BlockSpec((W, W), lambda i, pt: (0, 0)),
                  pl.BlockSpec((LANES, LANES), lambda i, pt: (0, 0))],
        out_specs=pl.BlockSpec((R, W), lambda i, pt: (i, 0)),
        scratch_shapes=[pltpu.VMEM((G, dh, LP), F32), pltpu.VMEM((G, dh, LP), F32), pltpu.VMEM((G, dh, LP), F32),
                        pltpu.VMEM((R, W), F32), pltpu.VMEM((R, W), F32), pltpu.VMEM((R, LANES), F32),
                        pltpu.VMEM((R, LC), F32), pltpu.VMEM((R, LC), F32),
                        pltpu.SemaphoreType.DMA((3, G))],
    )
    return pl.pallas_call(
        functools.partial(_attn_sample_kernel, G=G, T=T, dh=dh, layer=layer, n_pages=n_pages, hp=page,
                          iw_off=iw_off, n_sel=n_sel),
        grid_spec=grid_spec,
        out_shape=jax.ShapeDtypeStruct((Bs * T, W), BF16),
        compiler_params=_cparams(("arbitrary",)),
        name="attn_sample",
    )(page_table, z, z, z, kn, v, ik, *caches_t, gq_tiled, eye, tri)


def _block_diag(blocks):
    G, a, b = blocks.shape
    eye = jnp.eye(G, dtype=blocks.dtype)
    return (eye[:, None, :, None] * blocks[:, :, None, :]).reshape(G * a, G * b)


def _lane_blocks(a):
    rows, n = a.shape
    return a.reshape(rows, n // LANES, LANES).transpose(1, 0, 2)


def kernel(x_prompt, x_sample, cache_attn_k, cache_attn_v, cache_idx_k, cache_mem_k, cache_mem_v, state_pool,
           state_conv, state_ssm_re, state_ssm_im, page_table, mem_prompt, norm_g, ffn_in, ffn_out, w_in,
           q_norm_g, k_norm_g, pool_mix, pool_scale, conv_w, ssm_a_re, ssm_a_im, ssm_log_step, ssm_b_re,
           ssm_b_im, ssm_c_re, ssm_c_im, ssm_d, ssm_glu_w, ssm_glu_b, w_branch, w_gate, b_gate, w_o,
           mem_norm_g, w_xq, w_xk, w_xv, xq_norm_g, xk_norm_g, w_xo):
    B, T, D = x_prompt.shape
    Bs, Ts, _ = x_sample.shape
    depth = norm_g.shape[0]
    Mp, Ms = B * T, Bs * Ts
    W = pool_scale.shape[1]
    dh = k_norm_g.shape[1]
    d_ff = ffn_out.shape[2]
    n_mem = mem_prompt.shape[1]
    Wx = w_xq.shape[2]
    SG, SN = ssm_a_re.shape[1:]
    NB = SG * SN // LANES
    past_len = page_table.shape[1] * cache_attn_k.shape[2]
    assert cache_idx_k.shape[-1] == dh and W == N_HEADS_C * dh == IDX_HEADS * dh
    TM = 1024

    o_k = 5 * W
    o_iq = o_k + 2 * dh
    o_ik = o_iq + W
    o_xs = o_ik + dh + IDX_HEADS
    tail_w = 2 * LANES
    Q_BLK, IQ_BLK, XS_BLK = 4, 5, 6
    TAIL_BLK = 7 * W // tail_w
    IW_OFF = 3 * dh
    n_z = 7 * W + tail_w

    h = (x_prompt.reshape(Mp, D), x_sample.reshape(Ms, D))
    caches = (cache_attn_k, cache_attn_v, cache_idx_k)
    mem_rows = mem_prompt.reshape(B * n_mem, D)
    outs = [[] for _ in range(16)]

    def ffn(h, l, i, g, split_out_rows=None):
        act = matmul(h, [(ffn_in, (l, i), 0), (ffn_in, (l, i), d_ff // 512)], n_out=d_ff, tn=512, tm=TM,
                     mode="swiglu", out_dtype=BF16, norm_g=g, name="ffn_in")
        return matmul(act, [(ffn_out, (l, i), 0)], n_out=D, tn=512, tm=TM, mode="residual", aux=h, scale=0.5,
                      split_out_rows=split_out_rows, name="ffn_out")

    for l in range(depth):
        h = ffn(h, l, 0, norm_g[l, 0])

        wl = w_in[l]
        w_in2 = jnp.concatenate([wl[:, 0:o_k], wl[:, o_iq:o_iq + W], wl[:, o_xs:o_xs + W], wl[:, o_k:o_k + 2 * dh],
                                 wl[:, o_ik:o_ik + dh + IDX_HEADS],
                                 jnp.zeros((D, tail_w - 3 * dh - IDX_HEADS), F32)], axis=1)
        z = matmul(h, [(w_in2, (), 0)], n_out=n_z, tn=1280, tm=TM, norm_g=norm_g[l, 1], name="in_proj")

        wmix = pool_mix[l].astype(BF16)
        ya_p, yb_p, npool_p, nconv_p = pool_conv(z, 0, B, T, 1, 512, 0, None, None, wmix, pool_scale[l], conv_w[l], W)
        ya_s, yb_s, npool_s, nconv_s = pool_conv(z, Mp, Bs, Ts, 16, Ts, past_len, state_pool[l], state_conv[l],
                                                 wmix, pool_scale[l], conv_w[l], W)

        kn, vv, ik = kv_prep(z, TAIL_BLK, tail_w, k_norm_g[l], dh)
        gq_tiled = jnp.tile(q_norm_g[l], N_HEADS_C).reshape(1, W)
        yc_p = attn_prompt(z, Q_BLK, IQ_BLK, TAIL_BLK, tail_w, IW_OFF, kn[:Mp].reshape(B, T, dh),
                           vv[:Mp].reshape(B, T, dh), ik[:Mp].reshape(B, T, dh), gq_tiled, B, T, W, dh)
        yc_s = attn_sample(z, Mp, Q_BLK, IQ_BLK, TAIL_BLK, tail_w, IW_OFF, kn, vv, ik, caches, page_table, l,
                           gq_tiled, Bs, Ts, W, dh)

        ab_re, ab_im, co_re, co_im = ssm_params(ssm_a_re[l], ssm_a_im[l], ssm_log_step[l])
        flat = lambda a: a.reshape(1, SG * SN)
        lev_re, lev_im, pw_re, pw_im = ssm_tables(flat(ab_re), flat(ab_im), SUBLANES)
        bb_re, bb_im = ssm_fold_input(_block_diag(ssm_b_re[l].transpose(0, 2, 1)),
                                      _block_diag(ssm_b_im[l].transpose(0, 2, 1)), flat(co_re), flat(co_im))
        consts = (
            _lane_blocks(bb_re), _lane_blocks(bb_im),
            _block_diag(ssm_c_re[l].transpose(0, 2, 1)).reshape(NB, LANES, W).astype(BF16),
            _block_diag(ssm_c_im[l].transpose(0, 2, 1)).reshape(NB, LANES, W).astype(BF16),
            _lane_blocks(lev_re), _lane_blocks(lev_im),
            _lane_blocks(pw_re), _lane_blocks(pw_im),
            ssm_d[l].reshape(1, W), ssm_glu_w[l].astype(BF16), ssm_glu_b[l].reshape(1, W))
        yd_p, sre_p, sim_p = ssm_mixer(z, XS_BLK, 0, B, T, consts, None)
        h0 = (state_ssm_re[l].reshape(Bs, NB, LANES), state_ssm_im[l].reshape(Bs, NB, LANES))
        yd_s, sre_s, sim_s = ssm_mixer(z, XS_BLK, Mp, Bs, Ts, consts, h0)

        merged = gated_merge(h, norm_g[l, 1], [(ya_p, ya_s), (yb_p, yb_s), (yc_p, yc_s), (yd_p, yd_s)],
                             w_gate, b_gate, w_branch, l)
        h = matmul(merged, [(w_o, (l,), 0)], n_out=D, tn=1024, tm=TM, mode="residual", aux=h, scale=1.0, name="w_o")

        mn = rmsnorm_rows(mem_rows, mem_norm_g[l])
        mk = matmul(mn, [(w_xk, (l,), 0)], n_out=Wx, tn=Wx, tm=B * n_mem, mode="headnorm",
                    aux=jnp.tile(xk_norm_g[l], X_HEADS).reshape(1, Wx), group=Wx // X_HEADS, name="mem_k")
        mv = matmul(mn, [(w_xv, (l,), 0)], n_out=Wx, tn=Wx, tm=B * n_mem, name="mem_v")
        qx = matmul(h, [(w_xq, (l,), 0)], n_out=Wx, tn=Wx, tm=TM, norm_g=norm_g[l, 2], name="w_xq")
        xa_p = cross_attention(qx, 0, B, T, 1, 512, mk.reshape(1, B, n_mem, Wx), mv.reshape(1, B, n_mem, Wx), 0,
                               xq_norm_g[l])
        xa_s = cross_attention(qx, Mp, Bs, Ts, 8, Ts, cache_mem_k.reshape(depth, Bs, n_mem * X_HEADS, Wx // X_HEADS),
                               cache_mem_v.reshape(depth, Bs, n_mem * X_HEADS, Wx // X_HEADS), l, xq_norm_g[l])
        h = matmul((xa_p, xa_s), [(w_xo, (l,), 0)], n_out=D, tn=1024, tm=TM, mode="residual", aux=h, scale=1.0,
                   name="w_xo")

        h = ffn(h, l, 1, norm_g[l, 3], split_out_rows=Mp if l == depth - 1 else None)

        xh = Wx // X_HEADS
        layer_out = (kn[:Mp].reshape(B, T, dh), vv[:Mp].reshape(B, T, dh), ik[:Mp].reshape(B, T, dh),
                     mk.reshape(B, n_mem, X_HEADS, xh), mv.reshape(B, n_mem, X_HEADS, xh), npool_p, nconv_p,
                     sre_p.reshape(B, SG, SN), sim_p.reshape(B, SG, SN),
                     kn[Mp:].reshape(Bs, Ts, dh), vv[Mp:].reshape(Bs, Ts, dh), ik[Mp:].reshape(Bs, Ts, dh),
                     npool_s, nconv_s, sre_s.reshape(Bs, SG, SN), sim_s.reshape(Bs, SG, SN))
        for acc, val in zip(outs, layer_out):
            acc.append(val)

    return (h[0].reshape(B, T, D), h[1].reshape(Bs, Ts, D)) + tuple(jnp.stack(o) for o in outs)
```

```python
import functools
import math

import jax
import jax.numpy as jnp
from jax import lax
from jax.experimental import pallas as pl
from jax.experimental.pallas import tpu as pltpu

F32, BF16, I32 = jnp.float32, jnp.bfloat16, jnp.int32
EPS = 1e-6
NEG_INF = float("-inf")

V7X_VMEM_BYTES = 64 * 1024 * 1024
VMEM_LIMIT = V7X_VMEM_BYTES - 8 * 1024 * 1024
LANES = 128
SUBLANES = 8

POOL_WINDOWS = (2, 4, 8, 16)
POOL_HIST = 16
CONV_WIDTH = 3
CONV_HIST = 8
N_HEADS_C = 8
IDX_HEADS = 8
TOPK_MAX = 256
X_HEADS = 4
SSM_CHUNK = 256
SSM_SEQ_ROWS = 128


def _cparams(sem, vmem=VMEM_LIMIT):
    return pltpu.CompilerParams(dimension_semantics=sem, vmem_limit_bytes=vmem)


def _dot(a, b):
    return jnp.dot(a, b, preferred_element_type=F32)


def _dot_nt(a, b):
    return lax.dot_general(a, b, (((1,), (1,)), ((), ())), preferred_element_type=F32)


def _rmsnorm_kernel(x_ref, g_ref, o_ref):
    x = x_ref[...]
    ms = jnp.mean(x * x, axis=-1, keepdims=True)
    o_ref[...] = (x * lax.rsqrt(ms + EPS) * g_ref[...]).astype(o_ref.dtype)


def rmsnorm_rows(x, g, tm=512):
    M, D = x.shape
    return pl.pallas_call(
        _rmsnorm_kernel,
        grid=(M // tm,),
        in_specs=[pl.BlockSpec((tm, D), lambda i: (i, 0)), pl.BlockSpec((1, D), lambda i: (0, 0))],
        out_specs=pl.BlockSpec((tm, D), lambda i: (i, 0)),
        out_shape=jax.ShapeDtypeStruct((M, D), BF16),
        compiler_params=_cparams(("arbitrary",)),
        name="rmsnorm",
    )(x, g.reshape(1, D))


def _rms_rows_bf16(x, g):
    ms = jnp.mean(x * x, axis=-1, keepdims=True)
    return (x * lax.rsqrt(ms + EPS) * g).astype(BF16)


def _pick_rows(ref_a, ref_b, rows, n_a_tiles):
    if ref_b is None:
        return ref_a[rows, :]
    return jnp.where(pl.program_id(1) >= n_a_tiles, ref_b[rows, :], ref_a[rows, :])


def _split_row_specs(xa, xb, tm, tn=None):
    na = xa.shape[0] // tm
    width = xa.shape[1] if tn is None else tn
    col = (lambda j: 0) if tn is None else (lambda j: j)
    spec_a = pl.BlockSpec((tm, width), lambda j, i: (jnp.minimum(i, na - 1), col(j)))
    spec_b = pl.BlockSpec((tm, width), lambda j, i: (jnp.maximum(i - na, 0), col(j)),
                          pipeline_mode=pl.Buffered(1) if (xb.shape[0] == tm and tn is None) else None)
    return spec_a, spec_b, na


def _mm_kernel(*refs, nw, mode, scale, sub, group, norm, n_a_tiles, aux_a_tiles, out_a_tiles):
    x_ref = refs[0]
    xb_ref = None
    if n_a_tiles is not None:
        xb_ref = refs[1]
        refs = refs[1:]
    w_refs = refs[1:1 + nw]
    p = 1 + nw
    if norm:
        g_ref = refs[p]
        p += 1
    aux_ref = auxb_ref = None
    if mode in ("bias_sigmoid", "residual", "headnorm"):
        aux_ref = refs[p]
        p += 1
        if aux_a_tiles is not None:
            auxb_ref = refs[p]
            p += 1
    o_ref = refs[p]
    ob_ref = None
    if out_a_tiles is not None:
        ob_ref = refs[p + 1]
        p += 1
    scr = refs[p + 1:p + 1 + nw]

    @pl.when(pl.program_id(1) == 0)
    def _():
        for w_ref, s in zip(w_refs, scr):
            s[...] = w_ref[...].astype(BF16)

    tm = x_ref.shape[0]

    def body(xr, r, carry):
        rows = pl.ds(pl.multiple_of(r * sub, sub), sub)
        x = xr[rows, :]
        if norm:
            x = _rms_rows_bf16(x, g_ref[...])
        acc = [_dot(x, s[...]) for s in scr]
        if mode == "swiglu":
            a, b = acc
            y = (a * jax.nn.sigmoid(a)) * b
        elif mode == "bias_sigmoid":
            y = jax.nn.sigmoid(acc[0] + aux_ref[...])
        elif mode == "residual":
            y = _pick_rows(aux_ref, auxb_ref, rows, aux_a_tiles) + scale * acc[0]
        elif mode == "headnorm":
            a = acc[0]
            parts = []
            for h in range(a.shape[1] // group):
                ah = a[:, h * group:(h + 1) * group]
                ms = jnp.mean(ah * ah, axis=-1, keepdims=True)
                parts.append(ah * lax.rsqrt(ms + EPS))
            y = jnp.concatenate(parts, axis=1) * aux_ref[...]
        else:
            y = acc[0]
        if ob_ref is None:
            o_ref[rows, :] = y.astype(o_ref.dtype)
        else:
            @pl.when(pl.program_id(1) < out_a_tiles)
            def _():
                o_ref[rows, :] = y.astype(o_ref.dtype)

            @pl.when(pl.program_id(1) >= out_a_tiles)
            def _():
                ob_ref[rows, :] = y.astype(ob_ref.dtype)
        return carry

    def run(xr):
        lax.fori_loop(0, tm // sub, functools.partial(body, xr), 0, unroll=True)

    if xb_ref is None:
        run(x_ref)
    else:
        pl.when(pl.program_id(1) < n_a_tiles)(functools.partial(run, x_ref))
        pl.when(pl.program_id(1) >= n_a_tiles)(functools.partial(run, xb_ref))


def matmul(x, weights, *, n_out, tn, tm, mode="plain", aux=None, scale=1.0, out_dtype=F32, group=LANES,
           norm_g=None, split_out_rows=None, name="matmul"):
    nw = len(weights)
    sub = min(tm, 256)
    n_a_tiles = None
    if isinstance(x, tuple):
        xa, xb = x
        spec_a, spec_b, n_a_tiles = _split_row_specs(xa, xb, tm)
        M, K = xa.shape[0] + xb.shape[0], xa.shape[1]
        in_specs = [spec_a, spec_b]
        args = [xa, xb]
    else:
        M, K = x.shape
        in_specs = [pl.BlockSpec((tm, K), lambda j, i: (i, 0))]
        args = [x]
    for arr, lead, coff in weights:
        nl = len(lead)
        in_specs.append(pl.BlockSpec((None,) * nl + (K, tn),
                                     functools.partial(lambda j, i, lead, coff: (*lead, 0, coff + j), lead=lead, coff=coff)))
        args.append(arr)
    if norm_g is not None:
        in_specs.append(pl.BlockSpec((1, K), lambda j, i: (0, 0)))
        args.append(norm_g.reshape(1, K))
    if mode in ("bias_sigmoid", "headnorm"):
        in_specs.append(pl.BlockSpec((1, tn), lambda j, i: (0, j)))
        args.append(aux)
    aux_a_tiles = None
    if mode == "residual" and isinstance(aux, tuple):
        spec_a, spec_b, aux_a_tiles = _split_row_specs(aux[0], aux[1], tm, tn)
        in_specs += [spec_a, spec_b]
        args += list(aux)
    elif mode == "residual":
        in_specs.append(pl.BlockSpec((tm, tn), lambda j, i: (i, j)))
        args.append(aux)
    out_a_tiles = None
    out_specs = pl.BlockSpec((tm, tn), lambda j, i: (i, j))
    out_shape = jax.ShapeDtypeStruct((M, n_out), out_dtype)
    if split_out_rows is not None:
        out_a_tiles = split_out_rows // tm
        out_specs = [pl.BlockSpec((tm, tn), lambda j, i: (jnp.minimum(i, out_a_tiles - 1), j)),
                     pl.BlockSpec((tm, tn), lambda j, i: (jnp.maximum(i - out_a_tiles, 0), j))]
        out_shape = [jax.ShapeDtypeStruct((split_out_rows, n_out), out_dtype),
                     jax.ShapeDtypeStruct((M - split_out_rows, n_out), out_dtype)]
    return pl.pallas_call(
        functools.partial(_mm_kernel, nw=nw, mode=mode, scale=scale, sub=sub, group=group, norm=norm_g is not None,
                          n_a_tiles=n_a_tiles, aux_a_tiles=aux_a_tiles, out_a_tiles=out_a_tiles),
        grid=(n_out // tn, M // tm),
        in_specs=in_specs,
        out_specs=out_specs,
        out_shape=out_shape,
        scratch_shapes=[pltpu.VMEM((K, tn), BF16) for _ in range(nw)],
        compiler_params=_cparams(("arbitrary", "arbitrary")),
        name=name,
    )(*args)


def _merge_kernel(*refs, nb, n_a_tiles):
    h_ref, g_ref = refs[0:2]
    br = refs[2:2 + nb]
    brb = refs[2 + nb:2 + 2 * nb]
    refs = refs[nb:]
    wg = refs[2 + nb:2 + 2 * nb]
    bg = refs[2 + 2 * nb:2 + 3 * nb]
    wb = refs[2 + 3 * nb:2 + 4 * nb]
    o_ref = refs[2 + 4 * nb]
    sg = refs[3 + 4 * nb:3 + 5 * nb]
    sb = refs[3 + 5 * nb:3 + 6 * nb]

    @pl.when(pl.program_id(1) == 0)
    def _():
        for w_ref, s in zip(wg + wb, sg + sb):
            s[...] = w_ref[...].astype(BF16)

    tm = o_ref.shape[0]
    sub = min(tm, 256)

    def body(r, carry):
        rows = pl.ds(pl.multiple_of(r * sub, sub), sub)
        u = _rms_rows_bf16(h_ref[rows, :], g_ref[...])
        acc = None
        for k in range(nb):
            gate = jax.nn.sigmoid(_dot(u, sg[k][...]) + bg[k][...])
            t = gate * _dot(_pick_rows(br[k], brb[k], rows, n_a_tiles), sb[k][...])
            acc = t if acc is None else acc + t
        o_ref[rows, :] = acc.astype(o_ref.dtype)
        return carry

    lax.fori_loop(0, tm // sub, body, 0, unroll=True)


def gated_merge(h, norm_g, branches, w_gate, b_gate, w_branch, layer, *, tn=256, tm=1024):
    nb = len(branches)
    M = h.shape[0]
    W = branches[0][0].shape[1]
    D = w_branch.shape[-1]
    nj = D // tn
    bias = b_gate.reshape(b_gate.shape[0], 1, nb * D)

    def per_branch(shape):
        return [pl.BlockSpec(shape, functools.partial(lambda j, i, k: (layer, 0, k * nj + j), k=k)) for k in range(nb)]

    row_specs = [_split_row_specs(a, b, tm) for a, b in branches]
    n_a_tiles = row_specs[0][2]
    in_specs = [pl.BlockSpec((tm, D), lambda j, i: (i, 0)), pl.BlockSpec((1, D), lambda j, i: (0, 0))]
    in_specs += [s[0] for s in row_specs] + [s[1] for s in row_specs]
    in_specs += per_branch((None, D, tn)) + per_branch((None, 1, tn))
    in_specs += [pl.BlockSpec((None, None, W, tn), functools.partial(lambda j, i, k: (layer, k, 0, j), k=k))
                 for k in range(nb)]
    return pl.pallas_call(
        functools.partial(_merge_kernel, nb=nb, n_a_tiles=n_a_tiles),
        grid=(nj, M // tm),
        in_specs=in_specs,
        out_specs=pl.BlockSpec((tm, tn), lambda j, i: (i, j)),
        out_shape=jax.ShapeDtypeStruct((M, D), BF16),
        scratch_shapes=[pltpu.VMEM((D, tn), BF16) for _ in range(nb)] + [pltpu.VMEM((W, tn), BF16) for _ in range(nb)],
        compiler_params=_cparams(("arbitrary", "arbitrary")),
        name="gated_merge",
    )(h, norm_g.reshape(1, D), *[a for a, _ in branches], *[b for _, b in branches],
      *([w_gate] * nb), *([bias] * nb), *([w_branch] * nb))


def _kvprep_kernel(t_ref, g_ref, k_ref, v_ref, ik_ref, *, dh):
    t = t_ref[...]
    k = t[:, 0:dh]
    ms = jnp.mean(k * k, axis=-1, keepdims=True)
    k_ref[...] = k * lax.rsqrt(ms + EPS) * g_ref[...]
    v_ref[...] = t[:, dh:2 * dh]
    ik_ref[...] = t[:, 2 * dh:3 * dh]


def kv_prep(z, tail_block, tail_w, gk, dh, tm=1024):
    M = z.shape[0]
    out = jax.ShapeDtypeStruct((M, dh), F32)
    return pl.pallas_call(
        functools.partial(_kvprep_kernel, dh=dh),
        grid=(M // tm,),
        in_specs=[pl.BlockSpec((tm, tail_w), lambda i: (i, tail_block)), pl.BlockSpec((1, dh), lambda i: (0, 0))],
        out_specs=[pl.BlockSpec((tm, dh), lambda i: (i, 0))] * 3,
        out_shape=[out, out, out],
        compiler_params=_cparams(("arbitrary",)),
        name="kv_prep",
    )(z, gk.reshape(1, dh))


def _poolconv_kernel(*refs, G, T, W, pos0, has_state):
    xp_ref, xc_ref, bg_ref, cg_ref = refs[0:4]
    p = 4
    if has_state:
        pbuf_ref, cbuf_ref = refs[4:6]
        p = 6
    wmix_ref, pscale_ref, convw_ref = refs[p:p + 3]
    ya_ref, yb_ref, npool_ref, nconv_ref = refs[p + 3:p + 7]
    fullp, fullc = refs[p + 7:p + 9]
    PH, CH = POOL_HIST, CONV_HIST
    nh = CONV_WIDTH - 1
    gw = W // len(POOL_WINDOWS)
    c = pl.program_id(1)

    @pl.when(c == 0)
    def _():
        if has_state:
            fullp[:, 1:PH, :] = pbuf_ref[...]
            fullc[:, CH - nh:CH, :] = cbuf_ref[...]
        else:
            fullp[:, 0:PH, :] = jnp.zeros((G, PH, W), F32)
            fullc[:, 0:CH, :] = jnp.zeros((G, CH, W), F32)

    @pl.when(c > 0)
    def _():
        fullp[:, 0:PH, :] = fullp[:, T:T + PH, :]
        fullc[:, 0:CH, :] = fullc[:, T:T + CH, :]

    fullp[:, PH:PH + T, :] = xp_ref[...].reshape(G, T, W)
    t_idx = lax.broadcasted_iota(I32, (1, T, 1), 1) + (c * T + (pos0 + 1))
    for gi, w in enumerate(POOL_WINDOWS):
        cols = slice(gi * gw, (gi + 1) * gw)
        acc = fullp[:, PH:PH + T, cols]
        for j in range(1, w):
            acc = acc + fullp[:, PH - j:PH - j + T, cols]
        cnt = jnp.minimum(t_idx, w).astype(F32)
        d = acc / cnt - fullp[:, PH:PH + T, cols]
        y = _dot(d.reshape(G * T, gw).astype(BF16), wmix_ref[gi]) * pscale_ref[:, cols]
        ya_ref[:, cols] = y.astype(ya_ref.dtype)
    npool_ref[...] = fullp[:, T + 1:T + PH, :]

    fullc[:, CH:CH + T, :] = (cg_ref[...] * xc_ref[...]).reshape(G, T, W)
    y = None
    for j in range(CONV_WIDTH):
        wj = convw_ref[j:j + 1, :].reshape(1, 1, W)
        term = wj * fullc[:, CH - nh + j:CH - nh + j + T, :]
        y = term if y is None else y + term
    yb_ref[...] = (bg_ref[...] * y.reshape(G * T, W)).astype(yb_ref.dtype)
    nconv_ref[...] = fullc[:, CH + T - nh:CH + T, :]


def pool_conv(z, row0, Bt, T, G, TC, pos0, pool_buf, conv_buf, wmix_bf16, pool_scale, conv_w, W):
    has_state = pool_buf is not None
    assert G == 1 or TC == T
    R = G * TC
    nc = T // TC
    rb0 = row0 // R
    nh = CONV_WIDTH - 1

    def zspec(cb):
        return pl.BlockSpec((R, W), functools.partial(lambda i, c, cb: (rb0 + i * nc + c, cb), cb=cb))

    in_specs = [zspec(0), zspec(1), zspec(2), zspec(3)]
    args = [z, z, z, z]
    if has_state:
        in_specs += [pl.BlockSpec((G, POOL_HIST - 1, W), lambda i, c: (i, 0, 0)),
                     pl.BlockSpec((G, nh, W), lambda i, c: (i, 0, 0))]
        args += [pool_buf, conv_buf]
    nwin = len(POOL_WINDOWS)
    in_specs += [pl.BlockSpec((nwin, W // nwin, W // nwin), lambda i, c: (0, 0, 0)),
                 pl.BlockSpec((1, W), lambda i, c: (0, 0)),
                 pl.BlockSpec((CONV_WIDTH, W), lambda i, c: (0, 0))]
    args += [wmix_bf16, pool_scale.reshape(1, W), conv_w]
    return pl.pallas_call(
        functools.partial(_poolconv_kernel, G=G, T=TC, W=W, pos0=pos0, has_state=has_state),
        grid=(Bt // G, nc),
        in_specs=in_specs,
        out_specs=[pl.BlockSpec((R, W), lambda i, c: (i * nc + c, 0)), pl.BlockSpec((R, W), lambda i, c: (i * nc + c, 0)),
                   pl.BlockSpec((G, POOL_HIST - 1, W), lambda i, c: (i, 0, 0)),
                   pl.BlockSpec((G, nh, W), lambda i, c: (i, 0, 0))],
        out_shape=[jax.ShapeDtypeStruct((Bt * T, W), BF16), jax.ShapeDtypeStruct((Bt * T, W), BF16),
                   jax.ShapeDtypeStruct((Bt, POOL_HIST - 1, W), F32), jax.ShapeDtypeStruct((Bt, nh, W), F32)],
        scratch_shapes=[pltpu.VMEM((G, POOL_HIST + TC, W), F32), pltpu.VMEM((G, CONV_HIST + TC, W), F32)],
        compiler_params=_cparams(("arbitrary", "arbitrary")),
        name="pool_conv",
    )(*args)


def _ssm_params_kernel(are_ref, aim_ref, ls_ref, abre_ref, abim_ref, core_ref, coim_ref):
    a_re = are_ref[...]
    a_im = aim_ref[...]
    step = jnp.exp(ls_ref[...])
    decay = jnp.exp(step * a_re)
    ab_re = decay * jnp.cos(step * a_im)
    ab_im = decay * jnp.sin(step * a_im)
    den = a_re * a_re + a_im * a_im
    nr = ab_re - 1.0
    abre_ref[...] = ab_re
    abim_ref[...] = ab_im
    core_ref[...] = (nr * a_re + ab_im * a_im) / den
    coim_ref[...] = (ab_im * a_re - nr * a_im) / den


def ssm_params(a_re, a_im, log_step):
    G, N = a_re.shape
    out = jax.ShapeDtypeStruct((G, N), F32)
    return pl.pallas_call(_ssm_params_kernel, out_shape=[out, out, out, out], name="ssm_params")(
        a_re, a_im, log_step.reshape(G, 1))


def _scan_levels(C):
    return [1 << k for k in range(int(math.log2(C)))]


def _ssm_tables_kernel(abre_ref, abim_ref, lre_ref, lim_ref, pre_ref, pim_ref, *, C):
    ar = abre_ref[...]
    ai = abim_ref[...]
    N = ar.shape[1]
    row = lax.broadcasted_iota(I32, (C, N), 0)
    hr = jnp.where(row == 0, ar, 0.0)
    hi = jnp.where(row == 0, ai, 0.0)
    lre_ref[...] = jnp.zeros(lre_ref.shape, F32)
    lim_ref[...] = jnp.zeros(lim_ref.shape, F32)
    for k, s in enumerate(_scan_levels(C)):
        lre_ref[k:k + 1, :] = ar
        lim_ref[k:k + 1, :] = ai
        sr = jnp.where(row >= s, pltpu.roll(hr, s, 0), 0.0)
        si = jnp.where(row >= s, pltpu.roll(hi, s, 0), 0.0)
        hr, hi = hr + ar * sr - ai * si, hi + ar * si + ai * sr
        ar, ai = ar * ar - ai * ai, 2.0 * ar * ai
    pre_ref[...] = hr
    pim_ref[...] = hi


def ssm_tables(ab_re, ab_im, C):
    N = ab_re.shape[1]
    nlev = len(_scan_levels(C))
    lev = jax.ShapeDtypeStruct((SUBLANES * ((nlev + SUBLANES - 1) // SUBLANES), N), F32)
    pw = jax.ShapeDtypeStruct((C, N), F32)
    return pl.pallas_call(functools.partial(_ssm_tables_kernel, C=C), out_shape=[lev, lev, pw, pw],
                          name="ssm_tables")(ab_re, ab_im)


def _ssm_fold_kernel(bre_ref, bim_ref, core_ref, coim_ref, ore_ref, oim_ref):
    bre = bre_ref[...]
    bim = bim_ref[...]
    cor = core_ref[...]
    coi = coim_ref[...]
    ore_ref[...] = (cor * bre - coi * bim).astype(ore_ref.dtype)
    oim_ref[...] = (cor * bim + coi * bre).astype(oim_ref.dtype)


def ssm_fold_input(b_re_bd, b_im_bd, co_re, co_im):
    out = jax.ShapeDtypeStruct(b_re_bd.shape, BF16)
    return pl.pallas_call(_ssm_fold_kernel, out_shape=[out, out], name="ssm_fold")(b_re_bd, b_im_bd, co_re, co_im)


def _gelu_tanh(x):
    return 0.5 * x * (1.0 + jnp.tanh(math.sqrt(2.0 / math.pi) * (x + 0.044715 * (x * x * x))))


def _ssm_kernel(*refs, R, T, NB, chained):
    (xs_ref, bre_ref, bim_ref, cre_ref, cim_ref, lre_ref, lim_ref) = refs[0:7]
    p = 7
    if chained:
        pre_ref, pim_ref = refs[p:p + 2]
        p += 2
    else:
        h0r_ref, h0i_ref = refs[p:p + 2]
        p += 2
    d_ref, gw_ref, gb_ref = refs[p:p + 3]
    yd_ref, sre_ref, sim_ref = refs[p + 3:p + 6]
    p += 6
    if chained:
        car_ref, cai_ref = refs[p:p + 2]
    G = R // T
    xs = xs_ref[...]
    xb = xs.astype(BF16)
    tpos = lax.broadcasted_iota(I32, (R, LANES), 0) % SUBLANES
    sub_pos = lax.broadcasted_iota(I32, (SUBLANES, LANES), 0)
    levels = _scan_levels(SUBLANES)

    if chained:
        c = pl.program_id(1)

        @pl.when(c == 0)
        def _():
            car_ref[...] = jnp.zeros(car_ref.shape, F32)
            cai_ref[...] = jnp.zeros(cai_ref.shape, F32)

    y = d_ref[...] * xs
    for cb in range(NB):
        hr = _dot(xb, bre_ref[cb])
        hi = _dot(xb, bim_ref[cb])
        if not chained:
            ar = lre_ref[cb][0:1, :]
            ai = lim_ref[cb][0:1, :]
            h0r = jnp.broadcast_to(h0r_ref[:, cb:cb + 1, :], (G, T, LANES)).reshape(R, LANES)
            h0i = jnp.broadcast_to(h0i_ref[:, cb:cb + 1, :], (G, T, LANES)).reshape(R, LANES)
            first = tpos == 0
            hr = hr + jnp.where(first, ar * h0r - ai * h0i, 0.0)
            hi = hi + jnp.where(first, ar * h0i + ai * h0r, 0.0)
        for k, s in enumerate(levels):
            ar = jnp.where(sub_pos >= s, lre_ref[cb][k:k + 1, :], 0.0)
            ai = jnp.where(sub_pos >= s, lim_ref[cb][k:k + 1, :], 0.0)
            ar = jnp.concatenate([ar] * (R // SUBLANES), axis=0)
            ai = jnp.concatenate([ai] * (R // SUBLANES), axis=0)
            sr = pltpu.roll(hr, s, 0)
            si = pltpu.roll(hi, s, 0)
            hr, hi = hr + ar * sr - ai * si, hi + ar * si + ai * sr
        if chained:
            cr = car_ref[cb:cb + 1, :]
            ci = cai_ref[cb:cb + 1, :]
            pwr = pre_ref[cb]
            pwi = pim_ref[cb]
            grs, gis = [], []
            for v in range(R // SUBLANES):
                rows = slice(v * SUBLANES, (v + 1) * SUBLANES)
                cbr = jnp.broadcast_to(cr, (SUBLANES, LANES))
                cbi = jnp.broadcast_to(ci, (SUBLANES, LANES))
                gr = hr[rows] + pwr * cbr - pwi * cbi
                gi = hi[rows] + pwr * cbi + pwi * cbr
                cr = gr[SUBLANES - 1:SUBLANES, :]
                ci = gi[SUBLANES - 1:SUBLANES, :]
                grs.append(gr)
                gis.append(gi)
            hr = jnp.concatenate(grs, axis=0)
            hi = jnp.concatenate(gis, axis=0)
            car_ref[cb:cb + 1, :] = cr
            cai_ref[cb:cb + 1, :] = ci
        else:
            sre_ref[:, cb:cb + 1, :] = hr.reshape(G, T, LANES)[:, T - 1:T, :]
            sim_ref[:, cb:cb + 1, :] = hi.reshape(G, T, LANES)[:, T - 1:T, :]
        y = y + _dot(hr.astype(BF16), cre_ref[cb]) - _dot(hi.astype(BF16), cim_ref[cb])

    z = _gelu_tanh(y)
    out = z * jax.nn.sigmoid(_dot(z.astype(BF16), gw_ref[...]) + gb_ref[...])
    yd_ref[...] = out.astype(yd_ref.dtype)

    if chained:
        @pl.when(c == pl.num_programs(1) - 1)
        def _():
            sre_ref[0] = car_ref[...]
            sim_ref[0] = cai_ref[...]


def ssm_mixer(z, xs_block, row0, Bt, T, consts, h0):
    (bre3, bim3, cre3, cim3, lre3, lim3, pre3, pim3, dvec, gw, gb) = consts
    NB = bre3.shape[0]
    W = dvec.shape[1]
    chained = h0 is None
    assert chained or T == SUBLANES
    R = SSM_CHUNK if chained else SSM_SEQ_ROWS
    rb0 = row0 // R

    def full(a):
        nd = a.ndim
        return pl.BlockSpec(a.shape, lambda *_: (0,) * nd)

    if chained:
        nchunk = T // R
        grid = (Bt, nchunk)
        xs_spec = pl.BlockSpec((R, W), lambda b, c: (rb0 + b * nchunk + c, xs_block))
        st_args, st_specs = [pre3, pim3], [full(pre3), full(pim3)]
        yd_spec = pl.BlockSpec((R, W), lambda b, c: (b * nchunk + c, 0))
        s_spec = pl.BlockSpec((1, NB, LANES), lambda b, c: (b, 0, 0))
        scratch = [pltpu.VMEM((NB, LANES), F32), pltpu.VMEM((NB, LANES), F32)]
        sem = ("arbitrary", "arbitrary")
        Tk = R
    else:
        G = R // T
        grid = (Bt // G,)
        xs_spec = pl.BlockSpec((R, W), lambda i: (rb0 + i, xs_block))
        st_args = list(h0)
        st_specs = [pl.BlockSpec((G, NB, LANES), lambda i: (i, 0, 0))] * 2
        yd_spec = pl.BlockSpec((R, W), lambda i: (i, 0))
        s_spec = pl.BlockSpec((G, NB, LANES), lambda i: (i, 0, 0))
        scratch = []
        sem = ("arbitrary",)
        Tk = T
    shared = [bre3, bim3, cre3, cim3, lre3, lim3]
    tailc = [dvec, gw, gb]
    s_shape = jax.ShapeDtypeStruct((Bt, NB, LANES), F32)
    return pl.pallas_call(
        functools.partial(_ssm_kernel, R=R, T=Tk, NB=NB, chained=chained),
        grid=grid,
        in_specs=[xs_spec] + [full(a) for a in shared] + st_specs + [full(a) for a in tailc],
        out_specs=[yd_spec, s_spec, s_spec],
        out_shape=[jax.ShapeDtypeStruct((Bt * T, W), BF16), s_shape, s_shape],
        scratch_shapes=scratch,
        compiler_params=_cparams(sem),
        name="ssm_chained" if chained else "ssm_stateful",
    )(z, *shared, *st_args, *tailc)


def _sort_keys(score):
    b = pltpu.bitcast(score, I32)
    key = jnp.where(b < 0, b ^ jnp.int32(0x7FFFFFFF), b)
    return jnp.where(key == -1, 0, key)


def _kth_largest_key(key, k):
    rows = key.shape[0]

    def body(i, t):
        cand = t + (jnp.int32(1) << (31 - i))
        cnt = jnp.sum(jnp.where(key >= cand, 1.0, 0.0), axis=-1, keepdims=True)
        return jnp.where(cnt >= float(k), cand, t)

    return lax.fori_loop(0, 32, body, jnp.full((rows, 1), -2 ** 31, I32))


def _blocked_prefix(eq, tri, offset):
    outs = []
    run = offset
    for j in range(eq.shape[1] // LANES):
        blk = eq[:, j * LANES:(j + 1) * LANES].astype(BF16)
        pj = _dot(blk, tri) + run
        outs.append(pj)
        run = pj[:, LANES - 1:LANES]
    return outs, run


def _attn_prompt_kernel(q_ref, iq_ref, tail_ref, k_ref, v_ref, ik_ref, gq_ref, e_ref, tri_ref, o_ref,
                        kb, vb, ikb, bias, *, TQ, L, dh, iw_off, n_sel, n_buckets):
    qb = pl.program_id(1)

    @pl.when(qb == 0)
    def _():
        kb[...] = k_ref[0].astype(BF16)
        vb[...] = v_ref[0].astype(BF16)
        ikb[...] = ik_ref[0].astype(BF16)

    scale = dh ** -0.5
    q = q_ref[...]
    q2 = q * q
    hi = q2.astype(BF16)
    lo = (q2 - hi.astype(F32)).astype(BF16)
    ss = _dot(hi, e_ref[...]) + _dot(lo, e_ref[...])
    qn = (q * lax.rsqrt(ss * (1.0 / dh) + EPS) * gq_ref[...] * scale).astype(BF16)
    iqs = (iq_ref[...] * scale).astype(BF16)
    iw = tail_ref[:, iw_off:iw_off + IDX_HEADS] * (IDX_HEADS ** -0.5)

    def attend(Lk):
        score = None
        for h in range(IDX_HEADS):
            lg = _dot_nt(iqs[:, h * dh:(h + 1) * dh], ikb[0:Lk, :])
            t = jnp.maximum(lg, 0.0) * iw[:, h:h + 1]
            score = t if score is None else score + t
        col = lax.broadcasted_iota(I32, (TQ, Lk), 1)
        qpos = qb * TQ + lax.broadcasted_iota(I32, (TQ, Lk), 0)
        causal = col <= qpos
        score = jnp.where(causal, score, NEG_INF)

        key = _sort_keys(score)
        thr = _kth_largest_key(key, n_sel)
        gt = key > thr
        eq = key == thr
        need = float(n_sel) - jnp.sum(jnp.where(gt, 1.0, 0.0), axis=-1, keepdims=True)
        pref, _ = _blocked_prefix(jnp.where(eq, 1.0, 0.0), tri_ref[...], jnp.zeros((TQ, 1), F32))
        pref = jnp.concatenate(pref, axis=1)
        sel = (gt | (eq & (pref <= need))) & causal
        bias[:, 0:Lk] = jnp.where(sel, 0.0, NEG_INF)

        for h in range(N_HEADS_C):
            lg = _dot_nt(qn[:, h * dh:(h + 1) * dh], kb[0:Lk, :]) + bias[:, 0:Lk]
            m = jnp.max(lg, axis=-1, keepdims=True)
            e = jnp.exp(lg - m)
            den = jnp.sum(e, axis=-1, keepdims=True)
            o = _dot(e.astype(BF16), vb[0:Lk, :]) / den
            o_ref[:, h * dh:(h + 1) * dh] = o.astype(o_ref.dtype)

    per = (L // TQ) // n_buckets
    for bk in range(n_buckets):
        pl.when(qb // per == bk)(functools.partial(attend, (bk + 1) * per * TQ))


def attn_prompt(z, q_block, iq_block, tail_block, tail_w, iw_off, kn, v, ik, gq_tiled, B, T, W, dh, TQ=256,
                n_buckets=4):
    nq = T // TQ
    n_sel = min(TOPK_MAX, T // 4)
    eye = (jnp.arange(W)[:, None] // dh == jnp.arange(W)[None, :] // dh).astype(BF16)
    tri = (jnp.arange(LANES)[:, None] <= jnp.arange(LANES)[None, :]).astype(BF16)
    kspec = pl.BlockSpec((1, T, dh), lambda b, i: (b, 0, 0))
    return pl.pallas_call(
        functools.partial(_attn_prompt_kernel, TQ=TQ, L=T, dh=dh, iw_off=iw_off, n_sel=n_sel, n_buckets=n_buckets),
        grid=(B, nq),
        in_specs=[pl.BlockSpec((TQ, W), lambda b, i: (b * nq + i, q_block)),
                  pl.BlockSpec((TQ, W), lambda b, i: (b * nq + i, iq_block)),
                  pl.BlockSpec((TQ, tail_w), lambda b, i: (b * nq + i, tail_block)),
                  kspec, kspec, kspec,
                  pl.BlockSpec((1, W), lambda b, i: (0, 0)),
                  pl.BlockSpec((W, W), lambda b, i: (0, 0)),
                  pl.BlockSpec((LANES, LANES), lambda b, i: (0, 0))],
        out_specs=pl.BlockSpec((TQ, W), lambda b, i: (b * nq + i, 0)),
        out_shape=jax.ShapeDtypeStruct((B * T, W), BF16),
        scratch_shapes=[pltpu.VMEM((T, dh), BF16), pltpu.VMEM((T, dh), BF16), pltpu.VMEM((T, dh), BF16),
                        pltpu.VMEM((TQ, T), F32)],
        compiler_params=_cparams(("arbitrary", "arbitrary")),
        name="attn_prompt",
    )(z, z, z, kn, v, ik, gq_tiled, eye, tri)


def _xattn_kernel(q_ref, mk_ref, mv_ref, g_ref, o_ref, *, G, T, dh, scale):
    g = g_ref[...]
    for s in range(G):
        rows = slice(s * T, (s + 1) * T)
        for h in range(X_HEADS):
            cols = slice(h * dh, (h + 1) * dh)
            qh = q_ref[rows, cols]
            ms = jnp.mean(qh * qh, axis=-1, keepdims=True)
            qn = (qh * lax.rsqrt(ms + EPS) * g).astype(BF16)
            lg = _dot_nt(qn, mk_ref[s, :, cols].astype(BF16)) * scale
            m = jnp.max(lg, axis=-1, keepdims=True)
            e = jnp.exp(lg - m)
            den = jnp.sum(e, axis=-1, keepdims=True)
            o = _dot(e.astype(BF16), mv_ref[s, :, cols].astype(BF16)) / den
            o_ref[rows, cols] = o.astype(o_ref.dtype)


def _xattn_rows_kernel(q_ref, mk_ref, mv_ref, g_ref, o_ref, *, G, T, dh, scale):
    g = g_ref[...]
    H = X_HEADS
    n = mk_ref.shape[1]
    own = (lax.broadcasted_iota(I32, (H * T, n), 1) % H) == (lax.broadcasted_iota(I32, (H * T, n), 0) // T)
    for s in range(G):
        rows = slice(s * T, (s + 1) * T)
        parts = []
        for h in range(H):
            qh = q_ref[rows, h * dh:(h + 1) * dh]
            ms = jnp.mean(qh * qh, axis=-1, keepdims=True)
            parts.append(qh * lax.rsqrt(ms + EPS) * g)
        qs = jnp.concatenate(parts, axis=0).astype(BF16)
        lg = jnp.where(own, _dot_nt(qs, mk_ref[s].astype(BF16)) * scale, NEG_INF)
        m = jnp.max(lg, axis=-1, keepdims=True)
        e = jnp.exp(lg - m)
        den = jnp.sum(e, axis=-1, keepdims=True)
        o = _dot(e.astype(BF16), mv_ref[s].astype(BF16)) / den
        for h in range(H):
            o_ref[rows, h * dh:(h + 1) * dh] = o[h * T:(h + 1) * T, :].astype(o_ref.dtype)


def cross_attention(qx, row0, Bt, T, G, TQ, mem_k, mem_v, layer, gq):
    Wx = qx.shape[1]
    dh = Wx // X_HEADS
    n_rows, wm = mem_k.shape[2:]
    assert G == 1 or TQ == T
    R = G * TQ
    nt = T // TQ
    rb0 = row0 // R
    mspec = pl.BlockSpec((None, G, n_rows, wm), lambda i, t: (layer, i, 0, 0))
    body = _xattn_kernel if wm == Wx else _xattn_rows_kernel
    return pl.pallas_call(
        functools.partial(body, G=G, T=TQ, dh=dh, scale=dh ** -0.5),
        grid=(Bt // G, nt),
        in_specs=[pl.BlockSpec((R, Wx), lambda i, t: (rb0 + i * nt + t, 0)), mspec, mspec,
                  pl.BlockSpec((1, dh), lambda i, t: (0, 0))],
        out_specs=pl.BlockSpec((R, Wx), lambda i, t: (i * nt + t, 0)),
        out_shape=jax.ShapeDtypeStruct((Bt * T, Wx), BF16),
        compiler_params=_cparams(("arbitrary", "arbitrary")),
        name="cross_attention",
    )(qx, mem_k, mem_v, gq.reshape(1, dh))


def _attn_sample_kernel(pt_ref, q_ref, iq_ref, tail_ref, kn_ref, vn_ref, ikn_ref, ck_hbm, cv_hbm, cik_hbm,
                        gq_ref, e_ref, tri_ref, o_ref,
                        kp, vp, ikp, qs_scr, iqs_scr, iw_scr, score, bias, sems,
                        *, G, T, dh, layer, n_pages, hp, iw_off, n_sel):
    step = pl.program_id(0)
    R = G * T
    H = N_HEADS_C
    LP = n_pages * hp
    LC = LP + LANES

    def page_copies(g, p):
        page = pt_ref[step * G + g, p]
        return [pltpu.make_async_copy(src.at[layer, page], dst.at[g, :, p * hp:(p + 1) * hp], sems.at[s, g])
                for s, (src, dst) in enumerate(((ck_hbm, kp), (cv_hbm, vp), (cik_hbm, ikp)))]

    def start_seq(g, c):
        for p in range(n_pages):
            for cp in page_copies(g, p):
                cp.start(priority=p % 2)
        return c

    def wait_seq(g):
        for p in range(n_pages):
            for cp in page_copies(g, p):
                cp.wait()

    lax.fori_loop(0, G, start_seq, 0)

    def padded_new(ref, rows):
        return jnp.concatenate([ref[rows, :], jnp.zeros((LANES - T, dh), F32)], axis=0).astype(BF16)

    scale = dh ** -0.5
    q = q_ref[...]
    q2 = q * q
    hi = q2.astype(BF16)
    lo = (q2 - hi.astype(F32)).astype(BF16)
    ss = _dot(hi, e_ref[...]) + _dot(lo, e_ref[...])
    qs_scr[...] = q * lax.rsqrt(ss * (1.0 / dh) + EPS) * gq_ref[...] * scale
    iqs_scr[...] = iq_ref[...] * scale
    iw_blk = (iw_off // LANES) * LANES
    iw_lane = iw_off - iw_blk
    iw_scr[...] = tail_ref[:, iw_blk:iw_blk + LANES]

    new_ok = lax.broadcasted_iota(I32, (R, LANES), 1) <= lax.broadcasted_iota(I32, (R, LANES), 0) % T

    def stack_heads(ref, rows):
        return jnp.concatenate([ref[rows, h * dh:(h + 1) * dh] for h in range(H)], axis=0).astype(BF16)

    def score_body(g, c):
        wait_seq(g)
        rows = pl.ds(pl.multiple_of(g * T, T), T)
        qs = stack_heads(iqs_scr, rows)
        lg_p = _dot(qs, ikp[g].astype(BF16))
        lg_n = _dot_nt(qs, padded_new(ikn_ref, rows))
        iw = iw_scr[rows, :] * (IDX_HEADS ** -0.5)
        sp = sn = None
        for h in range(IDX_HEADS):
            w = iw[:, iw_lane + h:iw_lane + h + 1]
            hs = slice(h * T, (h + 1) * T)
            tp = jnp.maximum(lg_p[hs], 0.0) * w
            tn = jnp.maximum(lg_n[hs], 0.0) * w
            sp, sn = (tp, tn) if sp is None else (sp + tp, sn + tn)
        score[rows, 0:LP] = sp
        score[rows, LP:LC] = sn
        return c

    lax.fori_loop(0, G, score_body, 0, unroll=2)
    score[:, LP:LC] = jnp.where(new_ok, score[:, LP:LC], NEG_INF)

    key = _sort_keys(score[...])
    thr = _kth_largest_key(key, n_sel)
    gt = key > thr
    eq = key == thr
    need = float(n_sel) - jnp.sum(jnp.where(gt, 1.0, 0.0), axis=-1, keepdims=True)
    pref, _ = _blocked_prefix(jnp.where(eq, 1.0, 0.0), tri_ref[...], jnp.zeros((R, 1), F32))
    pref = jnp.concatenate(pref, axis=1)
    bias[...] = jnp.where(gt | (eq & (pref <= need)), 0.0, NEG_INF)
    bias[:, LP:LC] = jnp.where(new_ok, bias[:, LP:LC], NEG_INF)

    def attn_body(g, c):
        rows = pl.ds(pl.multiple_of(g * T, T), T)
        qs = stack_heads(qs_scr, rows)
        bh = jnp.concatenate([bias[rows, :]] * H, axis=0)
        lg_p = _dot(qs, kp[g].astype(BF16)) + bh[:, 0:LP]
        lg_n = _dot_nt(qs, padded_new(kn_ref, rows)) + bh[:, LP:LC]
        m = jnp.maximum(jnp.max(lg_p, axis=-1, keepdims=True), jnp.max(lg_n, axis=-1, keepdims=True))
        ep = jnp.exp(lg_p - m)
        en = jnp.exp(lg_n - m)
        den = jnp.sum(ep, axis=-1, keepdims=True) + jnp.sum(en, axis=-1, keepdims=True)
        o = (_dot_nt(ep.astype(BF16), vp[g].astype(BF16)) + _dot(en.astype(BF16), padded_new(vn_ref, rows))) / den
        for h in range(H):
            o_ref[rows, h * dh:(h + 1) * dh] = o[h * T:(h + 1) * T, :].astype(o_ref.dtype)
        return c

    lax.fori_loop(0, G, attn_body, 0, unroll=2)


def attn_sample(z, row0, q_block, iq_block, tail_block, tail_w, iw_off, kn, v, ik, caches, page_table, layer,
                gq_tiled, Bs, T, W, dh, G=16):
    n_pages = page_table.shape[1]
    page = caches[0].shape[2]
    caches_t = [c.transpose(0, 1, 3, 2) for c in caches]
    R = G * T
    rb0 = row0 // R
    LP = n_pages * page
    LC = LP + LANES
    n_sel = min(TOPK_MAX, (LP + T) // 4)
    eye = (jnp.arange(W)[:, None] // dh == jnp.arange(W)[None, :] // dh).astype(BF16)
    tri = (jnp.arange(LANES)[:, None] <= jnp.arange(LANES)[None, :]).astype(BF16)
    any_spec = pl.BlockSpec(memory_space=pl.ANY)
    nspec = pl.BlockSpec((R, dh), lambda i, pt: (rb0 + i, 0))
    grid_spec = pltpu.PrefetchScalarGridSpec(
        num_scalar_prefetch=1,
        grid=(Bs // G,),
        in_specs=[pl.BlockSpec((R, W), lambda i, pt: (rb0 + i, q_block)),
                  pl.BlockSpec((R, W), lambda i, pt: (rb0 + i, iq_block)),
                  pl.BlockSpec((R, tail_w), lambda i, pt: (rb0 + i, tail_block)),
                  nspec, nspec, nspec, any_spec, any_spec, any_spec,
                  pl.BlockSpec((1, W), lambda i, pt: (0, 0)),
                  pl.BlockSpec((W, W), lambda i, pt: (0, 0)),
                  pl.BlockSpec((LANES, LANES), lambda i, pt: (0, 0))],
        out_specs=pl.BlockSpec((R, W), lambda i, pt: (i, 0)),
        scratch_shapes=[pltpu.VMEM((G, dh, LP), F32), pltpu.VMEM((G, dh, LP), F32), pltpu.VMEM((G, dh, LP), F32),
                        pltpu.VMEM((R, W), F32), pltpu.VMEM((R, W), F32), pltpu.VMEM((R, LANES), F32),
                        pltpu.VMEM((R, LC), F32), pltpu.VMEM((R, LC), F32),
                        pltpu.SemaphoreType.DMA((3, G))],
    )
    return pl.pallas_call(
        functools.partial(_attn_sample_kernel, G=G, T=T, dh=dh, layer=layer, n_pages=n_pages, hp=page,
                          iw_off=iw_off, n_sel=n_sel),
        grid_spec=grid_spec,
        out_shape=jax.ShapeDtypeStruct((Bs * T, W), BF16),
        compiler_params=_cparams(("arbitrary",)),
        name="attn_sample",
    )(page_table, z, z, z, kn, v, ik, *caches_t, gq_tiled, eye, tri)


def _block_diag(blocks):
    G, a, b = blocks.shape
    eye = jnp.eye(G, dtype=blocks.dtype)
    return (eye[:, None, :, None] * blocks[:, :, None, :]).reshape(G * a, G * b)


def _lane_blocks(a):
    rows, n = a.shape
    return a.reshape(rows, n // LANES, LANES).transpose(1, 0, 2)


def kernel(x_prompt, x_sample, cache_attn_k, cache_attn_v, cache_idx_k, cache_mem_k, cache_mem_v, state_pool,
           state_conv, state_ssm_re, state_ssm_im, page_table, mem_prompt, norm_g, ffn_in, ffn_out, w_in,
           q_norm_g, k_norm_g, pool_mix, pool_scale, conv_w, ssm_a_re, ssm_a_im, ssm_log_step, ssm_b_re,
           ssm_b_im, ssm_c_re, ssm_c_im, ssm_d, ssm_glu_w, ssm_glu_b, w_branch, w_gate, b_gate, w_o,
           mem_norm_g, w_xq, w_xk, w_xv, xq_norm_g, xk_norm_g, w_xo):
    B, T, D = x_prompt.shape
    Bs, Ts, _ = x_sample.shape
    depth = norm_g.shape[0]
    Mp, Ms = B * T, Bs * Ts
    W = pool_scale.shape[1]
    dh = k_norm_g.shape[1]
    d_ff = ffn_out.shape[2]
    n_mem = mem_prompt.shape[1]
    Wx = w_xq.shape[2]
    SG, SN = ssm_a_re.shape[1:]
    NB = SG * SN // LANES
    past_len = page_table.shape[1] * cache_attn_k.shape[2]
    assert cache_idx_k.shape[-1] == dh and W == N_HEADS_C * dh == IDX_HEADS * dh
    TM = 1024

    o_k = 5 * W
    o_iq = o_k + 2 * dh
    o_ik = o_iq + W
    o_xs = o_ik + dh + IDX_HEADS
    tail_w = 2 * LANES
    Q_BLK, IQ_BLK, XS_BLK = 4, 5, 6
    TAIL_BLK = 7 * W // tail_w
    IW_OFF = 3 * dh
    n_z = 7 * W + tail_w

    h = (x_prompt.reshape(Mp, D), x_sample.reshape(Ms, D))
    caches = (cache_attn_k, cache_attn_v, cache_idx_k)
    mem_rows = mem_prompt.reshape(B * n_mem, D)
    outs = [[] for _ in range(16)]

    def ffn(h, l, i, g, split_out_rows=None):
        act = matmul(h, [(ffn_in, (l, i), 0), (ffn_in, (l, i), d_ff // 512)], n_out=d_ff, tn=512, tm=TM,
                     mode="swiglu", out_dtype=BF16, norm_g=g, name="ffn_in")
        return matmul(act, [(ffn_out, (l, i), 0)], n_out=D, tn=512, tm=TM, mode="residual", aux=h, scale=0.5,
                      split_out_rows=split_out_rows, name="ffn_out")

    for l in range(depth):
        h = ffn(h, l, 0, norm_g[l, 0])

        wl = w_in[l]
        w_in2 = jnp.concatenate([wl[:, 0:o_k], wl[:, o_iq:o_iq + W], wl[:, o_xs:o_xs + W], wl[:, o_k:o_k + 2 * dh],
                                 wl[:, o_ik:o_ik + dh + IDX_HEADS],
                                 jnp.zeros((D, tail_w - 3 * dh - IDX_HEADS), F32)], axis=1)
        z = matmul(h, [(w_in2, (), 0)], n_out=n_z, tn=1280, tm=TM, norm_g=norm_g[l, 1], name="in_proj")

        wmix = pool_mix[l].astype(BF16)
        ya_p, yb_p, npool_p, nconv_p = pool_conv(z, 0, B, T, 1, 512, 0, None, None, wmix, pool_scale[l], conv_w[l], W)
        ya_s, yb_s, npool_s, nconv_s = pool_conv(z, Mp, Bs, Ts, 16, Ts, past_len, state_pool[l], state_conv[l],
                                                 wmix, pool_scale[l], conv_w[l], W)

        kn, vv, ik = kv_prep(z, TAIL_BLK, tail_w, k_norm_g[l], dh)
        gq_tiled = jnp.tile(q_norm_g[l], N_HEADS_C).reshape(1, W)
        yc_p = attn_prompt(z, Q_BLK, IQ_BLK, TAIL_BLK, tail_w, IW_OFF, kn[:Mp].reshape(B, T, dh),
                           vv[:Mp].reshape(B, T, dh), ik[:Mp].reshape(B, T, dh), gq_tiled, B, T, W, dh)
        yc_s = attn_sample(z, Mp, Q_BLK, IQ_BLK, TAIL_BLK, tail_w, IW_OFF, kn, vv, ik, caches, page_table, l,
                           gq_tiled, Bs, Ts, W, dh)

        ab_re, ab_im, co_re, co_im = ssm_params(ssm_a_re[l], ssm_a_im[l], ssm_log_step[l])
        flat = lambda a: a.reshape(1, SG * SN)
        lev_re, lev_im, pw_re, pw_im = ssm_tables(flat(ab_re), flat(ab_im), SUBLANES)
        bb_re, bb_im = ssm_fold_input(_block_diag(ssm_b_re[l].transpose(0, 2, 1)),
                                      _block_diag(ssm_b_im[l].transpose(0, 2, 1)), flat(co_re), flat(co_im))
        consts = (
            _lane_blocks(bb_re), _lane_blocks(bb_im),
            _block_diag(ssm_c_re[l].transpose(0, 2, 1)).reshape(NB, LANES, W).astype(BF16),
            _block_diag(ssm_c_im[l].transpose(0, 2, 1)).reshape(NB, LANES, W).astype(BF16),
            _lane_blocks(lev_re), _lane_blocks(lev_im),
            _lane_blocks(pw_re), _lane_blocks(pw_im),
            ssm_d[l].reshape(1, W), ssm_glu_w[l].astype(BF16), ssm_glu_b[l].reshape(1, W))
        yd_p, sre_p, sim_p = ssm_mixer(z, XS_BLK, 0, B, T, consts, None)
        h0 = (state_ssm_re[l].reshape(Bs, NB, LANES), state_ssm_im[l].reshape(Bs, NB, LANES))
        yd_s, sre_s, sim_s = ssm_mixer(z, XS_BLK, Mp, Bs, Ts, consts, h0)

        merged = gated_merge(h, norm_g[l, 1], [(ya_p, ya_s), (yb_p, yb_s), (yc_p, yc_s), (yd_p, yd_s)],
                             w_gate, b_gate, w_branch, l)
        h = matmul(merged, [(w_o, (l,), 0)], n_out=D, tn=1024, tm=TM, mode="residual", aux=h, scale=1.0, name="w_o")

        mn = rmsnorm_rows(mem_rows, mem_norm_g[l])
        mk = matmul(mn, [(w_xk, (l,), 0)], n_out=Wx, tn=Wx, tm=B * n_mem, mode="headnorm",
                    aux=jnp.tile(xk_norm_g[l], X_HEADS).reshape(1, Wx), group=Wx // X_HEADS, name="mem_k")
        mv = matmul(mn, [(w_xv, (l,), 0)], n_out=Wx, tn=Wx, tm=B * n_mem, name="mem_v")
        qx = matmul(h, [(w_xq, (l,), 0)], n_out=Wx, tn=Wx, tm=TM, norm_g=norm_g[l, 2], name="w_xq")
        xa_p = cross_attention(qx, 0, B, T, 1, 512, mk.reshape(1, B, n_mem, Wx), mv.reshape(1, B, n_mem, Wx), 0,
                               xq_norm_g[l])
        xa_s = cross_attention(qx, Mp, Bs, Ts, 8, Ts, cache_mem_k.reshape(depth, Bs, n_mem * X_HEADS, Wx // X_HEADS),
                               cache_mem_v.reshape(depth, Bs, n_mem * X_HEADS, Wx // X_HEADS), l, xq_norm_g[l])
        h = matmul((xa_p, xa_s), [(w_xo, (l,), 0)], n_out=D, tn=1024, tm=TM, mode="residual", aux=h, scale=1.0,
                   name="w_xo")

        h = ffn(h, l, 1, norm_g[l, 3], split_out_rows=Mp if l == depth - 1 else None)

        xh = Wx // X_HEADS
        layer_out = (kn[:Mp].reshape(B, T, dh), vv[:Mp].reshape(B, T, dh), ik[:Mp].reshape(B, T, dh),
                     mk.reshape(B, n_mem, X_HEADS, xh), mv.reshape(B, n_mem, X_HEADS, xh), npool_p, nconv_p,
                     sre_p.reshape(B, SG, SN), sim_p.reshape(B, SG, SN),
                     kn[Mp:].reshape(Bs, Ts, dh), vv[Mp:].reshape(Bs, Ts, dh), ik[Mp:].reshape(Bs, Ts, dh),
                     npool_s, nconv_s, sre_s.reshape(Bs, SG, SN), sim_s.reshape(Bs, SG, SN))
        for acc, val in zip(outs, layer_out):
            acc.append(val)

    return (h[0].reshape(B, T, D), h[1].reshape(Bs, Ts, D)) + tuple(jnp.stack(o) for o in outs)
```
